```python
import math
import jax
import jax.numpy as jnp
from jax import lax
import numpy as np

D_MODEL = 2048
BATCH = 8
SEQ = 2048
DEPTH = 2

GRID_W = 64
CTX_LEN = 256
EPS = 1e-6

MIX_A = D_MODEL // 2
CHUNK = 128
A_GROUPS = 8
A_GROUP_DIM = MIX_A // A_GROUPS
MIX_B = D_MODEL - MIX_A
B_HEAD = 64
B_HEADS = MIX_B // B_HEAD
DECAY_LORA = 64
AAA_LORA = 64
GATE_LORA = 160
B_WIDTH = 3 * MIX_B + DECAY_LORA + AAA_LORA + GATE_LORA
B_SPLITS = (MIX_B, 2 * MIX_B, 3 * MIX_B, 3 * MIX_B + DECAY_LORA, 3 * MIX_B + DECAY_LORA + AAA_LORA)
IN_EVEN = 2 * MIX_A + B_WIDTH
GN_EPS = B_HEAD * 1e-5

C_HEADS = 16
C_HEAD = D_MODEL // C_HEADS // 2
ROPE_AXIS = C_HEAD // 2
ROPE_THETA = 10000.0
Q_BLOCK = 128
SUBLN_EPS = 1e-5

D_FF = 5632
N_EXPERTS = 8
TOP_K = 2
D_EXPERT = 7168

kernel_name = 'hybrid_dit_gmlp_rwkv7_diffattn_moe'


def rmsnorm(x, g, eps=EPS):
    xf = x.astype(jnp.float32)
    y = xf * lax.rsqrt(jnp.mean(xf * xf, axis=-1, keepdims=True) + eps)
    return (y * g.astype(jnp.float32)).astype(x.dtype)


def ada(cond, w, b):
    m = (jax.nn.silu(cond) @ w + b).reshape(cond.shape[:-1] + (6, D_MODEL))
    if cond.ndim == 2:
        m = m[:, None]
    return tuple(m[..., i, :] for i in range(6))


def modulate(h, shift, scale):
    return h * (1.0 + scale) + shift


def swiglu(h, w1, w3, w2):
    return (jax.nn.silu(h @ w1) * (h @ w3)) @ w2


def lambda_init(layer):
    return 0.8 - 0.6 * math.exp(-0.3 * layer)


def chunk_sgu(z, norm_g, ws, bs):
    bsz, t, _ = z.shape
    u = jax.nn.gelu(z[..., :MIX_A])
    v = jax.nn.gelu(z[..., MIX_A:]).reshape(bsz, t // CHUNK, CHUNK, A_GROUPS, A_GROUP_DIM)
    v = rmsnorm(v, norm_g)
    s = jnp.einsum('gpq,bnqgc->bnpgc', ws, v) + bs.T[:, :, None]
    return u * s.reshape(bsz, t, MIX_A)


def centred_shift(z, w):
    zp = jnp.pad(z, ((0, 0), (1, 1), (0, 0)))
    return w[0] * zp[:, :-2] + w[1] * zp[:, 1:-1] + w[2] * zp[:, 2:]


def _wkv_step(S, inp):
    r, decay, k, v, kk, a = inp
    sk = jnp.einsum('bhvk,bhk->bhv', S, kk)
    S = S * decay[:, :, None, :] - sk[..., None] * (kk * a)[:, :, None, :] + v[..., None] * k[:, :, None, :]
    return S, jnp.einsum('bhvk,bhk->bhv', S, r)


def rwkv7_bidir(z_ctx, z_lat, shift_w, w0, w_up, a0, a_up, g_up, k_k, k_a, r_k, lnx_w, lnx_b):
    f32 = jnp.float32

    def features(z):
        bsz, t, _ = z.shape
        z = centred_shift(z, shift_w).astype(f32)
        r, k, v, wd, ad, gd = jnp.split(z, B_SPLITS, axis=-1)
        heads = lambda a: a.reshape(bsz, t, B_HEADS, B_HEAD)
        g = jax.nn.sigmoid(gd) @ g_up.astype(f32)
        kk = heads(k * k_k)
        kk = kk / jnp.maximum(jnp.sqrt(jnp.sum(kk * kk, axis=-1, keepdims=True)), 1e-12)
        dirs = []
        for d in range(2):
            wlog = -jax.nn.softplus(-(w0[d] + jnp.tanh(wd) @ w_up[d])) - 0.5
            a = jax.nn.sigmoid(a0[d] + ad @ a_up[d])
            kd = k * (1.0 + (a - 1.0) * k_a)
            dirs.append((heads(jnp.exp(-jnp.exp(wlog))), heads(a), heads(kd)))
        return heads(r), heads(v), kk, g, dirs

    def wkv(S0, r, v, kk, decay, a, kd, reverse):
        tm = lambda a_: jnp.moveaxis(a_, 1, 0)
        S, y = lax.scan(_wkv_step, S0, (tm(r), tm(decay), tm(kd), tm(v), tm(kk), tm(a)), reverse=reverse)
        return S, jnp.moveaxis(y, 0, 1)

    def readout(r, v, g, ys, kds):
        y = ys[0] + ys[1]
        mu = jnp.mean(y, axis=-1, keepdims=True)
        var = jnp.mean(jnp.square(y - mu), axis=-1, keepdims=True)
        yn = ((y - mu) * lax.rsqrt(var + GN_EPS)).reshape(g.shape)
        out = yn * lnx_w + lnx_b
        bonus = (jnp.sum(r * kds[0] * r_k, axis=-1, keepdims=True)
                 + jnp.sum(r * kds[1] * r_k, axis=-1, keepdims=True)) * v
        return (out + bonus.reshape(g.shape)) * g

    rc, vc, kkc, gc, dc = features(z_ctx)
    rl, vl, kkl, gl, dl = features(z_lat)
    S0 = jnp.zeros((z_lat.shape[0], B_HEADS, B_HEAD, B_HEAD), f32)
    yc, yl = [], []
    for d, rev in enumerate((False, True)):
        decay, a, kd = dc[d]
        S_ctx, y = wkv(S0, rc, vc, kkc, decay, a, kd, rev)
        yc.append(y)
        decay, a, kd = dl[d]
        _, y = wkv(S_ctx, rl, vl, kkl, decay, a, kd, rev)
        yl.append(y)
    out_c = readout(rc, vc, gc, yc, (dc[0][2], dc[1][2]))
    out_l = readout(rl, vl, gl, yl, (dl[0][2], dl[1][2]))
    return out_c, out_l


def axial_rope(n):
    rows = n // GRID_W
    row = jnp.broadcast_to(jnp.arange(rows, dtype=jnp.float32)[:, None], (rows, GRID_W)).reshape(-1)
    col = jnp.broadcast_to(jnp.arange(GRID_W, dtype=jnp.float32)[None, :], (rows, GRID_W)).reshape(-1)
    inv = ROPE_THETA ** (-jnp.arange(0, ROPE_AXIS, 2, dtype=jnp.float32) / ROPE_AXIS)
    ar = row[:, None] * inv
    ac = col[:, None] * inv
    ang = jnp.concatenate([ar, ar, ac, ac], axis=-1)
    return jnp.cos(ang), jnp.sin(ang)


def apply_rope(x, cos, sin):
    xs = x.reshape(x.shape[:-1] + (2, 2, ROPE_AXIS // 2))
    rot = jnp.stack([-xs[..., 1, :], xs[..., 0, :]], axis=-2).reshape(x.shape)
    cos = cos[None, :, None, None, :]
    sin = sin[None, :, None, None, :]
    return (x * cos + rot * sin).astype(x.dtype)


def diff_attention(h, hc, w_qkv, q_g, k_g, lq1, lk1, lq2, lk2, subln_g, w_out, lam_init):
    bsz, t, _ = h.shape
    f32 = jnp.float32
    q, k, v = jnp.split(h @ w_qkv, 3, axis=-1)
    kc, vc = jnp.split(hc @ w_qkv[:, D_MODEL:], 2, axis=-1)
    sub = lambda a: a.reshape(a.shape[:2] + (C_HEADS, 2, C_HEAD))
    cos, sin = axial_rope(t)
    q = apply_rope(rmsnorm(sub(q), q_g), cos, sin)
    k = apply_rope(rmsnorm(sub(k), k_g), cos, sin)
    kc = rmsnorm(sub(kc), k_g)
    keys = jnp.concatenate([kc, k], axis=1)
    vals = jnp.concatenate([vc, v], axis=1).reshape(bsz, -1, C_HEADS, 2 * C_HEAD).astype(f32)
    lam = (jnp.exp(jnp.sum(lq1 * lk1).astype(f32)) - jnp.exp(jnp.sum(lq2 * lk2).astype(f32)) + lam_init)
    scale = C_HEAD ** -0.5

    def block(qb):
        s = jnp.einsum('bqhid,bkhid->bhiqk', qb, keys, preferred_element_type=f32) * scale
        p = jax.nn.softmax(s, axis=-1)
        attn = p[:, :, 0] - lam * p[:, :, 1]
        return jnp.einsum('bhqk,bkhe->bqhe', attn, vals)

    qb = q.reshape(bsz, t // Q_BLOCK, Q_BLOCK, C_HEADS, 2, C_HEAD).transpose(1, 0, 2, 3, 4, 5)
    o = lax.map(block, qb).transpose(1, 0, 2, 3, 4).reshape(bsz, t, C_HEADS, 2 * C_HEAD)
    o = rmsnorm(o, subln_g, SUBLN_EPS) * (1.0 - lam_init)
    return o.reshape(bsz, t, D_MODEL).astype(h.dtype) @ w_out


def moe(h, router, w1, w3, w2):
    bsz, t, d = h.shape
    hf = h.reshape(-1, d)
    logits = (hf @ router).astype(jnp.float32)
    top_v, top_i = lax.top_k(logits, TOP_K)
    gates = jax.nn.softmax(top_v, axis=-1)
    combine = jnp.sum(jax.nn.one_hot(top_i, N_EXPERTS, dtype=jnp.float32) * gates[..., None], axis=1)
    out = jnp.zeros(hf.shape, jnp.float32)
    for e in range(N_EXPERTS):
        out = out + combine[:, e:e + 1] * swiglu(hf, w1[e], w3[e], w2[e])
    return out.reshape(bsz, t, d).astype(h.dtype)


def even_layer(x, xc, c, c_ctx, ada_w, ada_b, n1, n2, w_in, sgu_g, sgu_w, sgu_b, shift_w, w0, w_up, a0, a_up,
               g_up, k_k, k_a, r_k, lnx_w, lnx_b, w_out, f1, f3, f2):
    ml = ada(c, ada_w, ada_b)
    mc = ada(c_ctx, ada_w, ada_b)
    hl = modulate(rmsnorm(x, n1), ml[0], ml[1]) @ w_in
    hc = modulate(rmsnorm(xc, n1), mc[0], mc[1]) @ w_in
    a_l = chunk_sgu(hl[..., :2 * MIX_A], sgu_g, sgu_w, sgu_b)
    a_c = chunk_sgu(hc[..., :2 * MIX_A], sgu_g, sgu_w, sgu_b)
    b_c, b_l = rwkv7_bidir(hc[..., 2 * MIX_A:], hl[..., 2 * MIX_A:], shift_w, w0, w_up, a0, a_up, g_up,
                           k_k, k_a, r_k, lnx_w, lnx_b)
    x = x + (ml[2] * (jnp.concatenate([a_l, b_l.astype(a_l.dtype)], axis=-1) @ w_out)).astype(x.dtype)
    xc = xc + (mc[2] * (jnp.concatenate([a_c, b_c.astype(a_c.dtype)], axis=-1) @ w_out)).astype(xc.dtype)
    x = x + (ml[5] * swiglu(modulate(rmsnorm(x, n2), ml[3], ml[4]), f1, f3, f2)).astype(x.dtype)
    xc = xc + (mc[5] * swiglu(modulate(rmsnorm(xc, n2), mc[3], mc[4]), f1, f3, f2)).astype(xc.dtype)
    return x, xc


def odd_layer(x, xc, c, c_ctx, lam_init, ada_w, ada_b, n1, n2, w_qkv, q_g, k_g, lq1, lk1, lq2, lk2, subln_g,
              w_out, router, e1, e3, e2):
    ml = ada(c, ada_w, ada_b)
    mc = ada(c_ctx, ada_w, ada_b)
    hl = modulate(rmsnorm(x, n1), ml[0], ml[1])
    hc = modulate(rmsnorm(xc, n1), mc[0], mc[1])
    x = x + (ml[2] * diff_attention(hl, hc, w_qkv, q_g, k_g, lq1, lk1, lq2, lk2, subln_g, w_out, lam_init)).astype(x.dtype)
    x = x + (ml[5] * moe(modulate(rmsnorm(x, n2), ml[3], ml[4]), router, e1, e3, e2)).astype(x.dtype)
    return x


def setup_inputs(seed: int = 0) -> dict:
    key = jax.random.key(seed)
    ks = iter(jax.random.split(key, 64))
    f32 = jnp.float32
    D = D_MODEL

    def nrm(shape, scale):
        return jax.random.normal(next(ks), shape, f32) * scale

    def gain(shape):
        return 1.0 + nrm(shape, 0.02)

    inp = {}
    inp['x'] = nrm((BATCH, SEQ, D), 1.0)
    inp['c'] = nrm((BATCH, D), 1.0)
    inp['ctx'] = nrm((BATCH, CTX_LEN, D), 1.0)
    inp['c_ctx'] = nrm((D,), 1.0)
    inp['l0_ada_w'] = nrm((D, 6 * D), 0.5 * D ** -0.5)
    inp['l0_ada_b'] = nrm((6 * D,), 0.02)
    inp['l0_norm1_g'] = gain((D,))
    inp['l0_norm2_g'] = gain((D,))
    inp['l0_w_in'] = nrm((D, IN_EVEN), D ** -0.5)
    inp['l0_sgu_norm_g'] = gain((A_GROUPS, A_GROUP_DIM))
    inp['l0_sgu_w'] = nrm((A_GROUPS, CHUNK, CHUNK), CHUNK ** -0.5)
    inp['l0_sgu_b'] = 1.0 + nrm((A_GROUPS, CHUNK), 0.02)
    inp['l0_shift_w'] = jnp.array([0.25, 0.5, 0.25], f32)[:, None] + nrm((3, B_WIDTH), 0.05)
    inp['l0_w0'] = jax.random.uniform(next(ks), (2, MIX_B), f32, -5.0, 1.0)
    inp['l0_w_up'] = nrm((2, DECAY_LORA, MIX_B), 0.5 * DECAY_LORA ** -0.5)
    inp['l0_a0'] = nrm((2, MIX_B), 0.1)
    inp['l0_a_up'] = nrm((2, AAA_LORA, MIX_B), AAA_LORA ** -0.5)
    inp['l0_g_up'] = nrm((GATE_LORA, MIX_B), GATE_LORA ** -0.5)
    inp['l0_k_k'] = 0.85 + nrm((MIX_B,), 0.02)
    inp['l0_k_a'] = 1.0 + nrm((MIX_B,), 0.02)
    inp['l0_r_k'] = nrm((B_HEADS, B_HEAD), 0.1)
    inp['l0_lnx_w'] = gain((MIX_B,))
    inp['l0_lnx_b'] = nrm((MIX_B,), 0.02)
    inp['l0_w_out'] = nrm((D, D), D ** -0.5)
    inp['l0_ffn_w1'] = nrm((D, D_FF), D ** -0.5)
    inp['l0_ffn_w3'] = nrm((D, D_FF), D ** -0.5)
    inp['l0_ffn_w2'] = nrm((D_FF, D), D_FF ** -0.5)
    inp['l1_ada_w'] = nrm((D, 6 * D), 0.5 * D ** -0.5)
    inp['l1_ada_b'] = nrm((6 * D,), 0.02)
    inp['l1_norm1_g'] = gain((D,))
    inp['l1_norm2_g'] = gain((D,))
    inp['l1_w_qkv'] = nrm((D, 3 * D), D ** -0.5)
    inp['l1_q_norm_g'] = gain((C_HEAD,))
    inp['l1_k_norm_g'] = gain((C_HEAD,))
    inp['l1_lam_q1'] = nrm((C_HEAD,), 0.1)
    inp['l1_lam_k1'] = nrm((C_HEAD,), 0.1)
    inp['l1_lam_q2'] = nrm((C_HEAD,), 0.1)
    inp['l1_lam_k2'] = nrm((C_HEAD,), 0.1)
    inp['l1_subln_g'] = gain((2 * C_HEAD,))
    inp['l1_w_out'] = nrm((D, D), D ** -0.5)
    inp['l1_router'] = nrm((D, N_EXPERTS), D ** -0.5)
    inp['l1_exp_w1'] = nrm((N_EXPERTS, D, D_EXPERT), D ** -0.5)
    inp['l1_exp_w3'] = nrm((N_EXPERTS, D, D_EXPERT), D ** -0.5)
    inp['l1_exp_w2'] = nrm((N_EXPERTS, D_EXPERT, D), D_EXPERT ** -0.5)
    return inp


def reference(x, c, ctx, c_ctx,
              l0_ada_w, l0_ada_b, l0_norm1_g, l0_norm2_g, l0_w_in, l0_sgu_norm_g, l0_sgu_w, l0_sgu_b,
              l0_shift_w, l0_w0, l0_w_up, l0_a0, l0_a_up, l0_g_up, l0_k_k, l0_k_a, l0_r_k, l0_lnx_w, l0_lnx_b,
              l0_w_out, l0_ffn_w1, l0_ffn_w3, l0_ffn_w2,
              l1_ada_w, l1_ada_b, l1_norm1_g, l1_norm2_g, l1_w_qkv, l1_q_norm_g, l1_k_norm_g,
              l1_lam_q1, l1_lam_k1, l1_lam_q2, l1_lam_k2, l1_subln_g, l1_w_out, l1_router,
              l1_exp_w1, l1_exp_w3, l1_exp_w2):
    even = (l0_ada_w, l0_ada_b, l0_norm1_g, l0_norm2_g, l0_w_in, l0_sgu_norm_g, l0_sgu_w, l0_sgu_b,
            l0_shift_w, l0_w0, l0_w_up, l0_a0, l0_a_up, l0_g_up, l0_k_k, l0_k_a, l0_r_k, l0_lnx_w, l0_lnx_b,
            l0_w_out, l0_ffn_w1, l0_ffn_w3, l0_ffn_w2)
    odd = (l1_ada_w, l1_ada_b, l1_norm1_g, l1_norm2_g, l1_w_qkv, l1_q_norm_g, l1_k_norm_g,
           l1_lam_q1, l1_lam_k1, l1_lam_q2, l1_lam_k2, l1_subln_g, l1_w_out, l1_router,
           l1_exp_w1, l1_exp_w3, l1_exp_w2)
    layer_params = (even, odd)
    xc = ctx
    for layer in range(DEPTH):
        if layer % 2 == 0:
            x, xc = even_layer(x, xc, c, c_ctx, *layer_params[layer])
        else:
            x = odd_layer(x, xc, c, c_ctx, lambda_init(layer), *layer_params[layer])
    return x
```

```python
import functools
import math

import jax
import jax.numpy as jnp
from jax import lax
from jax.experimental import pallas as pl
from jax.experimental.pallas import tpu as pltpu

F32 = jnp.float32
BF16 = jnp.bfloat16

D_MODEL = 2048
GRID_W = 64
EPS = 1e-6

MIX_A = 1024
SGU_CHUNK = 128
A_GROUPS = 8
MIX_B = 1024
B_HEAD = 64
B_HEADS = 16
DECAY_LORA = 64
AAA_LORA = 64
GATE_LORA = 160
B_WIDTH = 3 * MIX_B + DECAY_LORA + AAA_LORA + GATE_LORA
GN_EPS = B_HEAD * 1e-5

C_HEADS = 16
C_HEAD = 64
ROPE_AXIS = C_HEAD // 2
ROPE_THETA = 10000.0
SUBLN_EPS = 1e-5

N_EXPERTS = 8
D_EXPERT = 7168

LANES = 128
VMEM_LIMIT = 52 * 1024 * 1024

RW_CHUNK = 64
RW_TILE = 256
RW_PAIRS = B_HEADS // 2
Z_WIDTH = 3584
IN_PAD = 2 * MIX_A + Z_WIDTH


def _cparams(sem):
    return pltpu.CompilerParams(dimension_semantics=sem, vmem_limit_bytes=VMEM_LIMIT)


def _group_of_tile(i, tm, n_lat, seq, batch):
    return jnp.where(i * tm < n_lat, (i * tm) // seq, batch)


def _norm_mod_kernel(x_ref, g_ref, mod_ref, o_ref, *, s_idx, c_idx):
    x = x_ref[...]
    ms = jnp.mean(x * x, axis=-1, keepdims=True)
    y = x * lax.rsqrt(ms + EPS) * g_ref[...]
    shift = mod_ref[0, s_idx:s_idx + 1, :]
    scale = mod_ref[0, c_idx:c_idx + 1, :]
    o_ref[...] = (y * (1.0 + scale) + shift).astype(o_ref.dtype)


def norm_mod(x, gain, mods, s_idx, c_idx, *, n_lat, seq, batch, tm=512):
    m, d = x.shape
    grp = functools.partial(_group_of_tile, tm=tm, n_lat=n_lat, seq=seq, batch=batch)
    return pl.pallas_call(
        functools.partial(_norm_mod_kernel, s_idx=s_idx, c_idx=c_idx),
        grid=(m // tm,),
        in_specs=[
            pl.BlockSpec((tm, d), lambda i: (i, 0)),
            pl.BlockSpec((1, d), lambda i: (0, 0)),
            pl.BlockSpec((1, 6, d), lambda i: (grp(i), 0, 0)),
        ],
        out_specs=pl.BlockSpec((tm, d), lambda i: (i, 0)),
        out_shape=jax.ShapeDtypeStruct((m, d), BF16),
        compiler_params=_cparams(("parallel",)),
        name="norm_mod",
    )(x, gain.reshape(1, d), mods)


def _mm_kernel(*refs, kind, nk, silu_x, has_bias, gate_idx):
    refs = list(refs)
    x_ref = refs.pop(0)
    w_refs = [refs.pop(0)]
    if kind == "swiglu":
        w_refs.append(refs.pop(0))
    bias_ref = refs.pop(0) if has_bias else None
    res_ref = mod_ref = None
    if kind == "res":
        res_ref = refs.pop(0)
        mod_ref = refs.pop(0)
    o_ref = refs.pop(0)
    acc_refs = refs

    x = x_ref[...]
    if silu_x:
        x = jax.nn.silu(x.astype(F32))
    x = x.astype(BF16)
    prods = [jnp.dot(x, w_ref[...].astype(BF16), preferred_element_type=F32) for w_ref in w_refs]

    def epilogue(vals):
        a = vals[0]
        if kind == "swiglu":
            a = jax.nn.silu(a) * vals[1]
        if has_bias:
            a = a + bias_ref[...]
        if kind == "res":
            a = res_ref[...] + mod_ref[0, gate_idx:gate_idx + 1, :] * a
        o_ref[...] = a.astype(o_ref.dtype)

    if nk == 1:
        epilogue(prods)
        return

    k = pl.program_id(2)

    @pl.when(k == 0)
    def _():
        for acc, p in zip(acc_refs, prods):
            acc[...] = p

    @pl.when(k > 0)
    def _():
        for acc, p in zip(acc_refs, prods):
            acc[...] += p

    @pl.when(k == nk - 1)
    def _():
        epilogue([acc[...] for acc in acc_refs])


def matmul(x, w, *, w3=None, bias=None, res=None, mods=None, gate_idx=0, out_dtype=F32, silu_x=False,
           tm=1024, tn=512, tk=None, group=None):
    m, kdim = x.shape
    n = w.shape[1]
    tm = min(tm, m)
    tk = kdim if tk is None else tk
    nk = kdim // tk
    kind = "swiglu" if w3 is not None else ("res" if res is not None else "plain")
    in_specs = [pl.BlockSpec((tm, tk), lambda i, j, k: (i, k)),
                pl.BlockSpec((tk, tn), lambda i, j, k: (k, j))]
    args = [x, w]
    if w3 is not None:
        in_specs.append(pl.BlockSpec((tk, tn), lambda i, j, k: (k, j)))
        args.append(w3)
    if bias is not None:
        in_specs.append(pl.BlockSpec((1, tn), lambda i, j, k: (0, j)))
        args.append(bias.reshape(1, n))
    if res is not None:
        in_specs.append(pl.BlockSpec((tm, tn), lambda i, j, k: (i, j)))
        in_specs.append(pl.BlockSpec((1, 6, tn), lambda i, j, k: (group(i), 0, j)))
        args += [res, mods]
    n_acc = 0 if nk == 1 else (2 if w3 is not None else 1)
    return pl.pallas_call(
        functools.partial(_mm_kernel, kind=kind, nk=nk, silu_x=silu_x, has_bias=bias is not None,
                          gate_idx=gate_idx),
        grid=(m // tm, n // tn, nk),
        in_specs=in_specs,
        out_specs=pl.BlockSpec((tm, tn), lambda i, j, k: (i, j)),
        out_shape=jax.ShapeDtypeStruct((m, n), out_dtype),
        scratch_shapes=[pltpu.VMEM((tm, tn), F32)] * n_acc,
        compiler_params=_cparams(("parallel", "parallel", "arbitrary")),
        name="matmul_" + kind,
    )(*args)


def _sgu_kernel(u_ref, v_ref, ng_ref, ws_ref, bs_ref, o_ref, *, chunks):
    for c in range(chunks):
        rows = slice(c * SGU_CHUNK, (c + 1) * SGU_CHUNK)
        for g in range(A_GROUPS):
            cols = slice(g * LANES, (g + 1) * LANES)
            vg = jax.nn.gelu(v_ref[rows, cols])
            ms = jnp.mean(vg * vg, axis=-1, keepdims=True)
            vn = vg * lax.rsqrt(ms + EPS) * ng_ref[:, cols]
            s = jnp.dot(ws_ref[g].astype(BF16), vn.astype(BF16), preferred_element_type=F32)
            u = jax.nn.gelu(u_ref[rows, cols])
            o_ref[rows, cols] = (u * (s + bs_ref[:, cols])).astype(o_ref.dtype)


def chunk_sgu(hp, norm_g, ws, bs, *, tm=256):
    m = hp.shape[0]
    bs_exp = jnp.repeat(bs.T, LANES, axis=1)
    return pl.pallas_call(
        functools.partial(_sgu_kernel, chunks=tm // SGU_CHUNK),
        grid=(m // tm,),
        in_specs=[
            pl.BlockSpec((tm, MIX_A), lambda i: (i, 0)),
            pl.BlockSpec((tm, MIX_A), lambda i: (i, 1)),
            pl.BlockSpec((1, MIX_A), lambda i: (0, 0)),
            pl.BlockSpec((A_GROUPS, SGU_CHUNK, SGU_CHUNK), lambda i: (0, 0, 0)),
            pl.BlockSpec((SGU_CHUNK, MIX_A), lambda i: (0, 0)),
        ],
        out_specs=pl.BlockSpec((tm, MIX_A), lambda i: (i, 0)),
        out_shape=jax.ShapeDtypeStruct((m, MIX_A), BF16),
        compiler_params=_cparams(("parallel",)),
        name="chunk_sgu",
    )(hp, hp, norm_g.reshape(1, MIX_A), ws, bs_exp)


def _shift_kernel(x_ref, p_ref, n_ref, w_ref, o_ref, *, ts, lat_tiles, tiles_per_lat, tiles_per_ctx):
    i = pl.program_id(0)
    is_lat = i < lat_tiles
    pos = jnp.where(is_lat, i % tiles_per_lat, (i - lat_tiles) % tiles_per_ctx)
    last = jnp.where(is_lat, tiles_per_lat - 1, tiles_per_ctx - 1)
    x = x_ref[...]
    prev_row = jnp.where(pos == 0, 0.0, p_ref[7:8, :])
    next_row = jnp.where(pos == last, 0.0, n_ref[0:1, :])
    rows = lax.broadcasted_iota(jnp.int32, x.shape, 0)
    xm = jnp.where(rows == 0, prev_row, pltpu.roll(x, 1, axis=0))
    xp = jnp.where(rows == ts - 1, next_row, pltpu.roll(x, ts - 1, axis=0))
    o_ref[...] = w_ref[0:1, :] * xm + w_ref[1:2, :] * x + w_ref[2:3, :] * xp


def token_shift(hp, shift_w, *, n_lat, seq, ctx_len, ts=256, tc=512):
    m = hp.shape[0]
    col0 = (2 * MIX_A) // tc
    halo = 8
    nblk8 = m // halo
    w_pad = jnp.pad(shift_w, ((0, 0), (0, Z_WIDTH - B_WIDTH)))
    kern = functools.partial(_shift_kernel, ts=ts, lat_tiles=n_lat // ts, tiles_per_lat=seq // ts,
                             tiles_per_ctx=ctx_len // ts)
    return pl.pallas_call(
        kern,
        grid=(m // ts, Z_WIDTH // tc),
        in_specs=[
            pl.BlockSpec((ts, tc), lambda i, j: (i, col0 + j)),
            pl.BlockSpec((halo, tc), lambda i, j: (jnp.maximum(i * (ts // halo) - 1, 0), col0 + j)),
            pl.BlockSpec((halo, tc), lambda i, j: (jnp.minimum((i + 1) * (ts // halo), nblk8 - 1), col0 + j)),
            pl.BlockSpec((3, tc), lambda i, j: (0, j)),
        ],
        out_specs=pl.BlockSpec((ts, tc), lambda i, j: (i, j)),
        out_shape=jax.ShapeDtypeStruct((m, Z_WIDTH), F32),
        compiler_params=_cparams(("parallel", "parallel")),
        name="token_shift",
    )(hp, hp, hp, w_pad)


def _mask_lanes(x, m0):
    z = jnp.zeros_like(x)
    return jnp.concatenate([jnp.where(m0, x, z), jnp.where(m0, z, x)], axis=0)


def _dot(a, b):
    return jnp.dot(a.astype(BF16), b.astype(BF16), preferred_element_type=F32)


def _dot_nt(a, b):
    return lax.dot_general(a.astype(BF16), b.astype(BF16), (((1,), (1,)), ((), ())),
                           preferred_element_type=F32)


def _dot_tn(a, b):
    return lax.dot_general(a.astype(BF16), b.astype(BF16), (((0,), (0,)), ((), ())),
                           preferred_element_type=F32)


def _head_sums(x, m0):
    s0 = jnp.sum(jnp.where(m0, x, 0.0), axis=-1, keepdims=True)
    s1 = jnp.sum(jnp.where(m0, 0.0, x), axis=-1, keepdims=True)
    return jnp.where(m0, s0, s1)


def _rwkv_kernel(*refs, reverse, n_chunks):
    if reverse:
        (zr_ref, zk_ref, zv_ref, zl_ref, w0_ref, wt_ref, a0_ref, wa_ref, kk_ref, ka_ref, rk_ref,
         y0_ref, g_ref, bon0_ref, lnw_ref, lnb_ref, out_ref,
         s_ref, r_s, lw_s, kd_s, v_s, kn_s, b_s, y_s) = refs
    else:
        (zr_ref, zk_ref, zv_ref, zl_ref, w0_ref, wt_ref, a0_ref, wa_ref, kk_ref, ka_ref, rk_ref, wg_ref,
         y_out_ref, g_out_ref, bon_out_ref,
         s_ref, r_s, lw_s, kd_s, v_s, kn_s, b_s, y_s) = refs
    ch = RW_CHUNK
    j = pl.program_id(1)

    @pl.when(j == 0)
    def _():
        s_ref[...] = jnp.zeros_like(s_ref)

    zl = zl_ref[...]
    lora_in = zl[:, 0:LANES]
    wl = w0_ref[...] + _dot(jnp.tanh(lora_in), wt_ref[...])
    wlog = -(jnp.maximum(-wl, 0.0) + jnp.log(1.0 + jnp.exp(-jnp.abs(wl)))) - 0.5
    lw = -jnp.exp(wlog)
    a = jax.nn.sigmoid(a0_ref[...] + _dot(lora_in, wa_ref[...]))
    r = zr_ref[...]
    k = zk_ref[...]
    v = zv_ref[...]
    kd = k * (1.0 + (a - 1.0) * ka_ref[...])
    kraw = k * kk_ref[...]
    rkd = r * kd * rk_ref[...]
    if not reverse:
        g_out_ref[...] = _dot(jax.nn.sigmoid(zl[:, LANES:3 * LANES]), wg_ref[...])

    lane = lax.broadcasted_iota(jnp.int32, (1, LANES), 1)
    m0 = lane < B_HEAD
    for p in range(RW_PAIRS):
        cols = slice(p * LANES, (p + 1) * LANES)
        kp = kraw[:, cols]
        nrm = jnp.sqrt(_head_sums(kp * kp, m0))
        kn = kp / jnp.maximum(nrm, 1e-12)
        r_s[p] = r[:, cols]
        lw_s[p] = lw[:, cols]
        kd_s[p] = kd[:, cols]
        v_s[p] = v[:, cols]
        kn_s[p] = kn
        b_s[p] = kn * a[:, cols]
        if not reverse:
            bon_out_ref[:, cols] = _head_sums(rkd[:, cols], m0) * v[:, cols]

    ri = lax.broadcasted_iota(jnp.int32, (2 * ch, 2 * ch), 0)
    ci = lax.broadcasted_iota(jnp.int32, (2 * ch, 2 * ch), 1)
    same_head = (ri // ch) == (ci // ch)
    if reverse:
        strict = same_head & (ci > ri)
        incl = same_head & (ci >= ri)
    else:
        strict = same_head & (ci < ri)
        incl = same_head & (ci <= ri)
    eye = (ri == ci).astype(F32)
    ti = lax.broadcasted_iota(jnp.int32, (ch, ch), 0)
    si = lax.broadcasted_iota(jnp.int32, (ch, ch), 1)
    tri = ((si >= ti) if reverse else (si <= ti)).astype(F32)

    def chunk_body(ci_, carry):
        c = (n_chunks - 1 - ci_) if reverse else ci_
        rows = pl.ds(pl.multiple_of(c * ch, ch), ch)

        def pair_body(p, carry2):
            s_bd = s_ref[p]
            lw_c = lw_s[p, rows, :]
            cs = jnp.dot(tri, lw_c, precision=lax.Precision.HIGHEST, preferred_element_type=F32)
            total = cs[0:1, :] if reverse else cs[ch - 1:ch, :]
            g_in = jnp.exp(cs)
            g_ex = jnp.exp(cs - lw_c)
            g_inv = jnp.exp(-cs)
            g_end = jnp.exp(total)
            kk_t = kn_s[p, rows, :] * g_ex
            bt = b_s[p, rows, :] * g_inv
            kdt = kd_s[p, rows, :] * g_inv
            rt = r_s[p, rows, :] * g_in
            vv = v_s[p, rows, :]

            lhs = jnp.concatenate([kk_t, kk_t, rt, rt], axis=0)
            bt_s = _mask_lanes(bt, m0)
            kd_m = _mask_lanes(kdt, m0)
            rhs = jnp.concatenate([bt_s, kd_m], axis=0)
            pq = _dot_nt(lhs, rhs)
            zero = jnp.zeros((2 * ch, 2 * ch), F32)
            a_m = jnp.where(strict, pq[0:2 * ch, 0:2 * ch], zero)
            b_m = jnp.where(strict, pq[0:2 * ch, 2 * ch:4 * ch], zero)
            m2 = jnp.where(incl, pq[2 * ch:4 * ch, 0:2 * ch], zero)
            m1 = jnp.where(incl, pq[2 * ch:4 * ch, 2 * ch:4 * ch], zero)

            t_m = eye - a_m
            pw = a_m
            n_sq = int(math.log2(ch)) - 1
            for _ in range(n_sq):
                pw = _dot(pw, pw)
                t_m = t_m + _dot(t_m, pw)

            v_st = _mask_lanes(vv, m0)
            kk_st = _mask_lanes(kk_t, m0)
            bv = _dot(b_m, v_st)
            ku = _dot(t_m, jnp.concatenate([kk_st, bv], axis=1))
            kkp = ku[:, 0:LANES]
            u0 = ku[:, LANES:2 * LANES]
            m2ku = _dot(m2, ku)
            rp_s = _mask_lanes(rt, m0) - m2ku[:, 0:LANES]
            y0_s = _dot(m1, v_st) - m2ku[:, LANES:2 * LANES]
            rp = rp_s[0:ch] + rp_s[ch:2 * ch]
            y0 = y0_s[0:ch] + y0_s[ch:2 * ch]
            y = _dot_nt(rp, s_bd) + y0

            btp = bt_s * g_end
            kdp = kd_m * g_end
            g_mat = _dot_tn(v_st, kdp) - _dot_tn(u0, btp)
            s_new = s_bd * g_end - _dot(_dot_nt(s_bd, kkp), btp) + g_mat
            s_ref[p] = s_new
            y_s[p, rows, :] = y
            return carry2

        return lax.fori_loop(0, RW_PAIRS, pair_body, carry)

    lax.fori_loop(0, n_chunks, chunk_body, 0)

    for p in range(RW_PAIRS):
        cols = slice(p * LANES, (p + 1) * LANES)
        if not reverse:
            y_out_ref[:, cols] = y_s[p]
        else:
            bon1 = _head_sums(rkd[:, cols], m0) * v[:, cols]
            ysum = y0_ref[:, cols] + y_s[p]
            mu = _head_sums(ysum, m0) * (1.0 / B_HEAD)
            dlt = ysum - mu
            var = _head_sums(dlt * dlt, m0) * (1.0 / B_HEAD)
            yn = dlt * lax.rsqrt(var + GN_EPS)
            o = (yn * lnw_ref[:, cols] + lnb_ref[:, cols] + bon0_ref[:, cols] + bon1) * g_ref[:, cols]
            out_ref[:, cols] = o.astype(out_ref.dtype)


def rwkv7_bidir(zs, w0, w_up, a0, a_up, g_up, k_k, k_a, r_k, lnx_w, lnx_b, *, batch, seq, ctx_len):
    m = zs.shape[0]
    n_lat = batch * seq
    tb = RW_TILE
    nct, nlt = ctx_len // tb, seq // tb
    steps = nct + nlt

    def row_block(reverse):
        def f(b, j):
            jc = (nct - 1 - j) if reverse else j
            jl = (nlt - 1 - (j - nct)) if reverse else (j - nct)
            return jnp.where(j < nct, (n_lat + b * ctx_len) // tb + jc, b * nlt + jl)
        return f

    def pad_rows(w, top, total):
        return jnp.pad(w, ((top, total - top - w.shape[0]), (0, 0)))

    wg = pad_rows(g_up, 0, 2 * LANES)
    rk = r_k.reshape(1, MIX_B)
    vec = lambda a: a.reshape(1, MIX_B)
    scratch = [pltpu.VMEM((RW_PAIRS, LANES, LANES), F32)] + [pltpu.VMEM((RW_PAIRS, tb, LANES), F32)] * 7
    const = lambda shape: pl.BlockSpec(shape, lambda b, j: (0,) * len(shape))

    def call(reverse, extra_in, extra_specs, out_shapes, out_specs, d):
        rb = row_block(reverse)
        wt = pad_rows(w_up[d], 0, LANES)
        wa = pad_rows(a_up[d], DECAY_LORA, LANES)
        in_specs = [
            pl.BlockSpec((tb, MIX_B), lambda b, j: (rb(b, j), 0)),
            pl.BlockSpec((tb, MIX_B), lambda b, j: (rb(b, j), 1)),
            pl.BlockSpec((tb, MIX_B), lambda b, j: (rb(b, j), 2)),
            pl.BlockSpec((tb, 512), lambda b, j: (rb(b, j), 6)),
            const((1, MIX_B)), const((LANES, MIX_B)), const((1, MIX_B)), const((LANES, MIX_B)),
            const((1, MIX_B)), const((1, MIX_B)), const((1, MIX_B)),
        ] + extra_specs
        args = [zs, zs, zs, zs, vec(w0[d]), wt, vec(a0[d]), wa, vec(k_k), vec(k_a), rk] + extra_in
        return pl.pallas_call(
            functools.partial(_rwkv_kernel, reverse=reverse, n_chunks=tb // RW_CHUNK),
            grid=(batch, steps),
            in_specs=in_specs,
            out_specs=out_specs,
            out_shape=out_shapes,
            scratch_shapes=scratch,
            compiler_params=_cparams(("parallel", "arbitrary")),
            name="rwkv7_rev" if reverse else "rwkv7_fwd",
        )(*args)

    rbf = row_block(False)
    tile_f = pl.BlockSpec((tb, MIX_B), lambda b, j: (rbf(b, j), 0))
    y0, g, bon0 = call(False, [wg], [const((2 * LANES, MIX_B))],
                       [jax.ShapeDtypeStruct((m, MIX_B), F32)] * 3, [tile_f] * 3, 0)
    rbr = row_block(True)
    tile_r = pl.BlockSpec((tb, MIX_B), lambda b, j: (rbr(b, j), 0))
    out = call(True, [y0, g, bon0, vec(lnx_w), vec(lnx_b)],
               [tile_r, tile_r, tile_r, const((1, MIX_B)), const((1, MIX_B))],
               jax.ShapeDtypeStruct((m, MIX_B), BF16), tile_r, 1)
    return out


def _half_rms(x, gain, m0):
    ms = _head_sums(x * x, m0) * (1.0 / C_HEAD)
    return x * lax.rsqrt(ms + EPS) * gain


def _rope(x, cos, sin):
    lane = lax.broadcasted_iota(jnp.int32, x.shape, 1)
    first = (lane % ROPE_AXIS) < (ROPE_AXIS // 2)
    half = ROPE_AXIS // 2
    rot = jnp.where(first, -pltpu.roll(x, LANES - half, axis=1), pltpu.roll(x, half, axis=1))
    return x * cos + rot * sin


def _attn_kernel(q_ref, kl_ref, vl_ref, kc_ref, vc_ref, cq_ref, sq_ref, ck_ref, sk_ref, qg_ref, kg_ref,
                 lam_ref, sg_ref, o_ref, k_s, v_s, *, ctx_len, lam_init):
    qi = pl.program_id(2)
    lane = lax.broadcasted_iota(jnp.int32, (1, LANES), 1)
    m0 = lane < C_HEAD

    @pl.when(qi == 0)
    def _():
        kc = _half_rms(kc_ref[...], kg_ref[...], m0)
        kl = _rope(_half_rms(kl_ref[...], kg_ref[...], m0), ck_ref[...], sk_ref[...])
        k_s[0:ctx_len, :] = kc.astype(BF16)
        k_s[ctx_len:, :] = kl.astype(BF16)
        v_s[0:ctx_len, :] = vc_ref[...].astype(BF16)
        v_s[ctx_len:, :] = vl_ref[...].astype(BF16)

    lp = lam_ref[...]
    lam = (jnp.exp(jnp.sum(lp[0:1] * lp[1:2], keepdims=True)) - jnp.exp(jnp.sum(lp[2:3] * lp[3:4], keepdims=True))
           + lam_init)
    q = _rope(_half_rms(q_ref[...], qg_ref[...], m0), cq_ref[...], sq_ref[...]) * (C_HEAD ** -0.5)
    keys = k_s[...]
    zq = jnp.zeros_like(q)
    s0 = _dot_nt(jnp.where(m0, q, zq), keys)
    s1 = _dot_nt(jnp.where(m0, zq, q), keys)
    e0 = jnp.exp(s0 - jnp.max(s0, axis=-1, keepdims=True))
    e1 = jnp.exp(s1 - jnp.max(s1, axis=-1, keepdims=True))
    w0 = 1.0 / jnp.sum(e0, axis=-1, keepdims=True)
    w1 = lam / jnp.sum(e1, axis=-1, keepdims=True)
    attn = e0 * w0 - e1 * w1
    o = jnp.dot(attn.astype(BF16), v_s[...], preferred_element_type=F32)
    ms = jnp.mean(o * o, axis=-1, keepdims=True)
    o = o * lax.rsqrt(ms + SUBLN_EPS) * sg_ref[...] * (1.0 - lam_init)
    o_ref[...] = o.astype(o_ref.dtype)


def _rope_tables(n):
    rows = n // GRID_W
    row = jnp.broadcast_to(jnp.arange(rows, dtype=F32)[:, None], (rows, GRID_W)).reshape(-1)
    col = jnp.broadcast_to(jnp.arange(GRID_W, dtype=F32)[None, :], (rows, GRID_W)).reshape(-1)
    inv = ROPE_THETA ** (-jnp.arange(0, ROPE_AXIS, 2, dtype=F32) / ROPE_AXIS)
    ar = row[:, None] * inv
    ac = col[:, None] * inv
    ang = jnp.concatenate([ar, ar, ac, ac, ar, ar, ac, ac], axis=-1)
    return jnp.cos(ang), jnp.sin(ang)


def diff_attention(qkv, q_g, k_g, lam_params, subln_g, lam_init, *, batch, seq, ctx_len, tq=256):
    n_lat = batch * seq
    cos, sin = _rope_tables(seq)
    nq = seq // tq
    hq, hk, hv = 0, D_MODEL // LANES, 2 * D_MODEL // LANES
    ctx_blk0 = n_lat // ctx_len
    two = lambda a: jnp.concatenate([a, a]).reshape(1, LANES)
    const = lambda shape: pl.BlockSpec(shape, lambda b, h, i: (0,) * len(shape))
    return pl.pallas_call(
        functools.partial(_attn_kernel, ctx_len=ctx_len, lam_init=lam_init),
        grid=(batch, C_HEADS, nq),
        in_specs=[
            pl.BlockSpec((tq, LANES), lambda b, h, i: (b * nq + i, hq + h)),
            pl.BlockSpec((seq, LANES), lambda b, h, i: (b, hk + h)),
            pl.BlockSpec((seq, LANES), lambda b, h, i: (b, hv + h)),
            pl.BlockSpec((ctx_len, LANES), lambda b, h, i: (ctx_blk0 + b, hk + h)),
            pl.BlockSpec((ctx_len, LANES), lambda b, h, i: (ctx_blk0 + b, hv + h)),
            pl.BlockSpec((tq, LANES), lambda b, h, i: (i, 0)),
            pl.BlockSpec((tq, LANES), lambda b, h, i: (i, 0)),
            const((seq, LANES)), const((seq, LANES)),
            const((1, LANES)), const((1, LANES)), const((4, C_HEAD)), const((1, LANES)),
        ],
        out_specs=pl.BlockSpec((tq, LANES), lambda b, h, i: (b * nq + i, h)),
        out_shape=jax.ShapeDtypeStruct((n_lat, D_MODEL), BF16),
        scratch_shapes=[pltpu.VMEM((ctx_len + seq, LANES), BF16)] * 2,
        compiler_params=_cparams(("parallel", "parallel", "arbitrary")),
        name="diff_attention",
    )(qkv, qkv, qkv, qkv, qkv, cos, sin, cos, sin, two(q_g), two(k_g), lam_params, subln_g.reshape(1, LANES))


def _route_kernel(x_ref, g_ref, mod_ref, rw_ref, xn_ref, idx_ref, gate_ref, *, s_idx, c_idx):
    x = x_ref[...]
    ms = jnp.mean(x * x, axis=-1, keepdims=True)
    y = x * lax.rsqrt(ms + EPS) * g_ref[...]
    h = y * (1.0 + mod_ref[0, c_idx:c_idx + 1, :]) + mod_ref[0, s_idx:s_idx + 1, :]
    xn_ref[...] = h
    logits = jnp.dot(h, rw_ref[...], precision=lax.Precision.HIGHEST, preferred_element_type=F32)
    lane = lax.broadcasted_iota(jnp.int32, logits.shape, 1)
    neg = jnp.float32(-jnp.inf)
    lg = jnp.where(lane < N_EXPERTS, logits, neg)
    m1 = jnp.max(lg, axis=-1, keepdims=True)
    i1 = jnp.min(jnp.where(lg == m1, lane, LANES), axis=-1, keepdims=True)
    lg2 = jnp.where(lane == i1, neg, lg)
    m2 = jnp.max(lg2, axis=-1, keepdims=True)
    i2 = jnp.min(jnp.where(lg2 == m2, lane, LANES), axis=-1, keepdims=True)
    e2 = jnp.exp(m2 - m1)
    g1 = 1.0 / (1.0 + e2)
    g2 = e2 * g1
    idx_ref[...] = jnp.where(lane == 0, i1, jnp.where(lane == 1, i2, 0))
    gate_ref[...] = jnp.where(lane == 0, g1, jnp.where(lane == 1, g2, 0.0))


def route(x, gain, mods, router, s_idx, c_idx, *, seq, batch, tm=256):
    m, d = x.shape
    rw = jnp.pad(router, ((0, 0), (0, LANES - N_EXPERTS)))
    return pl.pallas_call(
        functools.partial(_route_kernel, s_idx=s_idx, c_idx=c_idx),
        grid=(m // tm,),
        in_specs=[
            pl.BlockSpec((tm, d), lambda i: (i, 0)),
            pl.BlockSpec((1, d), lambda i: (0, 0)),
            pl.BlockSpec((1, 6, d), lambda i: ((i * tm) // seq, 0, 0)),
            pl.BlockSpec((d, LANES), lambda i: (0, 0)),
        ],
        out_specs=[pl.BlockSpec((tm, d), lambda i: (i, 0)),
                   pl.BlockSpec((tm, LANES), lambda i: (i, 0)),
                   pl.BlockSpec((tm, LANES), lambda i: (i, 0))],
        out_shape=[jax.ShapeDtypeStruct((m, d), F32),
                   jax.ShapeDtypeStruct((m, LANES), jnp.int32),
                   jax.ShapeDtypeStruct((m, LANES), F32)],
        compiler_params=_cparams(("parallel",)),
        name="moe_route",
    )(x, gain.reshape(1, d), mods, rw)


def _row_copy(src_hbm, dst_ref, src_row, dst_row, sem):
    return pltpu.make_async_copy(src_hbm.at[pl.ds(src_row, 1)], dst_ref.at[pl.ds(dst_row, 1)], sem)


def _gather_kernel(idx_ref, src_hbm, o_ref, sem, *, tg):
    def issue(r, c):
        _row_copy(src_hbm, o_ref, idx_ref[0, 0, r], r, sem).start()
        return c

    def drain(r, c):
        _row_copy(src_hbm, o_ref, 0, r, sem).wait()
        return c

    lax.fori_loop(0, tg, issue, 0)
    lax.fori_loop(0, tg, drain, 0)


def gather_rows(src, idx, *, tg=256):
    mp = idx.shape[0]
    d = src.shape[1]
    return pl.pallas_call(
        functools.partial(_gather_kernel, tg=tg),
        grid=(mp // tg,),
        in_specs=[pl.BlockSpec((1, 1, tg), lambda i: (i, 0, 0), memory_space=pltpu.SMEM),
                  pl.BlockSpec(memory_space=pl.ANY)],
        out_specs=pl.BlockSpec((tg, d), lambda i: (i, 0)),
        out_shape=jax.ShapeDtypeStruct((mp, d), src.dtype),
        scratch_shapes=[pltpu.SemaphoreType.DMA(())],
        compiler_params=_cparams(("arbitrary",)),
        name="moe_gather",
    )(idx.reshape(mp // tg, 1, tg), src)


def _gmm_kernel(te_ref, nu_ref, *refs, kind, nk):
    refs = list(refs)
    x_ref = refs.pop(0)
    w_refs = [refs.pop(0)]
    if kind == "swiglu":
        w_refs.append(refs.pop(0))
    o_ref = refs.pop(0)
    acc_refs = refs
    i = pl.program_id(0)
    k = pl.program_id(2)

    @pl.when(i >= nu_ref[0])
    def _():
        o_ref[...] = jnp.zeros_like(o_ref)

    @pl.when(i < nu_ref[0])
    def _():
        x = x_ref[...].astype(BF16)
        prods = [jnp.dot(x, w_ref[0].astype(BF16), preferred_element_type=F32) for w_ref in w_refs]

        def epilogue(vals):
            a = vals[0]
            if kind == "swiglu":
                a = jax.nn.silu(a) * vals[1]
            o_ref[...] = a.astype(o_ref.dtype)

        if nk == 1:
            epilogue(prods)
        else:
            @pl.when(k == 0)
            def _():
                for acc, p in zip(acc_refs, prods):
                    acc[...] = p

            @pl.when(k > 0)
            def _():
                for acc, p in zip(acc_refs, prods):
                    acc[...] += p

            @pl.when(k == nk - 1)
            def _():
                epilogue([acc[...] for acc in acc_refs])


def grouped_matmul(x, w, tile_expert, n_used, *, w3=None, out_dtype=F32, tm=512, tn=512, tk=None):
    mp, kdim = x.shape
    n = w.shape[2]
    tk = kdim if tk is None else tk
    nk = kdim // tk
    kind = "swiglu" if w3 is not None else "plain"
    w_spec = pl.BlockSpec((1, tk, tn), lambda i, j, k, te, nu: (te[i], k, j))
    in_specs = [pl.BlockSpec((tm, tk), lambda i, j, k, te, nu: (i, k)), w_spec]
    args = [x, w]
    if w3 is not None:
        in_specs.append(w_spec)
        args.append(w3)
    n_acc = 0 if nk == 1 else 1
    return pl.pallas_call(
        functools.partial(_gmm_kernel, kind=kind, nk=nk),
        grid_spec=pltpu.PrefetchScalarGridSpec(
            num_scalar_prefetch=2,
            grid=(mp // tm, n // tn, nk),
            in_specs=in_specs,
            out_specs=pl.BlockSpec((tm, tn), lambda i, j, k, te, nu: (i, j)),
            scratch_shapes=[pltpu.VMEM((tm, tn), F32)] * n_acc,
        ),
        out_shape=jax.ShapeDtypeStruct((mp, n), out_dtype),
        compiler_params=_cparams(("parallel", "parallel", "arbitrary")),
        name="moe_gmm_" + kind,
    )(tile_expert, n_used, *args)


def _combine_kernel(pos_ref, ys_hbm, x_ref, gate_ref, mod_ref, o_ref, buf, sem, *, tc, gate_idx):
    def issue(r, c):
        _row_copy(ys_hbm, buf.at[0], pos_ref[0, 0, 2 * r], r, sem).start()
        _row_copy(ys_hbm, buf.at[1], pos_ref[0, 0, 2 * r + 1], r, sem).start()
        return c

    def drain(r, c):
        _row_copy(ys_hbm, buf.at[0], 0, r, sem).wait()
        _row_copy(ys_hbm, buf.at[1], 0, r, sem).wait()
        return c

    lax.fori_loop(0, tc, issue, 0)
    lax.fori_loop(0, tc, drain, 0)
    g = gate_ref[...]
    moe = g[:, 0:1] * buf[0] + g[:, 1:2] * buf[1]
    o_ref[...] = x_ref[...] + mod_ref[0, gate_idx:gate_idx + 1, :] * moe


def moe_combine(ys, pos, x, gates, mods, gate_idx, *, seq, tc=256):
    m, d = x.shape
    return pl.pallas_call(
        functools.partial(_combine_kernel, tc=tc, gate_idx=gate_idx),
        grid=(m // tc,),
        in_specs=[pl.BlockSpec((1, 1, 2 * tc), lambda i: (i, 0, 0), memory_space=pltpu.SMEM),
                  pl.BlockSpec(memory_space=pl.ANY),
                  pl.BlockSpec((tc, d), lambda i: (i, 0)),
                  pl.BlockSpec((tc, LANES), lambda i: (i, 0)),
                  pl.BlockSpec((1, 6, d), lambda i: ((i * tc) // seq, 0, 0))],
        out_specs=pl.BlockSpec((tc, d), lambda i: (i, 0)),
        out_shape=jax.ShapeDtypeStruct((m, d), F32),
        scratch_shapes=[pltpu.VMEM((2, tc, d), F32), pltpu.SemaphoreType.DMA(())],
        compiler_params=_cparams(("arbitrary",)),
        name="moe_combine",
    )(pos.reshape(m // tc, 1, 2 * tc), ys, x, gates, mods)


def moe_layer(x, gain, mods, router, w1, w3, w2, *, seq, batch, tm=512):
    n = x.shape[0]
    xn, idx, gates = route(x, gain, mods, router, 3, 4, seq=seq, batch=batch)
    e_flat = idx[:, 0:2].reshape(-1)
    onehot = (e_flat[:, None] == jnp.arange(N_EXPERTS)[None, :]).astype(jnp.int32)
    ranks = jnp.cumsum(onehot, axis=0) - onehot
    rank = jnp.sum(ranks * onehot, axis=1)
    counts = jnp.sum(onehot, axis=0)
    padded = ((counts + tm - 1) // tm) * tm
    starts = jnp.cumsum(padded) - padded
    pos = starts[e_flat] + rank
    mp = 2 * n + N_EXPERTS * tm
    token_of_row = jnp.zeros((mp,), jnp.int32).at[pos].set(jnp.arange(2 * n, dtype=jnp.int32) // 2)
    n_tiles = mp // tm
    ends = jnp.cumsum(padded)
    tile_start = jnp.arange(n_tiles, dtype=jnp.int32) * tm
    tile_expert = jnp.minimum(jnp.sum((tile_start[:, None] >= ends[None, :]).astype(jnp.int32), axis=1),
                              N_EXPERTS - 1).astype(jnp.int32)
    n_used = (ends[-1] // tm).astype(jnp.int32).reshape(1)

    xs = gather_rows(xn, token_of_row)
    hs = grouped_matmul(xs, w1, tile_expert, n_used, w3=w3, out_dtype=BF16, tm=tm, tn=512)
    ys = grouped_matmul(hs, w2, tile_expert, n_used, out_dtype=F32, tm=tm, tn=D_MODEL, tk=512)
    return moe_combine(ys, pos.astype(jnp.int32), x, gates, mods, 5, seq=seq)


def kernel(x, c, ctx, c_ctx, l0_ada_w, l0_ada_b, l0_norm1_g, l0_norm2_g, l0_w_in, l0_sgu_norm_g, l0_sgu_w, l0_sgu_b, l0_shift_w, l0_w0, l0_w_up, l0_a0, l0_a_up, l0_g_up, l0_k_k, l0_k_a, l0_r_k, l0_lnx_w, l0_lnx_b, l0_w_out, l0_ffn_w1, l0_ffn_w3, l0_ffn_w2, l1_ada_w, l1_ada_b, l1_norm1_g, l1_norm2_g, l1_w_qkv, l1_q_norm_g, l1_k_norm_g, l1_lam_q1, l1_lam_k1, l1_lam_q2, l1_lam_k2, l1_subln_g, l1_w_out, l1_router, l1_exp_w1, l1_exp_w3, l1_exp_w2):
    batch, seq, d = x.shape
    ctx_len = ctx.shape[1]
    n_lat = batch * seq
    tm = min(1024, seq, batch * ctx_len)
    geo = dict(n_lat=n_lat, seq=seq, batch=batch)
    grp = functools.partial(_group_of_tile, tm=tm, **geo)

    xall = jnp.concatenate([x.reshape(n_lat, d), ctx.reshape(batch * ctx_len, d)], axis=0)
    cond = jnp.concatenate([c, c_ctx[None, :], jnp.zeros((16 - batch - 1, d), F32)], axis=0)

    def ada(w, b):
        return matmul(cond, w, bias=b, silu_x=True, tn=1024).reshape(16, 6, d)

    mods = ada(l0_ada_w, l0_ada_b)
    hn = norm_mod(xall, l0_norm1_g, mods, 0, 1, tm=min(512, tm), **geo)
    w_in = jnp.pad(l0_w_in, ((0, 0), (0, IN_PAD - l0_w_in.shape[1])))
    hp = matmul(hn, w_in, tm=tm)
    a_out = chunk_sgu(hp, l0_sgu_norm_g, l0_sgu_w, l0_sgu_b)
    zs = token_shift(hp, l0_shift_w, n_lat=n_lat, seq=seq, ctx_len=ctx_len)
    b_out = rwkv7_bidir(zs, l0_w0, l0_w_up, l0_a0, l0_a_up, l0_g_up, l0_k_k, l0_k_a, l0_r_k, l0_lnx_w, l0_lnx_b,
                        batch=batch, seq=seq, ctx_len=ctx_len)
    mixed = jnp.concatenate([a_out, b_out], axis=1)
    x1 = matmul(mixed, l0_w_out, res=xall, mods=mods, gate_idx=2, group=grp, tm=tm)
    hn = norm_mod(x1, l0_norm2_g, mods, 3, 4, tm=min(512, tm), **geo)
    hff = matmul(hn, l0_ffn_w1, w3=l0_ffn_w3, out_dtype=BF16, tm=tm)
    x2 = matmul(hff, l0_ffn_w2, res=x1, mods=mods, gate_idx=5, group=grp, tm=tm, tn=1024, tk=512)

    mods = ada(l1_ada_w, l1_ada_b)
    hn = norm_mod(x2, l1_norm1_g, mods, 0, 1, tm=min(512, tm), **geo)
    qkv = matmul(hn, l1_w_qkv, tm=tm)
    lam_params = jnp.stack([l1_lam_q1, l1_lam_k1, l1_lam_q2, l1_lam_k2])
    lam_init = 0.8 - 0.6 * math.exp(-0.3 * 1)
    o = diff_attention(qkv, l1_q_norm_g, l1_k_norm_g, lam_params, l1_subln_g, lam_init,
                       batch=batch, seq=seq, ctx_len=ctx_len)
    x3 = matmul(o, l1_w_out, res=x2, mods=mods, gate_idx=2, group=grp, tm=tm)
    x4 = moe_layer(x3, l1_norm2_g, mods, l1_router, l1_exp_w1, l1_exp_w3, l1_exp_w2, seq=seq, batch=batch)
    return x4.reshape(batch, seq, d)
```

```python
import functools
import math

import jax
import jax.numpy as jnp
from jax import lax
from jax.experimental import pallas as pl
from jax.experimental.pallas import tpu as pltpu

F32 = jnp.float32
BF16 = jnp.bfloat16

D_MODEL = 2048
GRID_W = 64
EPS = 1e-6

MIX_A = 1024
SGU_CHUNK = 128
A_GROUPS = 8
MIX_B = 1024
B_HEAD = 64
B_HEADS = 16
DECAY_LORA = 64
AAA_LORA = 64
GATE_LORA = 160
B_WIDTH = 3 * MIX_B + DECAY_LORA + AAA_LORA + GATE_LORA
GN_EPS = B_HEAD * 1e-5

C_HEADS = 16
C_HEAD = 64
ROPE_AXIS = C_HEAD // 2
ROPE_THETA = 10000.0
SUBLN_EPS = 1e-5

N_EXPERTS = 8
D_EXPERT = 7168

LOG2E = 1.4426950408889634
LANES = 128
VMEM_LIMIT = 52 * 1024 * 1024

RW_CHUNK = 64
INV_BLOCK = 16
RW_TILE = 256
RW_PAIRS = B_HEADS // 2
Z_WIDTH = 3584
IN_PAD = 2 * MIX_A + Z_WIDTH


def _cparams(sem):
    return pltpu.CompilerParams(dimension_semantics=sem, vmem_limit_bytes=VMEM_LIMIT)


def _group_of_tile(i, tm, n_lat, seq, batch):
    return jnp.where(i * tm < n_lat, (i * tm) // seq, batch)


def _norm_mod_kernel(x_ref, g_ref, mod_ref, o_ref, *, s_idx, c_idx):
    x = x_ref[...]
    ms = jnp.mean(x * x, axis=-1, keepdims=True)
    y = x * lax.rsqrt(ms + EPS) * g_ref[...]
    shift = mod_ref[0, s_idx:s_idx + 1, :]
    scale = mod_ref[0, c_idx:c_idx + 1, :]
    o_ref[...] = (y * (1.0 + scale) + shift).astype(o_ref.dtype)


def norm_mod(x, gain, mods, s_idx, c_idx, *, n_lat, seq, batch, tm=512):
    m, d = x.shape
    grp = functools.partial(_group_of_tile, tm=tm, n_lat=n_lat, seq=seq, batch=batch)
    return pl.pallas_call(
        functools.partial(_norm_mod_kernel, s_idx=s_idx, c_idx=c_idx),
        grid=(m // tm,),
        in_specs=[
            pl.BlockSpec((tm, d), lambda i: (i, 0)),
            pl.BlockSpec((1, d), lambda i: (0, 0)),
            pl.BlockSpec((1, 6, d), lambda i: (grp(i), 0, 0)),
        ],
        out_specs=pl.BlockSpec((tm, d), lambda i: (i, 0)),
        out_shape=jax.ShapeDtypeStruct((m, d), BF16),
        compiler_params=_cparams(("parallel",)),
        name="norm_mod",
    )(x, gain.reshape(1, d), mods)


def _mm_kernel(*refs, kind, nk, silu_x, has_bias, gate_idx):
    refs = list(refs)
    x_ref = refs.pop(0)
    w_refs = [refs.pop(0)]
    if kind == "swiglu":
        w_refs.append(refs.pop(0))
    bias_ref = refs.pop(0) if has_bias else None
    res_ref = mod_ref = None
    if kind == "res":
        res_ref = refs.pop(0)
        mod_ref = refs.pop(0)
    o_ref = refs.pop(0)
    acc_refs = refs

    x = x_ref[...]
    if silu_x:
        x = jax.nn.silu(x.astype(F32))
    x = x.astype(BF16)
    prods = [jnp.dot(x, w_ref[...].astype(BF16), preferred_element_type=F32) for w_ref in w_refs]

    def epilogue(vals):
        a = vals[0]
        if kind == "swiglu":
            a = jax.nn.silu(a) * vals[1]
        if has_bias:
            a = a + bias_ref[...]
        if kind == "res":
            a = res_ref[...] + mod_ref[0, gate_idx:gate_idx + 1, :] * a
        o_ref[...] = a.astype(o_ref.dtype)

    if nk == 1:
        epilogue(prods)
        return

    k = pl.program_id(2)

    @pl.when(k == 0)
    def _():
        for acc, p in zip(acc_refs, prods):
            acc[...] = p

    @pl.when(k > 0)
    def _():
        for acc, p in zip(acc_refs, prods):
            acc[...] += p

    @pl.when(k == nk - 1)
    def _():
        epilogue([acc[...] for acc in acc_refs])


def matmul(x, w, *, w3=None, bias=None, res=None, mods=None, gate_idx=0, out_dtype=F32, silu_x=False,
           tm=1024, tn=512, tk=None, group=None):
    m, kdim = x.shape
    n = w.shape[1]
    tm = min(tm, m)
    tk = kdim if tk is None else tk
    nk = kdim // tk
    kind = "swiglu" if w3 is not None else ("res" if res is not None else "plain")
    in_specs = [pl.BlockSpec((tm, tk), lambda i, j, k: (i, k)),
                pl.BlockSpec((tk, tn), lambda i, j, k: (k, j))]
    args = [x, w]
    if w3 is not None:
        in_specs.append(pl.BlockSpec((tk, tn), lambda i, j, k: (k, j)))
        args.append(w3)
    if bias is not None:
        in_specs.append(pl.BlockSpec((1, tn), lambda i, j, k: (0, j)))
        args.append(bias.reshape(1, n))
    if res is not None:
        in_specs.append(pl.BlockSpec((tm, tn), lambda i, j, k: (i, j)))
        in_specs.append(pl.BlockSpec((1, 6, tn), lambda i, j, k: (group(i), 0, j)))
        args += [res, mods]
    n_acc = 0 if nk == 1 else (2 if w3 is not None else 1)
    return pl.pallas_call(
        functools.partial(_mm_kernel, kind=kind, nk=nk, silu_x=silu_x, has_bias=bias is not None,
                          gate_idx=gate_idx),
        grid=(m // tm, n // tn, nk),
        in_specs=in_specs,
        out_specs=pl.BlockSpec((tm, tn), lambda i, j, k: (i, j)),
        out_shape=jax.ShapeDtypeStruct((m, n), out_dtype),
        scratch_shapes=[pltpu.VMEM((tm, tn), F32)] * n_acc,
        compiler_params=_cparams(("parallel", "parallel", "arbitrary")),
        name="matmul_" + kind,
    )(*args)


def _sgu_kernel(u_ref, v_ref, ng_ref, ws_ref, bs_ref, o_ref, *, chunks):
    for c in range(chunks):
        rows = slice(c * SGU_CHUNK, (c + 1) * SGU_CHUNK)
        for g in range(A_GROUPS):
            cols = slice(g * LANES, (g + 1) * LANES)
            vg = jax.nn.gelu(v_ref[rows, cols])
            ms = jnp.mean(vg * vg, axis=-1, keepdims=True)
            vn = vg * lax.rsqrt(ms + EPS) * ng_ref[:, cols]
            s = jnp.dot(ws_ref[g].astype(BF16), vn.astype(BF16), preferred_element_type=F32)
            u = jax.nn.gelu(u_ref[rows, cols])
            o_ref[rows, cols] = (u * (s + bs_ref[:, cols])).astype(o_ref.dtype)


def chunk_sgu(hp, norm_g, ws, bs, *, tm=256):
    m = hp.shape[0]
    bs_exp = jnp.repeat(bs.T, LANES, axis=1)
    return pl.pallas_call(
        functools.partial(_sgu_kernel, chunks=tm // SGU_CHUNK),
        grid=(m // tm,),
        in_specs=[
            pl.BlockSpec((tm, MIX_A), lambda i: (i, 0)),
            pl.BlockSpec((tm, MIX_A), lambda i: (i, 1)),
            pl.BlockSpec((1, MIX_A), lambda i: (0, 0)),
            pl.BlockSpec((A_GROUPS, SGU_CHUNK, SGU_CHUNK), lambda i: (0, 0, 0)),
            pl.BlockSpec((SGU_CHUNK, MIX_A), lambda i: (0, 0)),
        ],
        out_specs=pl.BlockSpec((tm, MIX_A), lambda i: (i, 0)),
        out_shape=jax.ShapeDtypeStruct((m, MIX_A), BF16),
        compiler_params=_cparams(("parallel",)),
        name="chunk_sgu",
    )(hp, hp, norm_g.reshape(1, MIX_A), ws, bs_exp)


def _shift_kernel(x_ref, p_ref, n_ref, w_ref, o_ref, *, ts, lat_tiles, tiles_per_lat, tiles_per_ctx):
    i = pl.program_id(0)
    is_lat = i < lat_tiles
    pos = jnp.where(is_lat, i % tiles_per_lat, (i - lat_tiles) % tiles_per_ctx)
    last = jnp.where(is_lat, tiles_per_lat - 1, tiles_per_ctx - 1)
    x = x_ref[...]
    prev_row = jnp.where(pos == 0, 0.0, p_ref[7:8, :])
    next_row = jnp.where(pos == last, 0.0, n_ref[0:1, :])
    rows = lax.broadcasted_iota(jnp.int32, x.shape, 0)
    xm = jnp.where(rows == 0, prev_row, pltpu.roll(x, 1, axis=0))
    xp = jnp.where(rows == ts - 1, next_row, pltpu.roll(x, ts - 1, axis=0))
    o_ref[...] = w_ref[0:1, :] * xm + w_ref[1:2, :] * x + w_ref[2:3, :] * xp


def token_shift(hp, shift_w, *, n_lat, seq, ctx_len, ts=256, tc=512):
    m = hp.shape[0]
    col0 = (2 * MIX_A) // tc
    halo = 8
    nblk8 = m // halo
    w_pad = jnp.pad(shift_w, ((0, 0), (0, Z_WIDTH - B_WIDTH)))
    kern = functools.partial(_shift_kernel, ts=ts, lat_tiles=n_lat // ts, tiles_per_lat=seq // ts,
                             tiles_per_ctx=ctx_len // ts)
    return pl.pallas_call(
        kern,
        grid=(m // ts, Z_WIDTH // tc),
        in_specs=[
            pl.BlockSpec((ts, tc), lambda i, j: (i, col0 + j)),
            pl.BlockSpec((halo, tc), lambda i, j: (jnp.maximum(i * (ts // halo) - 1, 0), col0 + j)),
            pl.BlockSpec((halo, tc), lambda i, j: (jnp.minimum((i + 1) * (ts // halo), nblk8 - 1), col0 + j)),
            pl.BlockSpec((3, tc), lambda i, j: (0, j)),
        ],
        out_specs=pl.BlockSpec((ts, tc), lambda i, j: (i, j)),
        out_shape=jax.ShapeDtypeStruct((m, Z_WIDTH), F32),
        compiler_params=_cparams(("parallel", "parallel")),
        name="token_shift",
    )(hp, hp, hp, w_pad)


def _mask_lanes(x, m0):
    z = jnp.zeros_like(x)
    return jnp.concatenate([jnp.where(m0, x, z), jnp.where(m0, z, x)], axis=0)


def _dot(a, b):
    return jnp.dot(a.astype(BF16), b.astype(BF16), preferred_element_type=F32)


def _dot_nt(a, b):
    return lax.dot_general(a.astype(BF16), b.astype(BF16), (((1,), (1,)), ((), ())),
                           preferred_element_type=F32)


def _dot_tn(a, b):
    return lax.dot_general(a.astype(BF16), b.astype(BF16), (((0,), (0,)), ((), ())),
                           preferred_element_type=F32)


def _head_sums(x, m0):
    s0 = jnp.sum(jnp.where(m0, x, 0.0), axis=-1, keepdims=True)
    s1 = jnp.sum(jnp.where(m0, 0.0, x), axis=-1, keepdims=True)
    return jnp.where(m0, s0, s1)


def _rwkv_kernel(*refs, reverse, n_chunks):
    if reverse:
        (zr_ref, zk_ref, zv_ref, zl_ref, w0_ref, wt_ref, a0_ref, wa_ref, kk_ref, ka_ref, rk_ref,
         y0_ref, g_ref, bon0_ref, lnw_ref, lnb_ref, out_ref,
         s_ref, kk_s, bt_s, kd_s, rt_s, v_s, y_s, ge_s) = refs
    else:
        (zr_ref, zk_ref, zv_ref, zl_ref, w0_ref, wt_ref, a0_ref, wa_ref, kk_ref, ka_ref, rk_ref, wg_ref,
         y_out_ref, g_out_ref, bon_out_ref,
         s_ref, kk_s, bt_s, kd_s, rt_s, v_s, y_s, ge_s) = refs
    ch = RW_CHUNK
    j = pl.program_id(1)

    @pl.when(j == 0)
    def _():
        s_ref[...] = jnp.zeros_like(s_ref)

    zl = zl_ref[...]
    lora_in = zl[:, 0:LANES]
    wl = w0_ref[...] + _dot(jnp.tanh(lora_in), wt_ref[...])
    wlog = -(jnp.maximum(-wl, 0.0) + jnp.log(1.0 + jnp.exp(-jnp.abs(wl)))) - 0.5
    lw = -jnp.exp(wlog)
    a = jax.nn.sigmoid(a0_ref[...] + _dot(lora_in, wa_ref[...]))
    r = zr_ref[...]
    k = zk_ref[...]
    v = zv_ref[...]
    kd = k * (1.0 + (a - 1.0) * ka_ref[...])
    kraw = k * kk_ref[...]
    rkd = r * kd * rk_ref[...]
    if not reverse:
        g_out_ref[...] = _dot(jax.nn.sigmoid(zl[:, LANES:3 * LANES]), wg_ref[...])

    lane = lax.broadcasted_iota(jnp.int32, (1, LANES), 1)
    m0 = lane < B_HEAD
    ti = lax.broadcasted_iota(jnp.int32, (ch, ch), 0)
    si = lax.broadcasted_iota(jnp.int32, (ch, ch), 1)
    tri = ((si >= ti) if reverse else (si <= ti)).astype(BF16)

    kn_l, b_l = [], []
    for p in range(RW_PAIRS):
        cols = slice(p * LANES, (p + 1) * LANES)
        kp = kraw[:, cols]
        nrm = jnp.sqrt(_head_sums(kp * kp, m0))
        kn = kp / jnp.maximum(nrm, 1e-12)
        kn_l.append(kn)
        b_l.append(kn * a[:, cols])
        v_s[p] = v[:, cols]
        if not reverse:
            bon_out_ref[:, cols] = _head_sums(rkd[:, cols], m0) * v[:, cols]

    for c in range(n_chunks):
        rws = slice(c * ch, (c + 1) * ch)
        lw_c = lw[rws, :]
        hi = lw_c.astype(BF16)
        rem = lw_c - hi.astype(F32)
        mid = rem.astype(BF16)
        lo = (rem - mid.astype(F32)).astype(BF16)
        cs = (jnp.dot(tri, hi, preferred_element_type=F32) + jnp.dot(tri, mid, preferred_element_type=F32)
              + jnp.dot(tri, lo, preferred_element_type=F32))
        total = cs[0:1, :] if reverse else cs[ch - 1:ch, :]
        g_in = jnp.exp(cs)
        g_ex = jnp.exp(cs - lw_c)
        g_inv = jnp.exp(-cs)
        g_end = jnp.exp(total)
        for p in range(RW_PAIRS):
            cols = slice(p * LANES, (p + 1) * LANES)
            kk_s[p, rws, :] = kn_l[p][rws, :] * g_ex[:, cols]
            bt_s[p, rws, :] = b_l[p][rws, :] * g_inv[:, cols]
            kd_s[p, rws, :] = kd[rws, cols] * g_inv[:, cols]
            rt_s[p, rws, :] = r[rws, cols] * g_in[:, cols]
            ge_s[p, c:c + 1, :] = g_end[:, cols]

    ri = lax.broadcasted_iota(jnp.int32, (2 * ch, 2 * ch), 0)
    ci = lax.broadcasted_iota(jnp.int32, (2 * ch, 2 * ch), 1)
    same_head = (ri // ch) == (ci // ch)
    if reverse:
        strict = same_head & (ci > ri)
        incl = same_head & (ci >= ri)
    else:
        strict = same_head & (ci < ri)
        incl = same_head & (ci <= ri)
    eye = (ri == ci).astype(F32)
    zero = jnp.zeros((2 * ch, 2 * ch), F32)
    n_sq = int(math.log2(INV_BLOCK)) - 1
    blk_diag = []
    bs = INV_BLOCK
    while bs <= ch:
        blk_diag.append((ri // bs) == (ci // bs))
        bs *= 2
    pairs = range(RW_PAIRS)

    def chunk_body(ci_, carry):
        c = (n_chunks - 1 - ci_) if reverse else ci_
        rows = pl.ds(pl.multiple_of(c * ch, ch), ch)
        kk_t = [kk_s[p, rows, :] for p in pairs]
        rt = [rt_s[p, rows, :] for p in pairs]
        bt_m = [_mask_lanes(bt_s[p, rows, :], m0) for p in pairs]
        kd_m = [_mask_lanes(kd_s[p, rows, :], m0) for p in pairs]
        v_st = [_mask_lanes(v_s[p, rows, :], m0) for p in pairs]
        g_end = [ge_s[p, pl.ds(c, 1), :] for p in pairs]

        pq = [_dot_nt(jnp.concatenate([kk_t[p], kk_t[p], rt[p], rt[p]], axis=0),
                      jnp.concatenate([bt_m[p], kd_m[p]], axis=0)) for p in pairs]
        a_m = [jnp.where(strict, pq[p][0:2 * ch, 0:2 * ch], zero) for p in pairs]
        b_m = [jnp.where(strict, pq[p][0:2 * ch, 2 * ch:4 * ch], zero) for p in pairs]
        m2 = [jnp.where(incl, pq[p][2 * ch:4 * ch, 0:2 * ch], zero) for p in pairs]
        m1 = [jnp.where(incl, pq[p][2 * ch:4 * ch, 2 * ch:4 * ch], zero) for p in pairs]

        q = [jnp.where(blk_diag[0], -a_m[p], zero) for p in pairs]
        t_m = [eye + q[p] for p in pairs]
        q = [_dot(q[p], q[p]) for p in pairs]
        for _ in range(n_sq - 1):
            both = [_dot(jnp.concatenate([q[p], t_m[p]], axis=0), q[p]) for p in pairs]
            q = [both[p][0:2 * ch] for p in pairs]
            t_m = [t_m[p] + both[p][2 * ch:4 * ch] for p in pairs]
        t_m = [t_m[p] + _dot(t_m[p], q[p]) for p in pairs]
        for lvl in range(1, len(blk_diag)):
            off = [jnp.where(blk_diag[lvl] & ~blk_diag[lvl - 1], a_m[p], zero) for p in pairs]
            t_off = [_dot(t_m[p], off[p]) for p in pairs]
            t_m = [t_m[p] - _dot(t_off[p], t_m[p]) for p in pairs]

        bv = [_dot(b_m[p], v_st[p]) for p in pairs]
        ku = [_dot(t_m[p], jnp.concatenate([_mask_lanes(kk_t[p], m0), bv[p]], axis=1)) for p in pairs]
        m2ku = [_dot(m2[p], ku[p]) for p in pairs]
        m1v = [_dot(m1[p], v_st[p]) for p in pairs]
        s_bd = [s_ref[p] for p in pairs]
        ys = []
        for p in pairs:
            rp_s = _mask_lanes(rt[p], m0) - m2ku[p][:, 0:LANES]
            y0_s = m1v[p] - m2ku[p][:, LANES:2 * LANES]
            rp = rp_s[0:ch] + rp_s[ch:2 * ch]
            y0 = y0_s[0:ch] + y0_s[ch:2 * ch]
            ys.append(_dot_nt(rp, s_bd[p]) + y0)
        u_all = [_dot_nt(ku[p][:, 0:LANES], s_bd[p]) + ku[p][:, LANES:2 * LANES] for p in pairs]
        for p in pairs:
            upd = _dot_tn(jnp.concatenate([v_st[p], -u_all[p]], axis=0),
                          jnp.concatenate([kd_m[p] * g_end[p], bt_m[p] * g_end[p]], axis=0))
            s_ref[p] = s_bd[p] * g_end[p] + upd
            y_s[p, rows, :] = ys[p]
        return carry

    lax.fori_loop(0, n_chunks, chunk_body, 0)

    for p in range(RW_PAIRS):
        cols = slice(p * LANES, (p + 1) * LANES)
        if not reverse:
            y_out_ref[:, cols] = y_s[p]
        else:
            bon1 = _head_sums(rkd[:, cols], m0) * v[:, cols]
            ysum = y0_ref[:, cols] + y_s[p]
            mu = _head_sums(ysum, m0) * (1.0 / B_HEAD)
            dlt = ysum - mu
            var = _head_sums(dlt * dlt, m0) * (1.0 / B_HEAD)
            yn = dlt * lax.rsqrt(var + GN_EPS)
            o = (yn * lnw_ref[:, cols] + lnb_ref[:, cols] + bon0_ref[:, cols] + bon1) * g_ref[:, cols]
            out_ref[:, cols] = o.astype(out_ref.dtype)


def rwkv7_bidir(zs, w0, w_up, a0, a_up, g_up, k_k, k_a, r_k, lnx_w, lnx_b, *, batch, seq, ctx_len):
    m = zs.shape[0]
    n_lat = batch * seq
    tb = RW_TILE
    nct, nlt = ctx_len // tb, seq // tb
    steps = nct + nlt

    def row_block(reverse):
        def f(b, j):
            jc = (nct - 1 - j) if reverse else j
            jl = (nlt - 1 - (j - nct)) if reverse else (j - nct)
            return jnp.where(j < nct, (n_lat + b * ctx_len) // tb + jc, b * nlt + jl)
        return f

    def pad_rows(w, top, total):
        return jnp.pad(w, ((top, total - top - w.shape[0]), (0, 0)))

    wg = pad_rows(g_up, 0, 2 * LANES)
    rk = r_k.reshape(1, MIX_B)
    vec = lambda a: a.reshape(1, MIX_B)
    scratch = ([pltpu.VMEM((RW_PAIRS, LANES, LANES), F32)] + [pltpu.VMEM((RW_PAIRS, tb, LANES), F32)] * 6
               + [pltpu.VMEM((RW_PAIRS, 8, LANES), F32)])
    const = lambda shape: pl.BlockSpec(shape, lambda b, j: (0,) * len(shape))

    def call(reverse, extra_in, extra_specs, out_shapes, out_specs, d):
        rb = row_block(reverse)
        wt = pad_rows(w_up[d], 0, LANES)
        wa = pad_rows(a_up[d], DECAY_LORA, LANES)
        in_specs = [
            pl.BlockSpec((tb, MIX_B), lambda b, j: (rb(b, j), 0)),
            pl.BlockSpec((tb, MIX_B), lambda b, j: (rb(b, j), 1)),
            pl.BlockSpec((tb, MIX_B), lambda b, j: (rb(b, j), 2)),
            pl.BlockSpec((tb, 512), lambda b, j: (rb(b, j), 6)),
            const((1, MIX_B)), const((LANES, MIX_B)), const((1, MIX_B)), const((LANES, MIX_B)),
            const((1, MIX_B)), const((1, MIX_B)), const((1, MIX_B)),
        ] + extra_specs
        args = [zs, zs, zs, zs, vec(w0[d]), wt, vec(a0[d]), wa, vec(k_k), vec(k_a), rk] + extra_in
        return pl.pallas_call(
            functools.partial(_rwkv_kernel, reverse=reverse, n_chunks=tb // RW_CHUNK),
            grid=(batch, steps),
            in_specs=in_specs,
            out_specs=out_specs,
            out_shape=out_shapes,
            scratch_shapes=scratch,
            compiler_params=_cparams(("parallel", "arbitrary")),
            name="rwkv7_rev" if reverse else "rwkv7_fwd",
        )(*args)

    rbf = row_block(False)
    tile_f = pl.BlockSpec((tb, MIX_B), lambda b, j: (rbf(b, j), 0))
    y0, g, bon0 = call(False, [wg], [const((2 * LANES, MIX_B))],
                       [jax.ShapeDtypeStruct((m, MIX_B), F32)] * 3, [tile_f] * 3, 0)
    rbr = row_block(True)
    tile_r = pl.BlockSpec((tb, MIX_B), lambda b, j: (rbr(b, j), 0))
    out = call(True, [y0, g, bon0, vec(lnx_w), vec(lnx_b)],
               [tile_r, tile_r, tile_r, const((1, MIX_B)), const((1, MIX_B))],
               jax.ShapeDtypeStruct((m, MIX_B), BF16), tile_r, 1)
    return out


def _half_rms(x, gain, m0):
    ms = _head_sums(x * x, m0) * (1.0 / C_HEAD)
    return x * lax.rsqrt(ms + EPS) * gain


def _rope(x, cos, sin):
    lane = lax.broadcasted_iota(jnp.int32, x.shape, 1)
    first = (lane % ROPE_AXIS) < (ROPE_AXIS // 2)
    half = ROPE_AXIS // 2
    rot = jnp.where(first, -pltpu.roll(x, LANES - half, axis=1), pltpu.roll(x, half, axis=1))
    return x * cos + rot * sin


def _attn_kernel(q_ref, kl_ref, vl_ref, kc_ref, vc_ref, cq_ref, sq_ref, ck_ref, sk_ref, qg_ref, kg_ref,
                 lam_ref, sg_ref, o_ref, k_s, v_s, *, ctx_len, lam_init, q_blk):
    qi = pl.program_id(2)
    lane = lax.broadcasted_iota(jnp.int32, (1, LANES), 1)
    m0 = lane < C_HEAD

    @pl.when(qi == 0)
    def _():
        kc = _half_rms(kc_ref[...], kg_ref[...], m0)
        kl = _rope(_half_rms(kl_ref[...], kg_ref[...], m0), ck_ref[...], sk_ref[...])
        k_s[0:ctx_len, :] = kc.astype(BF16)
        k_s[ctx_len:, :] = kl.astype(BF16)
        v_s[0:ctx_len, :] = vc_ref[...].astype(BF16)
        v_s[ctx_len:, :] = vl_ref[...].astype(BF16)

    lp = lam_ref[...]
    lam = (jnp.exp(jnp.sum(lp[0:1] * lp[1:2], keepdims=True)) - jnp.exp(jnp.sum(lp[2:3] * lp[3:4], keepdims=True))
           + lam_init)
    q = _rope(_half_rms(q_ref[...], qg_ref[...], m0), cq_ref[...], sq_ref[...]) * (C_HEAD ** -0.5 * LOG2E)
    keys = k_s[...]
    vals = v_s[...]
    zq = jnp.zeros_like(q)
    q_sub = [jnp.where(m0, q, zq), jnp.where(m0, zq, q)]
    blocks = [(slice(r0, r0 + q_blk), i) for r0 in range(0, q.shape[0], q_blk) for i in (0, 1)]
    scores = [_dot_nt(q_sub[i][rows], keys) for rows, i in blocks]
    outs = []
    for s, (rows, i) in zip(scores, blocks):
        e = jnp.exp2(s - jnp.max(s, axis=-1, keepdims=True))
        w = (lam if i else 1.0) / jnp.sum(e, axis=-1, keepdims=True)
        outs.append(jnp.dot(e.astype(BF16), vals, preferred_element_type=F32) * w)
    for n, r0 in enumerate(range(0, q.shape[0], q_blk)):
        o = outs[2 * n] - outs[2 * n + 1]
        ms = jnp.mean(o * o, axis=-1, keepdims=True)
        o = o * lax.rsqrt(ms + SUBLN_EPS) * sg_ref[...] * (1.0 - lam_init)
        o_ref[r0:r0 + q_blk, :] = o.astype(o_ref.dtype)


def _rope_tables(n):
    rows = n // GRID_W
    row = jnp.broadcast_to(jnp.arange(rows, dtype=F32)[:, None], (rows, GRID_W)).reshape(-1)
    col = jnp.broadcast_to(jnp.arange(GRID_W, dtype=F32)[None, :], (rows, GRID_W)).reshape(-1)
    inv = ROPE_THETA ** (-jnp.arange(0, ROPE_AXIS, 2, dtype=F32) / ROPE_AXIS)
    ar = row[:, None] * inv
    ac = col[:, None] * inv
    ang = jnp.concatenate([ar, ar, ac, ac, ar, ar, ac, ac], axis=-1)
    return jnp.cos(ang), jnp.sin(ang)


def diff_attention(qkv, q_g, k_g, lam_params, subln_g, lam_init, *, batch, seq, ctx_len, tq=512):
    n_lat = batch * seq
    tq = min(tq, seq)
    cos, sin = _rope_tables(seq)
    nq = seq // tq
    hq, hk, hv = 0, D_MODEL // LANES, 2 * D_MODEL // LANES
    ctx_blk0 = n_lat // ctx_len
    two = lambda a: jnp.concatenate([a, a]).reshape(1, LANES)
    const = lambda shape: pl.BlockSpec(shape, lambda b, h, i: (0,) * len(shape))
    return pl.pallas_call(
        functools.partial(_attn_kernel, ctx_len=ctx_len, lam_init=lam_init, q_blk=min(256, tq)),
        grid=(batch, C_HEADS, nq),
        in_specs=[
            pl.BlockSpec((tq, LANES), lambda b, h, i: (b * nq + i, hq + h)),
            pl.BlockSpec((seq, LANES), lambda b, h, i: (b, hk + h)),
            pl.BlockSpec((seq, LANES), lambda b, h, i: (b, hv + h)),
            pl.BlockSpec((ctx_len, LANES), lambda b, h, i: (ctx_blk0 + b, hk + h)),
            pl.BlockSpec((ctx_len, LANES), lambda b, h, i: (ctx_blk0 + b, hv + h)),
            pl.BlockSpec((tq, LANES), lambda b, h, i: (i, 0)),
            pl.BlockSpec((tq, LANES), lambda b, h, i: (i, 0)),
            const((seq, LANES)), const((seq, LANES)),
            const((1, LANES)), const((1, LANES)), const((4, C_HEAD)), const((1, LANES)),
        ],
        out_specs=pl.BlockSpec((tq, LANES), lambda b, h, i: (b * nq + i, h)),
        out_shape=jax.ShapeDtypeStruct((n_lat, D_MODEL), BF16),
        scratch_shapes=[pltpu.VMEM((ctx_len + seq, LANES), BF16)] * 2,
        compiler_params=_cparams(("parallel", "parallel", "arbitrary")),
        name="diff_attention",
    )(qkv, qkv, qkv, qkv, qkv, cos, sin, cos, sin, two(q_g), two(k_g), lam_params, subln_g.reshape(1, LANES))


def _route_kernel(x_ref, g_ref, mod_ref, rw_ref, xn_ref, idx_ref, gate_ref, *, s_idx, c_idx):
    x = x_ref[...]
    ms = jnp.mean(x * x, axis=-1, keepdims=True)
    y = x * lax.rsqrt(ms + EPS) * g_ref[...]
    h = y * (1.0 + mod_ref[0, c_idx:c_idx + 1, :]) + mod_ref[0, s_idx:s_idx + 1, :]
    xn_ref[...] = h
    logits = jnp.dot(h, rw_ref[...], precision=lax.Precision.HIGHEST, preferred_element_type=F32)
    lane = lax.broadcasted_iota(jnp.int32, logits.shape, 1)
    neg = jnp.float32(-jnp.inf)
    lg = jnp.where(lane < N_EXPERTS, logits, neg)
    m1 = jnp.max(lg, axis=-1, keepdims=True)
    i1 = jnp.min(jnp.where(lg == m1, lane, LANES), axis=-1, keepdims=True)
    lg2 = jnp.where(lane == i1, neg, lg)
    m2 = jnp.max(lg2, axis=-1, keepdims=True)
    i2 = jnp.min(jnp.where(lg2 == m2, lane, LANES), axis=-1, keepdims=True)
    e2 = jnp.exp(m2 - m1)
    g1 = 1.0 / (1.0 + e2)
    g2 = e2 * g1
    idx_ref[...] = jnp.where(lane == 0, i1, jnp.where(lane == 1, i2, 0))
    gate_ref[...] = jnp.where(lane == 0, g1, jnp.where(lane == 1, g2, 0.0))


def route(x, gain, mods, router, s_idx, c_idx, *, seq, batch, tm=256):
    m, d = x.shape
    rw = jnp.pad(router, ((0, 0), (0, LANES - N_EXPERTS)))
    return pl.pallas_call(
        functools.partial(_route_kernel, s_idx=s_idx, c_idx=c_idx),
        grid=(m // tm,),
        in_specs=[
            pl.BlockSpec((tm, d), lambda i: (i, 0)),
            pl.BlockSpec((1, d), lambda i: (0, 0)),
            pl.BlockSpec((1, 6, d), lambda i: ((i * tm) // seq, 0, 0)),
            pl.BlockSpec((d, LANES), lambda i: (0, 0)),
        ],
        out_specs=[pl.BlockSpec((tm, d), lambda i: (i, 0)),
                   pl.BlockSpec((tm, LANES), lambda i: (i, 0)),
                   pl.BlockSpec((tm, LANES), lambda i: (i, 0))],
        out_shape=[jax.ShapeDtypeStruct((m, d), F32),
                   jax.ShapeDtypeStruct((m, LANES), jnp.int32),
                   jax.ShapeDtypeStruct((m, LANES), F32)],
        compiler_params=_cparams(("parallel",)),
        name="moe_route",
    )(x, gain.reshape(1, d), mods, rw)


def _row_copy(src_hbm, dst_ref, src_row, dst_row, sem):
    return pltpu.make_async_copy(src_hbm.at[pl.ds(src_row, 1)], dst_ref.at[pl.ds(dst_row, 1)], sem)


def _gather_kernel(used_ref, idx_ref, src_hbm, o_ref, buf, sem, *, tg):
    i = pl.program_id(0)

    @pl.when(i * tg >= used_ref[0])
    def _():
        o_ref[...] = jnp.zeros_like(o_ref)

    @pl.when(i * tg < used_ref[0])
    def _():
        def issue(r, c):
            _row_copy(src_hbm, buf, idx_ref[0, 0, r], r, sem).start()
            return c

        def drain(r, c):
            _row_copy(src_hbm, buf, 0, r, sem).wait()
            return c

        lax.fori_loop(0, tg, issue, 0, unroll=8)
        lax.fori_loop(0, tg, drain, 0, unroll=8)
        o_ref[...] = buf[...].astype(o_ref.dtype)


def gather_rows(src, idx, rows_used, *, tg=256):
    mp = idx.shape[0]
    d = src.shape[1]
    return pl.pallas_call(
        functools.partial(_gather_kernel, tg=tg),
        grid_spec=pltpu.PrefetchScalarGridSpec(
            num_scalar_prefetch=1,
            grid=(mp // tg,),
            in_specs=[pl.BlockSpec((1, 1, tg), lambda i, u: (i, 0, 0), memory_space=pltpu.SMEM),
                      pl.BlockSpec(memory_space=pl.ANY)],
            out_specs=pl.BlockSpec((tg, d), lambda i, u: (i, 0)),
            scratch_shapes=[pltpu.VMEM((tg, d), src.dtype), pltpu.SemaphoreType.DMA(())],
        ),
        out_shape=jax.ShapeDtypeStruct((mp, d), BF16),
        compiler_params=_cparams(("arbitrary",)),
        name="moe_gather",
    )(rows_used, idx.reshape(mp // tg, 1, tg), src)


def _gmm_kernel(te_ref, nu_ref, *refs, kind, nk):
    refs = list(refs)
    x_ref = refs.pop(0)
    w_refs = [refs.pop(0)]
    if kind == "swiglu":
        w_refs.append(refs.pop(0))
    o_ref = refs.pop(0)
    acc_refs = refs
    i = pl.program_id(0)
    k = pl.program_id(2)

    @pl.when(i >= nu_ref[0])
    def _():
        o_ref[...] = jnp.zeros_like(o_ref)

    @pl.when(i < nu_ref[0])
    def _():
        x = x_ref[...].astype(BF16)
        prods = [jnp.dot(x, w_ref[0].astype(BF16), preferred_element_type=F32) for w_ref in w_refs]

        def epilogue(vals):
            a = vals[0]
            if kind == "swiglu":
                a = jax.nn.silu(a) * vals[1]
            o_ref[...] = a.astype(o_ref.dtype)

        if nk == 1:
            epilogue(prods)
        else:
            @pl.when(k == 0)
            def _():
                for acc, p in zip(acc_refs, prods):
                    acc[...] = p

            @pl.when(k > 0)
            def _():
                for acc, p in zip(acc_refs, prods):
                    acc[...] += p

            @pl.when(k == nk - 1)
            def _():
                epilogue([acc[...] for acc in acc_refs])


def grouped_matmul(x, w, tile_expert, n_used, *, w3=None, out_dtype=F32, tm=512, tn=512, tk=None):
    mp, kdim = x.shape
    n = w.shape[2]
    tk = kdim if tk is None else tk
    nk = kdim // tk
    kind = "swiglu" if w3 is not None else "plain"
    w_spec = pl.BlockSpec((1, tk, tn), lambda i, j, k, te, nu: (te[i], k, j))
    in_specs = [pl.BlockSpec((tm, tk), lambda i, j, k, te, nu: (i, k)), w_spec]
    args = [x, w]
    if w3 is not None:
        in_specs.append(w_spec)
        args.append(w3)
    n_acc = 0 if nk == 1 else 1
    return pl.pallas_call(
        functools.partial(_gmm_kernel, kind=kind, nk=nk),
        grid_spec=pltpu.PrefetchScalarGridSpec(
            num_scalar_prefetch=2,
            grid=(mp // tm, n // tn, nk),
            in_specs=in_specs,
            out_specs=pl.BlockSpec((tm, tn), lambda i, j, k, te, nu: (i, j)),
            scratch_shapes=[pltpu.VMEM((tm, tn), F32)] * n_acc,
        ),
        out_shape=jax.ShapeDtypeStruct((mp, n), out_dtype),
        compiler_params=_cparams(("parallel", "parallel", "arbitrary")),
        name="moe_gmm_" + kind,
    )(tile_expert, n_used, *args)


def _combine_kernel(pos_ref, ys_hbm, x_ref, gate_ref, mod_ref, o_ref, buf, sem, *, tc, gate_idx):
    def issue(r, c):
        _row_copy(ys_hbm, buf.at[0], pos_ref[0, 0, 2 * r], r, sem).start()
        _row_copy(ys_hbm, buf.at[1], pos_ref[0, 0, 2 * r + 1], r, sem).start()
        return c

    def drain(r, c):
        _row_copy(ys_hbm, buf.at[0], 0, r, sem).wait()
        _row_copy(ys_hbm, buf.at[1], 0, r, sem).wait()
        return c

    lax.fori_loop(0, tc, issue, 0, unroll=8)
    lax.fori_loop(0, tc, drain, 0, unroll=8)
    g = gate_ref[...]
    moe = g[:, 0:1] * buf[0] + g[:, 1:2] * buf[1]
    o_ref[...] = x_ref[...] + mod_ref[0, gate_idx:gate_idx + 1, :] * moe


def moe_combine(ys, pos, x, gates, mods, gate_idx, *, seq, tc=256):
    m, d = x.shape
    return pl.pallas_call(
        functools.partial(_combine_kernel, tc=tc, gate_idx=gate_idx),
        grid=(m // tc,),
        in_specs=[pl.BlockSpec((1, 1, 2 * tc), lambda i: (i, 0, 0), memory_space=pltpu.SMEM),
                  pl.BlockSpec(memory_space=pl.ANY),
                  pl.BlockSpec((tc, d), lambda i: (i, 0)),
                  pl.BlockSpec((tc, LANES), lambda i: (i, 0)),
                  pl.BlockSpec((1, 6, d), lambda i: ((i * tc) // seq, 0, 0))],
        out_specs=pl.BlockSpec((tc, d), lambda i: (i, 0)),
        out_shape=jax.ShapeDtypeStruct((m, d), F32),
        scratch_shapes=[pltpu.VMEM((2, tc, d), F32), pltpu.SemaphoreType.DMA(())],
        compiler_params=_cparams(("arbitrary",)),
        name="moe_combine",
    )(pos.reshape(m // tc, 1, 2 * tc), ys, x, gates, mods)


def moe_layer(x, gain, mods, router, w1, w3, w2, *, seq, batch, tm=1024):
    n = x.shape[0]
    xn, idx, gates = route(x, gain, mods, router, 3, 4, seq=seq, batch=batch)
    e_flat = idx[:, 0:2].reshape(-1)
    onehot = (e_flat[:, None] == jnp.arange(N_EXPERTS)[None, :]).astype(jnp.int32)
    ranks = jnp.cumsum(onehot, axis=0) - onehot
    rank = jnp.sum(ranks * onehot, axis=1)
    counts = jnp.sum(onehot, axis=0)
    padded = ((counts + tm - 1) // tm) * tm
    starts = jnp.cumsum(padded) - padded
    pos = starts[e_flat] + rank
    mp = 2 * n + N_EXPERTS * tm
    token_of_row = jnp.zeros((mp,), jnp.int32).at[pos].set(jnp.arange(2 * n, dtype=jnp.int32) // 2)
    n_tiles = mp // tm
    ends = jnp.cumsum(padded)
    tile_start = jnp.arange(n_tiles, dtype=jnp.int32) * tm
    tile_expert = jnp.minimum(jnp.sum((tile_start[:, None] >= ends[None, :]).astype(jnp.int32), axis=1),
                              N_EXPERTS - 1).astype(jnp.int32)
    n_used = (ends[-1] // tm).astype(jnp.int32).reshape(1)

    xs = gather_rows(xn, token_of_row, ends[-1].astype(jnp.int32).reshape(1))
    hs = grouped_matmul(xs, w1, tile_expert, n_used, w3=w3, out_dtype=BF16, tm=tm, tn=512)
    ys = grouped_matmul(hs, w2, tile_expert, n_used, out_dtype=F32, tm=tm, tn=1024, tk=D_EXPERT // 4)
    return moe_combine(ys, pos.astype(jnp.int32), x, gates, mods, 5, seq=seq)


def kernel(x, c, ctx, c_ctx, l0_ada_w, l0_ada_b, l0_norm1_g, l0_norm2_g, l0_w_in, l0_sgu_norm_g, l0_sgu_w, l0_sgu_b, l0_shift_w, l0_w0, l0_w_up, l0_a0, l0_a_up, l0_g_up, l0_k_k, l0_k_a, l0_r_k, l0_lnx_w, l0_lnx_b, l0_w_out, l0_ffn_w1, l0_ffn_w3, l0_ffn_w2, l1_ada_w, l1_ada_b, l1_norm1_g, l1_norm2_g, l1_w_qkv, l1_q_norm_g, l1_k_norm_g, l1_lam_q1, l1_lam_k1, l1_lam_q2, l1_lam_k2, l1_subln_g, l1_w_out, l1_router, l1_exp_w1, l1_exp_w3, l1_exp_w2):
    batch, seq, d = x.shape
    ctx_len = ctx.shape[1]
    n_lat = batch * seq
    tm = min(1024, seq, batch * ctx_len)
    geo = dict(n_lat=n_lat, seq=seq, batch=batch)
    grp = functools.partial(_group_of_tile, tm=tm, **geo)

    xall = jnp.concatenate([x.reshape(n_lat, d), ctx.reshape(batch * ctx_len, d)], axis=0)
    cond = jnp.concatenate([c, c_ctx[None, :], jnp.zeros((16 - batch - 1, d), F32)], axis=0)

    def ada(w, b):
        return matmul(cond, w, bias=b, silu_x=True, tn=1024).reshape(16, 6, d)

    mods = ada(l0_ada_w, l0_ada_b)
    hn = norm_mod(xall, l0_norm1_g, mods, 0, 1, tm=min(512, tm), **geo)
    w_in = jnp.pad(l0_w_in, ((0, 0), (0, IN_PAD - l0_w_in.shape[1])))
    hp = matmul(hn, w_in, tm=tm)
    a_out = chunk_sgu(hp, l0_sgu_norm_g, l0_sgu_w, l0_sgu_b)
    zs = token_shift(hp, l0_shift_w, n_lat=n_lat, seq=seq, ctx_len=ctx_len)
    b_out = rwkv7_bidir(zs, l0_w0, l0_w_up, l0_a0, l0_a_up, l0_g_up, l0_k_k, l0_k_a, l0_r_k, l0_lnx_w, l0_lnx_b,
                        batch=batch, seq=seq, ctx_len=ctx_len)
    mixed = jnp.concatenate([a_out, b_out], axis=1)
    x1 = matmul(mixed, l0_w_out, res=xall, mods=mods, gate_idx=2, group=grp, tm=tm)
    hn = norm_mod(x1, l0_norm2_g, mods, 3, 4, tm=min(512, tm), **geo)
    hff = matmul(hn, l0_ffn_w1, w3=l0_ffn_w3, out_dtype=BF16, tm=tm)
    x2 = matmul(hff, l0_ffn_w2, res=x1, mods=mods, gate_idx=5, group=grp, tm=tm, tn=512,
                tk=l0_ffn_w2.shape[0] // 2)

    mods = ada(l1_ada_w, l1_ada_b)
    hn = norm_mod(x2, l1_norm1_g, mods, 0, 1, tm=min(512, tm), **geo)
    qkv = matmul(hn, l1_w_qkv, tm=tm)
    lam_params = jnp.stack([l1_lam_q1, l1_lam_k1, l1_lam_q2, l1_lam_k2])
    lam_init = 0.8 - 0.6 * math.exp(-0.3 * 1)
    o = diff_attention(qkv, l1_q_norm_g, l1_k_norm_g, lam_params, l1_subln_g, lam_init,
                       batch=batch, seq=seq, ctx_len=ctx_len)
    x3 = matmul(o, l1_w_out, res=x2, mods=mods, gate_idx=2, group=grp, tm=tm)
    x4 = moe_layer(x3, l1_norm2_g, mods, l1_router, l1_exp_w1, l1_exp_w3, l1_exp_w2, seq=seq, batch=batch)
    return x4.reshape(batch, seq, d)
```

```python
import functools
import math

import jax
import jax.numpy as jnp
from jax import lax
from jax.experimental import pallas as pl
from jax.experimental.pallas import tpu as pltpu

F32 = jnp.float32
BF16 = jnp.bfloat16

D_MODEL = 2048
GRID_W = 64
EPS = 1e-6

MIX_A = 1024
SGU_CHUNK = 128
A_GROUPS = 8
MIX_B = 1024
B_HEAD = 64
B_HEADS = 16
DECAY_LORA = 64
AAA_LORA = 64
GATE_LORA = 160
B_WIDTH = 3 * MIX_B + DECAY_LORA + AAA_LORA + GATE_LORA
GN_EPS = B_HEAD * 1e-5

C_HEADS = 16
C_HEAD = 64
ROPE_AXIS = C_HEAD // 2
ROPE_THETA = 10000.0
SUBLN_EPS = 1e-5

N_EXPERTS = 8
D_EXPERT = 7168

LOG2E = 1.4426950408889634
LANES = 128
VMEM_LIMIT = 52 * 1024 * 1024
GMM_VMEM_LIMIT = 58 * 1024 * 1024

RW_CHUNK = 64
INV_BLOCK = 16
DMA_UNROLL = 8
RW_TILE = 256
RW_PAIRS = B_HEADS // 2
Z_WIDTH = 3584


def _cparams(sem):
    return pltpu.CompilerParams(dimension_semantics=sem, vmem_limit_bytes=VMEM_LIMIT)


def _group_of_tile(i, tm, n_lat, seq, batch):
    return jnp.where(i * tm < n_lat, (i * tm) // seq, batch)


def _norm_mod_kernel(x_ref, g_ref, mod_ref, o_ref, *, s_idx, c_idx):
    x = x_ref[...]
    ms = jnp.mean(x * x, axis=-1, keepdims=True)
    y = x * lax.rsqrt(ms + EPS) * g_ref[...]
    shift = mod_ref[0, s_idx:s_idx + 1, :]
    scale = mod_ref[0, c_idx:c_idx + 1, :]
    o_ref[...] = (y * (1.0 + scale) + shift).astype(o_ref.dtype)


def norm_mod(x, gain, mods, s_idx, c_idx, *, n_lat, seq, batch, tm=512):
    m, d = x.shape
    grp = functools.partial(_group_of_tile, tm=tm, n_lat=n_lat, seq=seq, batch=batch)
    return pl.pallas_call(
        functools.partial(_norm_mod_kernel, s_idx=s_idx, c_idx=c_idx),
        grid=(m // tm,),
        in_specs=[
            pl.BlockSpec((tm, d), lambda i: (i, 0)),
            pl.BlockSpec((1, d), lambda i: (0, 0)),
            pl.BlockSpec((1, 6, d), lambda i: (grp(i), 0, 0)),
        ],
        out_specs=pl.BlockSpec((tm, d), lambda i: (i, 0)),
        out_shape=jax.ShapeDtypeStruct((m, d), BF16),
        compiler_params=_cparams(("parallel",)),
        name="norm_mod",
    )(x, gain.reshape(1, d), mods)


def _mm_kernel(*refs, kind, nk, silu_x, has_bias, gate_idx):
    refs = list(refs)
    x_ref = refs.pop(0)
    w_refs = [refs.pop(0)]
    if kind == "swiglu":
        w_refs.append(refs.pop(0))
    bias_ref = refs.pop(0) if has_bias else None
    res_ref = mod_ref = None
    if kind == "res":
        res_ref = refs.pop(0)
        mod_ref = refs.pop(0)
    o_ref = refs.pop(0)
    acc_refs = refs

    x = x_ref[...]
    if silu_x:
        x = jax.nn.silu(x.astype(F32))
    x = x.astype(BF16)
    prods = [jnp.dot(x, w_ref[...].astype(BF16), preferred_element_type=F32) for w_ref in w_refs]

    def epilogue(vals):
        a = vals[0]
        if kind == "swiglu":
            a = jax.nn.silu(a) * vals[1]
        if has_bias:
            a = a + bias_ref[...]
        if kind == "res":
            a = res_ref[...] + mod_ref[0, gate_idx:gate_idx + 1, :] * a
        o_ref[...] = a.astype(o_ref.dtype)

    if nk == 1:
        epilogue(prods)
        return

    k = pl.program_id(2)

    @pl.when(k == 0)
    def _():
        for acc, p in zip(acc_refs, prods):
            acc[...] = p

    @pl.when(k > 0)
    def _():
        for acc, p in zip(acc_refs, prods):
            acc[...] += p

    @pl.when(k == nk - 1)
    def _():
        epilogue([acc[...] for acc in acc_refs])


def matmul(x, w, *, w3=None, bias=None, res=None, mods=None, gate_idx=0, out_dtype=F32, silu_x=False,
           tm=1024, tn=512, tk=None, group=None):
    m, kdim = x.shape
    n = w.shape[1]
    tm = min(tm, m)
    tk = kdim if tk is None else tk
    nk = kdim // tk
    kind = "swiglu" if w3 is not None else ("res" if res is not None else "plain")
    in_specs = [pl.BlockSpec((tm, tk), lambda i, j, k: (i, k)),
                pl.BlockSpec((tk, tn), lambda i, j, k: (k, j))]
    args = [x, w]
    if w3 is not None:
        in_specs.append(pl.BlockSpec((tk, tn), lambda i, j, k: (k, j)))
        args.append(w3)
    if bias is not None:
        in_specs.append(pl.BlockSpec((1, tn), lambda i, j, k: (0, j)))
        args.append(bias.reshape(1, n))
    if res is not None:
        in_specs.append(pl.BlockSpec((tm, tn), lambda i, j, k: (i, j)))
        in_specs.append(pl.BlockSpec((1, 6, tn), lambda i, j, k: (group(i), 0, j)))
        args += [res, mods]
    n_acc = 0 if nk == 1 else (2 if w3 is not None else 1)
    return pl.pallas_call(
        functools.partial(_mm_kernel, kind=kind, nk=nk, silu_x=silu_x, has_bias=bias is not None,
                          gate_idx=gate_idx),
        grid=(m // tm, n // tn, nk),
        in_specs=in_specs,
        out_specs=pl.BlockSpec((tm, tn), lambda i, j, k: (i, j)),
        out_shape=jax.ShapeDtypeStruct((m, n), out_dtype),
        scratch_shapes=[pltpu.VMEM((tm, tn), F32)] * n_acc,
        compiler_params=_cparams(("parallel", "parallel", "arbitrary")),
        name="matmul_" + kind,
    )(*args)


def _sgu_kernel(u_ref, v_ref, ng_ref, ws_ref, bs_ref, o_ref, *, chunks):
    for c in range(chunks):
        rows = slice(c * SGU_CHUNK, (c + 1) * SGU_CHUNK)
        for g in range(A_GROUPS):
            cols = slice(g * LANES, (g + 1) * LANES)
            vg = jax.nn.gelu(v_ref[rows, cols].astype(F32))
            ms = jnp.mean(vg * vg, axis=-1, keepdims=True)
            vn = vg * lax.rsqrt(ms + EPS) * ng_ref[:, cols]
            s = jnp.dot(ws_ref[g].astype(BF16), vn.astype(BF16), preferred_element_type=F32)
            u = jax.nn.gelu(u_ref[rows, cols].astype(F32))
            o_ref[rows, cols] = (u * (s + bs_ref[:, cols])).astype(o_ref.dtype)


def chunk_sgu(hp, norm_g, ws, bs, *, tm=256):
    m = hp.shape[0]
    bs_exp = jnp.repeat(bs.T, LANES, axis=1)
    return pl.pallas_call(
        functools.partial(_sgu_kernel, chunks=tm // SGU_CHUNK),
        grid=(m // tm,),
        in_specs=[
            pl.BlockSpec((tm, MIX_A), lambda i: (i, 0)),
            pl.BlockSpec((tm, MIX_A), lambda i: (i, 1)),
            pl.BlockSpec((1, MIX_A), lambda i: (0, 0)),
            pl.BlockSpec((A_GROUPS, SGU_CHUNK, SGU_CHUNK), lambda i: (0, 0, 0)),
            pl.BlockSpec((SGU_CHUNK, MIX_A), lambda i: (0, 0)),
        ],
        out_specs=pl.BlockSpec((tm, MIX_A), lambda i: (i, 0)),
        out_shape=jax.ShapeDtypeStruct((m, MIX_A), BF16),
        compiler_params=_cparams(("parallel",)),
        name="chunk_sgu",
    )(hp, hp, norm_g.reshape(1, MIX_A), ws, bs_exp)


def _shift_kernel(x_ref, p_ref, n_ref, w_ref, o_ref, *, ts, lat_tiles, tiles_per_lat, tiles_per_ctx):
    i = pl.program_id(0)
    is_lat = i < lat_tiles
    pos = jnp.where(is_lat, i % tiles_per_lat, (i - lat_tiles) % tiles_per_ctx)
    last = jnp.where(is_lat, tiles_per_lat - 1, tiles_per_ctx - 1)
    x = x_ref[...]
    prev_row = jnp.where(pos == 0, 0.0, p_ref[7:8, :])
    next_row = jnp.where(pos == last, 0.0, n_ref[0:1, :])
    rows = lax.broadcasted_iota(jnp.int32, x.shape, 0)
    xm = jnp.where(rows == 0, prev_row, pltpu.roll(x, 1, axis=0))
    xp = jnp.where(rows == ts - 1, next_row, pltpu.roll(x, ts - 1, axis=0))
    o_ref[...] = w_ref[0:1, :] * xm + w_ref[1:2, :] * x + w_ref[2:3, :] * xp


def token_shift(z, shift_w, *, n_lat, seq, ctx_len, ts=256, tc=Z_WIDTH):
    m = z.shape[0]
    halo = 8
    nblk8 = m // halo
    w_pad = jnp.pad(shift_w, ((0, 0), (0, Z_WIDTH - B_WIDTH)))
    kern = functools.partial(_shift_kernel, ts=ts, lat_tiles=n_lat // ts, tiles_per_lat=seq // ts,
                             tiles_per_ctx=ctx_len // ts)
    return pl.pallas_call(
        kern,
        grid=(m // ts, Z_WIDTH // tc),
        in_specs=[
            pl.BlockSpec((ts, tc), lambda i, j: (i, j)),
            pl.BlockSpec((halo, tc), lambda i, j: (jnp.maximum(i * (ts // halo) - 1, 0), j)),
            pl.BlockSpec((halo, tc), lambda i, j: (jnp.minimum((i + 1) * (ts // halo), nblk8 - 1), j)),
            pl.BlockSpec((3, tc), lambda i, j: (0, j)),
        ],
        out_specs=pl.BlockSpec((ts, tc), lambda i, j: (i, j)),
        out_shape=jax.ShapeDtypeStruct((m, Z_WIDTH), F32),
        compiler_params=_cparams(("parallel", "parallel")),
        name="token_shift",
    )(z, z, z, w_pad)


def _mask_lanes(x, m0):
    z = jnp.zeros_like(x)
    return jnp.concatenate([jnp.where(m0, x, z), jnp.where(m0, z, x)], axis=0)


def _dot(a, b):
    return jnp.dot(a.astype(BF16), b.astype(BF16), preferred_element_type=F32)


def _dot_nt(a, b):
    return lax.dot_general(a.astype(BF16), b.astype(BF16), (((1,), (1,)), ((), ())),
                           preferred_element_type=F32)


def _dot_tn(a, b):
    return lax.dot_general(a.astype(BF16), b.astype(BF16), (((0,), (0,)), ((), ())),
                           preferred_element_type=F32)


def _head_sums(x, m0):
    s0 = jnp.sum(jnp.where(m0, x, 0.0), axis=-1, keepdims=True)
    s1 = jnp.sum(jnp.where(m0, 0.0, x), axis=-1, keepdims=True)
    return jnp.where(m0, s0, s1)


def _rwkv_kernel(*refs, reverse, n_chunks):
    if reverse:
        (zr_ref, zk_ref, zv_ref, zl_ref, w0_ref, wt_ref, a0_ref, wa_ref, kk_ref, ka_ref, rk_ref,
         y0_ref, g_ref, bon0_ref, lnw_ref, lnb_ref, out_ref,
         s_ref, kk_s, bt_s, kd_s, rt_s, v_s, y_s, ge_s) = refs
    else:
        (zr_ref, zk_ref, zv_ref, zl_ref, w0_ref, wt_ref, a0_ref, wa_ref, kk_ref, ka_ref, rk_ref, wg_ref,
         y_out_ref, g_out_ref, bon_out_ref,
         s_ref, kk_s, bt_s, kd_s, rt_s, v_s, y_s, ge_s) = refs
    ch = RW_CHUNK
    j = pl.program_id(1)

    @pl.when(j == 0)
    def _():
        s_ref[...] = jnp.zeros_like(s_ref)

    zl = zl_ref[...]
    lora_in = zl[:, 0:LANES]
    wl = w0_ref[...] + _dot(jnp.tanh(lora_in), wt_ref[...])
    wlog = -(jnp.maximum(-wl, 0.0) + jnp.log(1.0 + jnp.exp(-jnp.abs(wl)))) - 0.5
    lw = -jnp.exp(wlog)
    a = jax.nn.sigmoid(a0_ref[...] + _dot(lora_in, wa_ref[...]))
    r = zr_ref[...]
    k = zk_ref[...]
    v = zv_ref[...]
    kd = k * (1.0 + (a - 1.0) * ka_ref[...])
    kraw = k * kk_ref[...]
    rkd = r * kd * rk_ref[...]
    if not reverse:
        g_out_ref[...] = _dot(jax.nn.sigmoid(zl[:, LANES:3 * LANES]), wg_ref[...])

    lane = lax.broadcasted_iota(jnp.int32, (1, LANES), 1)
    m0 = lane < B_HEAD
    ti = lax.broadcasted_iota(jnp.int32, (ch, ch), 0)
    si = lax.broadcasted_iota(jnp.int32, (ch, ch), 1)
    tri = ((si >= ti) if reverse else (si <= ti)).astype(BF16)

    kn_l, b_l = [], []
    for p in range(RW_PAIRS):
        cols = slice(p * LANES, (p + 1) * LANES)
        kp = kraw[:, cols]
        nrm = jnp.sqrt(_head_sums(kp * kp, m0))
        kn = kp / jnp.maximum(nrm, 1e-12)
        kn_l.append(kn)
        b_l.append(kn * a[:, cols])
        v_s[p] = v[:, cols]
        if not reverse:
            bon_out_ref[:, cols] = _head_sums(rkd[:, cols], m0) * v[:, cols]

    for c in range(n_chunks):
        rws = slice(c * ch, (c + 1) * ch)
        lw_c = lw[rws, :]
        hi = lw_c.astype(BF16)
        rem = lw_c - hi.astype(F32)
        mid = rem.astype(BF16)
        lo = (rem - mid.astype(F32)).astype(BF16)
        cs = (jnp.dot(tri, hi, preferred_element_type=F32) + jnp.dot(tri, mid, preferred_element_type=F32)
              + jnp.dot(tri, lo, preferred_element_type=F32))
        total = cs[0:1, :] if reverse else cs[ch - 1:ch, :]
        g_in = jnp.exp(cs)
        g_ex = jnp.exp(cs - lw_c)
        g_inv = jnp.exp(-cs)
        g_end = jnp.exp(total)
        for p in range(RW_PAIRS):
            cols = slice(p * LANES, (p + 1) * LANES)
            kk_s[p, rws, :] = kn_l[p][rws, :] * g_ex[:, cols]
            bt_s[p, rws, :] = b_l[p][rws, :] * g_inv[:, cols]
            kd_s[p, rws, :] = kd[rws, cols] * g_inv[:, cols]
            rt_s[p, rws, :] = r[rws, cols] * g_in[:, cols]
            ge_s[p, c:c + 1, :] = g_end[:, cols]

    ri = lax.broadcasted_iota(jnp.int32, (2 * ch, 2 * ch), 0)
    ci = lax.broadcasted_iota(jnp.int32, (2 * ch, 2 * ch), 1)
    same_head = (ri // ch) == (ci // ch)
    if reverse:
        strict = same_head & (ci > ri)
        incl = same_head & (ci >= ri)
    else:
        strict = same_head & (ci < ri)
        incl = same_head & (ci <= ri)
    eye = (ri == ci).astype(F32)
    zero = jnp.zeros((2 * ch, 2 * ch), F32)
    n_sq = int(math.log2(INV_BLOCK)) - 1
    blk_diag = []
    bs = INV_BLOCK
    while bs <= ch:
        blk_diag.append((ri // bs) == (ci // bs))
        bs *= 2
    pairs = range(RW_PAIRS)

    def chunk_body(ci_, carry):
        c = (n_chunks - 1 - ci_) if reverse else ci_
        rows = pl.ds(pl.multiple_of(c * ch, ch), ch)
        kk_t = [kk_s[p, rows, :] for p in pairs]
        rt = [rt_s[p, rows, :] for p in pairs]
        bt_m = [_mask_lanes(bt_s[p, rows, :], m0) for p in pairs]
        kd_m = [_mask_lanes(kd_s[p, rows, :], m0) for p in pairs]
        v_st = [_mask_lanes(v_s[p, rows, :], m0) for p in pairs]
        g_end = [ge_s[p, pl.ds(c, 1), :] for p in pairs]

        pq = [_dot_nt(jnp.concatenate([kk_t[p], kk_t[p], rt[p], rt[p]], axis=0),
                      jnp.concatenate([bt_m[p], kd_m[p]], axis=0)) for p in pairs]
        a_m = [jnp.where(strict, pq[p][0:2 * ch, 0:2 * ch], zero) for p in pairs]
        b_m = [jnp.where(strict, pq[p][0:2 * ch, 2 * ch:4 * ch], zero) for p in pairs]
        m2 = [jnp.where(incl, pq[p][2 * ch:4 * ch, 0:2 * ch], zero) for p in pairs]
        m1 = [jnp.where(incl, pq[p][2 * ch:4 * ch, 2 * ch:4 * ch], zero) for p in pairs]

        q = [jnp.where(blk_diag[0], -a_m[p], zero) for p in pairs]
        t_m = [eye + q[p] for p in pairs]
        q = [_dot(q[p], q[p]) for p in pairs]
        for _ in range(n_sq - 1):
            both = [_dot(jnp.concatenate([q[p], t_m[p]], axis=0), q[p]) for p in pairs]
            q = [both[p][0:2 * ch] for p in pairs]
            t_m = [t_m[p] + both[p][2 * ch:4 * ch] for p in pairs]
        t_m = [t_m[p] + _dot(t_m[p], q[p]) for p in pairs]
        for lvl in range(1, len(blk_diag)):
            off = [jnp.where(blk_diag[lvl] & ~blk_diag[lvl - 1], a_m[p], zero) for p in pairs]
            t_off = [_dot(t_m[p], off[p]) for p in pairs]
            t_m = [t_m[p] - _dot(t_off[p], t_m[p]) for p in pairs]

        bv = [_dot(b_m[p], v_st[p]) for p in pairs]
        ku = [_dot(t_m[p], jnp.concatenate([_mask_lanes(kk_t[p], m0), bv[p]], axis=1)) for p in pairs]
        m2ku = [_dot(m2[p], ku[p]) for p in pairs]
        m1v = [_dot(m1[p], v_st[p]) for p in pairs]
        s_bd = [s_ref[p] for p in pairs]
        ys = []
        for p in pairs:
            rp_s = _mask_lanes(rt[p], m0) - m2ku[p][:, 0:LANES]
            y0_s = m1v[p] - m2ku[p][:, LANES:2 * LANES]
            rp = rp_s[0:ch] + rp_s[ch:2 * ch]
            y0 = y0_s[0:ch] + y0_s[ch:2 * ch]
            ys.append(_dot_nt(rp, s_bd[p]) + y0)
        u_all = [_dot_nt(ku[p][:, 0:LANES], s_bd[p]) + ku[p][:, LANES:2 * LANES] for p in pairs]
        for p in pairs:
            upd = _dot_tn(jnp.concatenate([v_st[p], -u_all[p]], axis=0),
                          jnp.concatenate([kd_m[p] * g_end[p], bt_m[p] * g_end[p]], axis=0))
            s_ref[p] = s_bd[p] * g_end[p] + upd
            y_s[p, rows, :] = ys[p]
        return carry

    lax.fori_loop(0, n_chunks, chunk_body, 0)

    for p in range(RW_PAIRS):
        cols = slice(p * LANES, (p + 1) * LANES)
        if not reverse:
            y_out_ref[:, cols] = y_s[p]
        else:
            bon1 = _head_sums(rkd[:, cols], m0) * v[:, cols]
            ysum = y0_ref[:, cols] + y_s[p]
            mu = _head_sums(ysum, m0) * (1.0 / B_HEAD)
            dlt = ysum - mu
            var = _head_sums(dlt * dlt, m0) * (1.0 / B_HEAD)
            yn = dlt * lax.rsqrt(var + GN_EPS)
            o = (yn * lnw_ref[:, cols] + lnb_ref[:, cols] + bon0_ref[:, cols] + bon1) * g_ref[:, cols]
            out_ref[:, cols] = o.astype(out_ref.dtype)


def rwkv7_bidir(zs, w0, w_up, a0, a_up, g_up, k_k, k_a, r_k, lnx_w, lnx_b, *, batch, seq, ctx_len):
    m = zs.shape[0]
    n_lat = batch * seq
    tb = RW_TILE
    nct, nlt = ctx_len // tb, seq // tb
    steps = nct + nlt

    def row_block(reverse):
        def f(b, j):
            jc = (nct - 1 - j) if reverse else j
            jl = (nlt - 1 - (j - nct)) if reverse else (j - nct)
            return jnp.where(j < nct, (n_lat + b * ctx_len) // tb + jc, b * nlt + jl)
        return f

    def pad_rows(w, top, total):
        return jnp.pad(w, ((top, total - top - w.shape[0]), (0, 0)))

    wg = pad_rows(g_up, 0, 2 * LANES)
    rk = r_k.reshape(1, MIX_B)
    vec = lambda a: a.reshape(1, MIX_B)
    scratch = ([pltpu.VMEM((RW_PAIRS, LANES, LANES), F32)] + [pltpu.VMEM((RW_PAIRS, tb, LANES), F32)] * 6
               + [pltpu.VMEM((RW_PAIRS, 8, LANES), F32)])
    const = lambda shape: pl.BlockSpec(shape, lambda b, j: (0,) * len(shape))

    def call(reverse, extra_in, extra_specs, out_shapes, out_specs, d):
        rb = row_block(reverse)
        wt = pad_rows(w_up[d], 0, LANES)
        wa = pad_rows(a_up[d], DECAY_LORA, LANES)
        in_specs = [
            pl.BlockSpec((tb, MIX_B), lambda b, j: (rb(b, j), 0)),
            pl.BlockSpec((tb, MIX_B), lambda b, j: (rb(b, j), 1)),
            pl.BlockSpec((tb, MIX_B), lambda b, j: (rb(b, j), 2)),
            pl.BlockSpec((tb, 512), lambda b, j: (rb(b, j), 6)),
            const((1, MIX_B)), const((LANES, MIX_B)), const((1, MIX_B)), const((LANES, MIX_B)),
            const((1, MIX_B)), const((1, MIX_B)), const((1, MIX_B)),
        ] + extra_specs
        args = [zs, zs, zs, zs, vec(w0[d]), wt, vec(a0[d]), wa, vec(k_k), vec(k_a), rk] + extra_in
        return pl.pallas_call(
            functools.partial(_rwkv_kernel, reverse=reverse, n_chunks=tb // RW_CHUNK),
            grid=(batch, steps),
            in_specs=in_specs,
            out_specs=out_specs,
            out_shape=out_shapes,
            scratch_shapes=scratch,
            compiler_params=_cparams(("parallel", "arbitrary")),
            name="rwkv7_rev" if reverse else "rwkv7_fwd",
        )(*args)

    rbf = row_block(False)
    tile_f = pl.BlockSpec((tb, MIX_B), lambda b, j: (rbf(b, j), 0))
    y0, g, bon0 = call(False, [wg], [const((2 * LANES, MIX_B))],
                       [jax.ShapeDtypeStruct((m, MIX_B), F32)] * 3, [tile_f] * 3, 0)
    rbr = row_block(True)
    tile_r = pl.BlockSpec((tb, MIX_B), lambda b, j: (rbr(b, j), 0))
    out = call(True, [y0, g, bon0, vec(lnx_w), vec(lnx_b)],
               [tile_r, tile_r, tile_r, const((1, MIX_B)), const((1, MIX_B))],
               jax.ShapeDtypeStruct((m, MIX_B), BF16), tile_r, 1)
    return out


def _half_rms(x, gain, m0):
    ms = _head_sums(x * x, m0) * (1.0 / C_HEAD)
    return x * lax.rsqrt(ms + EPS) * gain


def _rope(x, cos, sin):
    lane = lax.broadcasted_iota(jnp.int32, x.shape, 1)
    first = (lane % ROPE_AXIS) < (ROPE_AXIS // 2)
    half = ROPE_AXIS // 2
    rot = jnp.where(first, -pltpu.roll(x, LANES - half, axis=1), pltpu.roll(x, half, axis=1))
    return x * cos + rot * sin


def _attn_kernel(q_ref, kl_ref, vl_ref, kc_ref, vc_ref, cq_ref, sq_ref, ck_ref, sk_ref, qg_ref, kg_ref,
                 lam_ref, sg_ref, o_ref, k_s, v_s, *, ctx_len, lam_init, q_blk):
    qi = pl.program_id(2)
    lane = lax.broadcasted_iota(jnp.int32, (1, LANES), 1)
    m0 = lane < C_HEAD

    @pl.when(qi == 0)
    def _():
        kc = _half_rms(kc_ref[...].astype(F32), kg_ref[...], m0)
        kl = _rope(_half_rms(kl_ref[...].astype(F32), kg_ref[...], m0), ck_ref[...], sk_ref[...])
        k_s[0:ctx_len, :] = kc.astype(BF16)
        k_s[ctx_len:, :] = kl.astype(BF16)
        v_s[0:ctx_len, :] = vc_ref[...].astype(BF16)
        v_s[ctx_len:, :] = vl_ref[...].astype(BF16)

    lp = lam_ref[...]
    lam = (jnp.exp(jnp.sum(lp[0:1] * lp[1:2], keepdims=True)) - jnp.exp(jnp.sum(lp[2:3] * lp[3:4], keepdims=True))
           + lam_init)
    q = (_rope(_half_rms(q_ref[...].astype(F32), qg_ref[...], m0), cq_ref[...], sq_ref[...])
         * (C_HEAD ** -0.5 * LOG2E))
    keys = k_s[...]
    vals = v_s[...]
    zq = jnp.zeros_like(q)
    q_sub = [jnp.where(m0, q, zq), jnp.where(m0, zq, q)]
    blocks = [(slice(r0, r0 + q_blk), i) for r0 in range(0, q.shape[0], q_blk) for i in (0, 1)]
    scores = [_dot_nt(q_sub[i][rows], keys) for rows, i in blocks]
    outs = []
    for s, (rows, i) in zip(scores, blocks):
        e = jnp.exp2(s - jnp.max(s, axis=-1, keepdims=True))
        w = (lam if i else 1.0) / jnp.sum(e, axis=-1, keepdims=True)
        outs.append(jnp.dot(e.astype(BF16), vals, preferred_element_type=F32) * w)
    for n, r0 in enumerate(range(0, q.shape[0], q_blk)):
        o = outs[2 * n] - outs[2 * n + 1]
        ms = jnp.mean(o * o, axis=-1, keepdims=True)
        o = o * lax.rsqrt(ms + SUBLN_EPS) * sg_ref[...] * (1.0 - lam_init)
        o_ref[r0:r0 + q_blk, :] = o.astype(o_ref.dtype)


def _rope_tables(n):
    rows = n // GRID_W
    row = jnp.broadcast_to(jnp.arange(rows, dtype=F32)[:, None], (rows, GRID_W)).reshape(-1)
    col = jnp.broadcast_to(jnp.arange(GRID_W, dtype=F32)[None, :], (rows, GRID_W)).reshape(-1)
    inv = ROPE_THETA ** (-jnp.arange(0, ROPE_AXIS, 2, dtype=F32) / ROPE_AXIS)
    ar = row[:, None] * inv
    ac = col[:, None] * inv
    ang = jnp.concatenate([ar, ar, ac, ac, ar, ar, ac, ac], axis=-1)
    return jnp.cos(ang), jnp.sin(ang)


def diff_attention(qkv, q_g, k_g, lam_params, subln_g, lam_init, *, batch, seq, ctx_len, tq=512):
    n_lat = batch * seq
    tq = min(tq, seq)
    cos, sin = _rope_tables(seq)
    nq = seq // tq
    hq, hk, hv = 0, D_MODEL // LANES, 2 * D_MODEL // LANES
    ctx_blk0 = n_lat // ctx_len
    two = lambda a: jnp.concatenate([a, a]).reshape(1, LANES)
    const = lambda shape: pl.BlockSpec(shape, lambda b, h, i: (0,) * len(shape))
    return pl.pallas_call(
        functools.partial(_attn_kernel, ctx_len=ctx_len, lam_init=lam_init, q_blk=min(256, tq)),
        grid=(batch, C_HEADS, nq),
        in_specs=[
            pl.BlockSpec((tq, LANES), lambda b, h, i: (b * nq + i, hq + h)),
            pl.BlockSpec((seq, LANES), lambda b, h, i: (b, hk + h)),
            pl.BlockSpec((seq, LANES), lambda b, h, i: (b, hv + h)),
            pl.BlockSpec((ctx_len, LANES), lambda b, h, i: (ctx_blk0 + b, hk + h)),
            pl.BlockSpec((ctx_len, LANES), lambda b, h, i: (ctx_blk0 + b, hv + h)),
            pl.BlockSpec((tq, LANES), lambda b, h, i: (i, 0)),
            pl.BlockSpec((tq, LANES), lambda b, h, i: (i, 0)),
            const((seq, LANES)), const((seq, LANES)),
            const((1, LANES)), const((1, LANES)), const((4, C_HEAD)), const((1, LANES)),
        ],
        out_specs=pl.BlockSpec((tq, LANES), lambda b, h, i: (b * nq + i, h)),
        out_shape=jax.ShapeDtypeStruct((n_lat, D_MODEL), BF16),
        scratch_shapes=[pltpu.VMEM((ctx_len + seq, LANES), BF16)] * 2,
        compiler_params=_cparams(("parallel", "parallel", "arbitrary")),
        name="diff_attention",
    )(qkv, qkv, qkv, qkv, qkv, cos, sin, cos, sin, two(q_g), two(k_g), lam_params, subln_g.reshape(1, LANES))


def _route_kernel(x_ref, g_ref, mod_ref, rw_ref, xn_ref, idx_ref, gate_ref, *, s_idx, c_idx):
    x = x_ref[...]
    ms = jnp.mean(x * x, axis=-1, keepdims=True)
    y = x * lax.rsqrt(ms + EPS) * g_ref[...]
    h = y * (1.0 + mod_ref[0, c_idx:c_idx + 1, :]) + mod_ref[0, s_idx:s_idx + 1, :]
    xn_ref[...] = h
    logits = jnp.dot(h, rw_ref[...], precision=lax.Precision.HIGHEST, preferred_element_type=F32)
    lane = lax.broadcasted_iota(jnp.int32, logits.shape, 1)
    neg = jnp.float32(-jnp.inf)
    lg = jnp.where(lane < N_EXPERTS, logits, neg)
    m1 = jnp.max(lg, axis=-1, keepdims=True)
    i1 = jnp.min(jnp.where(lg == m1, lane, LANES), axis=-1, keepdims=True)
    lg2 = jnp.where(lane == i1, neg, lg)
    m2 = jnp.max(lg2, axis=-1, keepdims=True)
    i2 = jnp.min(jnp.where(lg2 == m2, lane, LANES), axis=-1, keepdims=True)
    e2 = jnp.exp(m2 - m1)
    g1 = 1.0 / (1.0 + e2)
    g2 = e2 * g1
    idx_ref[...] = jnp.where(lane == 0, i1, jnp.where(lane == 1, i2, 0))
    gate_ref[...] = jnp.where(lane == 0, g1, jnp.where(lane == 1, g2, 0.0))


def route(x, gain, mods, router, s_idx, c_idx, *, seq, batch, tm=256):
    m, d = x.shape
    rw = jnp.pad(router, ((0, 0), (0, LANES - N_EXPERTS)))
    return pl.pallas_call(
        functools.partial(_route_kernel, s_idx=s_idx, c_idx=c_idx),
        grid=(m // tm,),
        in_specs=[
            pl.BlockSpec((tm, d), lambda i: (i, 0)),
            pl.BlockSpec((1, d), lambda i: (0, 0)),
            pl.BlockSpec((1, 6, d), lambda i: ((i * tm) // seq, 0, 0)),
            pl.BlockSpec((d, LANES), lambda i: (0, 0)),
        ],
        out_specs=[pl.BlockSpec((tm, d), lambda i: (i, 0)),
                   pl.BlockSpec((tm, LANES), lambda i: (i, 0)),
                   pl.BlockSpec((tm, LANES), lambda i: (i, 0))],
        out_shape=[jax.ShapeDtypeStruct((m, d), F32),
                   jax.ShapeDtypeStruct((m, LANES), jnp.int32),
                   jax.ShapeDtypeStruct((m, LANES), F32)],
        compiler_params=_cparams(("parallel",)),
        name="moe_route",
    )(x, gain.reshape(1, d), mods, rw)


def _row_copy(src_hbm, dst_ref, src_row, dst_row, sem):
    return pltpu.make_async_copy(src_hbm.at[pl.ds(src_row, 1)], dst_ref.at[pl.ds(dst_row, 1)], sem)


def _gather_kernel(used_ref, idx_ref, src_hbm, o_ref, buf, sem, *, tg):
    i = pl.program_id(0)

    @pl.when(i * tg >= used_ref[0])
    def _():
        o_ref[...] = jnp.zeros_like(o_ref)

    @pl.when(i * tg < used_ref[0])
    def _():
        def issue(g, c):
            for u in range(DMA_UNROLL):
                r = g * DMA_UNROLL + u
                _row_copy(src_hbm, buf, idx_ref[0, 0, r], r, sem).start(priority=u % 2)
            return c

        def drain(r, c):
            _row_copy(src_hbm, buf, 0, r, sem).wait()
            return c

        lax.fori_loop(0, tg // DMA_UNROLL, issue, 0)
        lax.fori_loop(0, tg, drain, 0, unroll=DMA_UNROLL)
        o_ref[...] = buf[...].astype(o_ref.dtype)


def gather_rows(src, idx, rows_used, *, tg=256):
    mp = idx.shape[0]
    d = src.shape[1]
    return pl.pallas_call(
        functools.partial(_gather_kernel, tg=tg),
        grid_spec=pltpu.PrefetchScalarGridSpec(
            num_scalar_prefetch=1,
            grid=(mp // tg,),
            in_specs=[pl.BlockSpec((1, 1, tg), lambda i, u: (i, 0, 0), memory_space=pltpu.SMEM),
                      pl.BlockSpec(memory_space=pl.ANY)],
            out_specs=pl.BlockSpec((tg, d), lambda i, u: (i, 0)),
            scratch_shapes=[pltpu.VMEM((tg, d), src.dtype), pltpu.SemaphoreType.DMA(())],
        ),
        out_shape=jax.ShapeDtypeStruct((mp, d), BF16),
        compiler_params=_cparams(("arbitrary",)),
        name="moe_gather",
    )(rows_used, idx.reshape(mp // tg, 1, tg), src)


def _gmm_kernel(te_ref, nu_ref, *refs, kind, kc):
    if kind == "swiglu":
        x_ref, w1_ref, w3_ref, o_ref = refs
        w_refs = (w1_ref, w3_ref)
    else:
        x_ref, w1_ref, o_ref = refs
        w_refs = (w1_ref,)
    i = pl.program_id(1)

    @pl.when(i >= nu_ref[0])
    def _():
        o_ref[...] = jnp.zeros_like(o_ref)

    @pl.when(i < nu_ref[0])
    def _():
        kdim = x_ref.shape[1]
        accs = [None] * len(w_refs)
        for k0 in range(0, kdim, kc):
            x = x_ref[:, k0:k0 + kc].astype(BF16)
            for n, w_ref in enumerate(w_refs):
                p = jnp.dot(x, w_ref[0, k0:k0 + kc, :].astype(BF16), preferred_element_type=F32)
                accs[n] = p if accs[n] is None else accs[n] + p
        a = accs[0]
        if kind == "swiglu":
            a = jax.nn.silu(a) * accs[1]
        o_ref[...] = a.astype(o_ref.dtype)


def grouped_matmul(x, w, tile_expert, n_used, *, w3=None, out_dtype=F32, tm=512, tn=512, kc=1024):
    mp, kdim = x.shape
    n = w.shape[2]
    kind = "swiglu" if w3 is not None else "plain"
    w_spec = pl.BlockSpec((1, kdim, tn), lambda j, i, te, nu: (te[i], 0, j))
    in_specs = [pl.BlockSpec((tm, kdim), lambda j, i, te, nu: (i, 0)), w_spec]
    args = [x, w]
    if w3 is not None:
        in_specs.append(w_spec)
        args.append(w3)
    return pl.pallas_call(
        functools.partial(_gmm_kernel, kind=kind, kc=min(kc, kdim)),
        grid_spec=pltpu.PrefetchScalarGridSpec(
            num_scalar_prefetch=2,
            grid=(n // tn, mp // tm),
            in_specs=in_specs,
            out_specs=pl.BlockSpec((tm, tn), lambda j, i, te, nu: (i, j)),
        ),
        out_shape=jax.ShapeDtypeStruct((mp, n), out_dtype),
        compiler_params=pltpu.CompilerParams(dimension_semantics=("arbitrary", "arbitrary"),
                                             vmem_limit_bytes=GMM_VMEM_LIMIT),
        name="moe_gmm_" + kind,
    )(tile_expert, n_used, *args)


def _combine_kernel(pos_ref, ys_hbm, x_ref, gate_ref, mod_ref, o_ref, buf, sem, *, tc, gate_idx):
    def issue(g, c):
        for u in range(DMA_UNROLL):
            r = g * DMA_UNROLL + u
            _row_copy(ys_hbm, buf.at[0], pos_ref[0, 0, 2 * r], r, sem).start(priority=0)
            _row_copy(ys_hbm, buf.at[1], pos_ref[0, 0, 2 * r + 1], r, sem).start(priority=1)
        return c

    def drain(r, c):
        _row_copy(ys_hbm, buf.at[0], 0, r, sem).wait()
        _row_copy(ys_hbm, buf.at[1], 0, r, sem).wait()
        return c

    lax.fori_loop(0, tc // DMA_UNROLL, issue, 0)
    lax.fori_loop(0, tc, drain, 0, unroll=DMA_UNROLL)
    g = gate_ref[...]
    moe = g[:, 0:1] * buf[0] + g[:, 1:2] * buf[1]
    o_ref[...] = x_ref[...] + mod_ref[0, gate_idx:gate_idx + 1, :] * moe


def moe_combine(ys, pos, x, gates, mods, gate_idx, *, seq, tc=256):
    m, d = x.shape
    return pl.pallas_call(
        functools.partial(_combine_kernel, tc=tc, gate_idx=gate_idx),
        grid=(m // tc,),
        in_specs=[pl.BlockSpec((1, 1, 2 * tc), lambda i: (i, 0, 0), memory_space=pltpu.SMEM),
                  pl.BlockSpec(memory_space=pl.ANY),
                  pl.BlockSpec((tc, d), lambda i: (i, 0)),
                  pl.BlockSpec((tc, LANES), lambda i: (i, 0)),
                  pl.BlockSpec((1, 6, d), lambda i: ((i * tc) // seq, 0, 0))],
        out_specs=pl.BlockSpec((tc, d), lambda i: (i, 0)),
        out_shape=jax.ShapeDtypeStruct((m, d), F32),
        scratch_shapes=[pltpu.VMEM((2, tc, d), F32), pltpu.SemaphoreType.DMA(())],
        compiler_params=_cparams(("arbitrary",)),
        name="moe_combine",
    )(pos.reshape(m // tc, 1, 2 * tc), ys, x, gates, mods)


def moe_layer(x, gain, mods, router, w1, w3, w2, *, seq, batch, tm=512):
    n = x.shape[0]
    xn, idx, gates = route(x, gain, mods, router, 3, 4, seq=seq, batch=batch)
    e_flat = idx[:, 0:2].reshape(-1)
    onehot = (e_flat[:, None] == jnp.arange(N_EXPERTS)[None, :]).astype(jnp.int32)
    ranks = jnp.cumsum(onehot, axis=0) - onehot
    rank = jnp.sum(ranks * onehot, axis=1)
    counts = jnp.sum(onehot, axis=0)
    padded = ((counts + tm - 1) // tm) * tm
    starts = jnp.cumsum(padded) - padded
    pos = starts[e_flat] + rank
    mp = 2 * n + N_EXPERTS * tm
    token_of_row = jnp.zeros((mp,), jnp.int32).at[pos].set(jnp.arange(2 * n, dtype=jnp.int32) // 2)
    n_tiles = mp // tm
    ends = jnp.cumsum(padded)
    tile_start = jnp.arange(n_tiles, dtype=jnp.int32) * tm
    tile_expert = jnp.minimum(jnp.sum((tile_start[:, None] >= ends[None, :]).astype(jnp.int32), axis=1),
                              N_EXPERTS - 1).astype(jnp.int32)
    n_used = (ends[-1] // tm).astype(jnp.int32).reshape(1)

    xs = gather_rows(xn, token_of_row, ends[-1].astype(jnp.int32).reshape(1))
    hs = grouped_matmul(xs, w1, tile_expert, n_used, w3=w3, out_dtype=BF16, tm=tm, tn=1024)
    ys = grouped_matmul(hs, w2, tile_expert, n_used, out_dtype=F32, tm=tm, tn=512)
    return moe_combine(ys, pos.astype(jnp.int32), x, gates, mods, 5, seq=seq)


def kernel(x, c, ctx, c_ctx, l0_ada_w, l0_ada_b, l0_norm1_g, l0_norm2_g, l0_w_in, l0_sgu_norm_g, l0_sgu_w, l0_sgu_b, l0_shift_w, l0_w0, l0_w_up, l0_a0, l0_a_up, l0_g_up, l0_k_k, l0_k_a, l0_r_k, l0_lnx_w, l0_lnx_b, l0_w_out, l0_ffn_w1, l0_ffn_w3, l0_ffn_w2, l1_ada_w, l1_ada_b, l1_norm1_g, l1_norm2_g, l1_w_qkv, l1_q_norm_g, l1_k_norm_g, l1_lam_q1, l1_lam_k1, l1_lam_q2, l1_lam_k2, l1_subln_g, l1_w_out, l1_router, l1_exp_w1, l1_exp_w3, l1_exp_w2):
    batch, seq, d = x.shape
    ctx_len = ctx.shape[1]
    n_lat = batch * seq
    tm = min(1024, seq, batch * ctx_len)
    geo = dict(n_lat=n_lat, seq=seq, batch=batch)
    grp = functools.partial(_group_of_tile, tm=tm, **geo)

    xall = jnp.concatenate([x.reshape(n_lat, d), ctx.reshape(batch * ctx_len, d)], axis=0)
    cond = jnp.concatenate([c, c_ctx[None, :], jnp.zeros((16 - batch - 1, d), F32)], axis=0)

    def ada(w, b):
        return matmul(cond, w, bias=b, silu_x=True, tn=1024).reshape(16, 6, d)

    mods = ada(l0_ada_w, l0_ada_b)
    hn = norm_mod(xall, l0_norm1_g, mods, 0, 1, tm=min(512, tm), **geo)
    w_z = jnp.pad(l0_w_in[:, 2 * MIX_A:], ((0, 0), (0, Z_WIDTH - B_WIDTH)))
    hp_a = matmul(hn, l0_w_in[:, :2 * MIX_A], out_dtype=BF16, tm=tm)
    hp_z = matmul(hn, w_z, tm=tm)
    a_out = chunk_sgu(hp_a, l0_sgu_norm_g, l0_sgu_w, l0_sgu_b)
    zs = token_shift(hp_z, l0_shift_w, n_lat=n_lat, seq=seq, ctx_len=ctx_len)
    b_out = rwkv7_bidir(zs, l0_w0, l0_w_up, l0_a0, l0_a_up, l0_g_up, l0_k_k, l0_k_a, l0_r_k, l0_lnx_w, l0_lnx_b,
                        batch=batch, seq=seq, ctx_len=ctx_len)
    mixed = jnp.concatenate([a_out, b_out], axis=1)
    x1 = matmul(mixed, l0_w_out, res=xall, mods=mods, gate_idx=2, group=grp, tm=tm)
    hn = norm_mod(x1, l0_norm2_g, mods, 3, 4, tm=min(512, tm), **geo)
    hff = matmul(hn, l0_ffn_w1, w3=l0_ffn_w3, out_dtype=BF16, tm=tm)
    x2 = matmul(hff, l0_ffn_w2, res=x1, mods=mods, gate_idx=5, group=grp, tm=tm, tn=512,
                tk=l0_ffn_w2.shape[0] // 2)

    mods = ada(l1_ada_w, l1_ada_b)
    hn = norm_mod(x2, l1_norm1_g, mods, 0, 1, tm=min(512, tm), **geo)
    qkv = matmul(hn, l1_w_qkv, out_dtype=BF16, tm=tm)
    lam_params = jnp.stack([l1_lam_q1, l1_lam_k1, l1_lam_q2, l1_lam_k2])
    lam_init = 0.8 - 0.6 * math.exp(-0.3 * 1)
    o = diff_attention(qkv, l1_q_norm_g, l1_k_norm_g, lam_params, l1_subln_g, lam_init,
                       batch=batch, seq=seq, ctx_len=ctx_len)
    x3 = matmul(o, l1_w_out, res=x2, mods=mods, gate_idx=2, group=grp, tm=tm)
    x4 = moe_layer(x3, l1_norm2_g, mods, l1_router, l1_exp_w1, l1_exp_w3, l1_exp_w2, seq=seq, batch=batch)
    return x4.reshape(batch, seq, d)
```

```python
import functools
import math

import jax
import jax.numpy as jnp
from jax import lax
from jax.experimental import pallas as pl
from jax.experimental.pallas import tpu as pltpu

F32 = jnp.float32
BF16 = jnp.bfloat16

D_MODEL = 2048
GRID_W = 64
EPS = 1e-6

MIX_A = 1024
SGU_CHUNK = 128
A_GROUPS = 8
MIX_B = 1024
B_HEAD = 64
B_HEADS = 16
DECAY_LORA = 64
AAA_LORA = 64
GATE_LORA = 160
B_WIDTH = 3 * MIX_B + DECAY_LORA + AAA_LORA + GATE_LORA
GN_EPS = B_HEAD * 1e-5

C_HEADS = 16
C_HEAD = 64
ROPE_AXIS = C_HEAD // 2
ROPE_THETA = 10000.0
SUBLN_EPS = 1e-5

N_EXPERTS = 8
D_EXPERT = 7168

LOG2E = 1.4426950408889634
LANES = 128
VMEM_LIMIT = 52 * 1024 * 1024
GMM_VMEM_LIMIT = 58 * 1024 * 1024

RW_CHUNK = 64
INV_BLOCK = 16
DMA_UNROLL = 8
RW_TILE = 256
RW_PAIRS = B_HEADS // 2
Z_WIDTH = 3584


def _cparams(sem):
    return pltpu.CompilerParams(dimension_semantics=sem, vmem_limit_bytes=VMEM_LIMIT)


def _group_of_tile(i, tm, n_lat, seq, batch):
    return jnp.where(i * tm < n_lat, (i * tm) // seq, batch)


def _norm_mod_kernel(x_ref, g_ref, mod_ref, o_ref, *, s_idx, c_idx):
    x = x_ref[...]
    ms = jnp.mean(x * x, axis=-1, keepdims=True)
    y = x * lax.rsqrt(ms + EPS) * g_ref[...]
    shift = mod_ref[0, s_idx:s_idx + 1, :]
    scale = mod_ref[0, c_idx:c_idx + 1, :]
    o_ref[...] = (y * (1.0 + scale) + shift).astype(o_ref.dtype)


def norm_mod(x, gain, mods, s_idx, c_idx, *, n_lat, seq, batch, tm=512):
    m, d = x.shape
    grp = functools.partial(_group_of_tile, tm=tm, n_lat=n_lat, seq=seq, batch=batch)
    return pl.pallas_call(
        functools.partial(_norm_mod_kernel, s_idx=s_idx, c_idx=c_idx),
        grid=(m // tm,),
        in_specs=[
            pl.BlockSpec((tm, d), lambda i: (i, 0)),
            pl.BlockSpec((1, d), lambda i: (0, 0)),
            pl.BlockSpec((1, 6, d), lambda i: (grp(i), 0, 0)),
        ],
        out_specs=pl.BlockSpec((tm, d), lambda i: (i, 0)),
        out_shape=jax.ShapeDtypeStruct((m, d), BF16),
        compiler_params=_cparams(("parallel",)),
        name="norm_mod",
    )(x, gain.reshape(1, d), mods)


def _mm_kernel(*refs, kind, nk, silu_x, has_bias, gate_idx):
    refs = list(refs)
    x_ref = refs.pop(0)
    w_refs = [refs.pop(0)]
    if kind == "swiglu":
        w_refs.append(refs.pop(0))
    bias_ref = refs.pop(0) if has_bias else None
    res_ref = mod_ref = None
    if kind == "res":
        res_ref = refs.pop(0)
        mod_ref = refs.pop(0)
    o_ref = refs.pop(0)
    acc_refs = refs

    x = x_ref[...]
    if silu_x:
        x = jax.nn.silu(x.astype(F32))
    x = x.astype(BF16)
    prods = [jnp.dot(x, w_ref[...].astype(BF16), preferred_element_type=F32) for w_ref in w_refs]

    def epilogue(vals):
        a = vals[0]
        if kind == "swiglu":
            a = jax.nn.silu(a) * vals[1]
        if has_bias:
            a = a + bias_ref[...]
        if kind == "res":
            a = res_ref[...] + mod_ref[0, gate_idx:gate_idx + 1, :] * a
        o_ref[...] = a.astype(o_ref.dtype)

    if nk == 1:
        epilogue(prods)
        return

    k = pl.program_id(2)

    @pl.when(k == 0)
    def _():
        for acc, p in zip(acc_refs, prods):
            acc[...] = p

    @pl.when(k > 0)
    def _():
        for acc, p in zip(acc_refs, prods):
            acc[...] += p

    @pl.when(k == nk - 1)
    def _():
        epilogue([acc[...] for acc in acc_refs])


def matmul(x, w, *, w3=None, bias=None, res=None, mods=None, gate_idx=0, out_dtype=F32, silu_x=False,
           tm=1024, tn=512, tk=None, geo=None):
    m, kdim = x.shape
    n = w.shape[1]
    tm = min(tm, m)
    tk = kdim if tk is None else tk
    nk = kdim // tk
    kind = "swiglu" if w3 is not None else ("res" if res is not None else "plain")
    in_specs = [pl.BlockSpec((tm, tk), lambda i, j, k: (i, k)),
                pl.BlockSpec((tk, tn), lambda i, j, k: (k, j))]
    args = [x, w]
    if w3 is not None:
        in_specs.append(pl.BlockSpec((tk, tn), lambda i, j, k: (k, j)))
        args.append(w3)
    if bias is not None:
        in_specs.append(pl.BlockSpec((1, tn), lambda i, j, k: (0, j)))
        args.append(bias.reshape(1, n))
    if res is not None:
        in_specs.append(pl.BlockSpec((tm, tn), lambda i, j, k: (i, j)))
        group = functools.partial(_group_of_tile, tm=tm, **geo)
        in_specs.append(pl.BlockSpec((1, 6, tn), lambda i, j, k: (group(i), 0, j)))
        args += [res, mods]
    n_acc = 0 if nk == 1 else (2 if w3 is not None else 1)
    return pl.pallas_call(
        functools.partial(_mm_kernel, kind=kind, nk=nk, silu_x=silu_x, has_bias=bias is not None,
                          gate_idx=gate_idx),
        grid=(m // tm, n // tn, nk),
        in_specs=in_specs,
        out_specs=pl.BlockSpec((tm, tn), lambda i, j, k: (i, j)),
        out_shape=jax.ShapeDtypeStruct((m, n), out_dtype),
        scratch_shapes=[pltpu.VMEM((tm, tn), F32)] * n_acc,
        compiler_params=_cparams(("parallel", "parallel", "arbitrary")),
        name="matmul_" + kind,
    )(*args)


def _sgu_kernel(u_ref, v_ref, ng_ref, ws_ref, bs_ref, o_ref, *, chunks):
    for c in range(chunks):
        rows = slice(c * SGU_CHUNK, (c + 1) * SGU_CHUNK)
        for g in range(A_GROUPS):
            cols = slice(g * LANES, (g + 1) * LANES)
            vg = jax.nn.gelu(v_ref[rows, cols].astype(F32))
            ms = jnp.mean(vg * vg, axis=-1, keepdims=True)
            vn = vg * lax.rsqrt(ms + EPS) * ng_ref[:, cols]
            s = jnp.dot(ws_ref[g].astype(BF16), vn.astype(BF16), preferred_element_type=F32)
            u = jax.nn.gelu(u_ref[rows, cols].astype(F32))
            o_ref[rows, cols] = (u * (s + bs_ref[:, cols])).astype(o_ref.dtype)


def chunk_sgu(hp, norm_g, ws, bs, *, tm=256):
    m = hp.shape[0]
    bs_exp = jnp.repeat(bs.T, LANES, axis=1)
    return pl.pallas_call(
        functools.partial(_sgu_kernel, chunks=tm // SGU_CHUNK),
        grid=(m // tm,),
        in_specs=[
            pl.BlockSpec((tm, MIX_A), lambda i: (i, 0)),
            pl.BlockSpec((tm, MIX_A), lambda i: (i, 1)),
            pl.BlockSpec((1, MIX_A), lambda i: (0, 0)),
            pl.BlockSpec((A_GROUPS, SGU_CHUNK, SGU_CHUNK), lambda i: (0, 0, 0)),
            pl.BlockSpec((SGU_CHUNK, MIX_A), lambda i: (0, 0)),
        ],
        out_specs=pl.BlockSpec((tm, MIX_A), lambda i: (i, 0)),
        out_shape=jax.ShapeDtypeStruct((m, MIX_A), BF16),
        compiler_params=_cparams(("parallel",)),
        name="chunk_sgu",
    )(hp, hp, norm_g.reshape(1, MIX_A), ws, bs_exp)


def _shift_kernel(x_ref, p_ref, n_ref, w_ref, o_ref, *, ts, lat_tiles, tiles_per_lat, tiles_per_ctx):
    i = pl.program_id(0)
    is_lat = i < lat_tiles
    pos = jnp.where(is_lat, i % tiles_per_lat, (i - lat_tiles) % tiles_per_ctx)
    last = jnp.where(is_lat, tiles_per_lat - 1, tiles_per_ctx - 1)
    x = x_ref[...]
    prev_row = jnp.where(pos == 0, 0.0, p_ref[7:8, :])
    next_row = jnp.where(pos == last, 0.0, n_ref[0:1, :])
    rows = lax.broadcasted_iota(jnp.int32, x.shape, 0)
    xm = jnp.where(rows == 0, prev_row, pltpu.roll(x, 1, axis=0))
    xp = jnp.where(rows == ts - 1, next_row, pltpu.roll(x, ts - 1, axis=0))
    o_ref[...] = w_ref[0:1, :] * xm + w_ref[1:2, :] * x + w_ref[2:3, :] * xp


def token_shift(z, shift_w, *, n_lat, seq, ctx_len, ts=256, tc=Z_WIDTH):
    m = z.shape[0]
    halo = 8
    nblk8 = m // halo
    w_pad = jnp.pad(shift_w, ((0, 0), (0, Z_WIDTH - B_WIDTH)))
    kern = functools.partial(_shift_kernel, ts=ts, lat_tiles=n_lat // ts, tiles_per_lat=seq // ts,
                             tiles_per_ctx=ctx_len // ts)
    return pl.pallas_call(
        kern,
        grid=(m // ts, Z_WIDTH // tc),
        in_specs=[
            pl.BlockSpec((ts, tc), lambda i, j: (i, j)),
            pl.BlockSpec((halo, tc), lambda i, j: (jnp.maximum(i * (ts // halo) - 1, 0), j)),
            pl.BlockSpec((halo, tc), lambda i, j: (jnp.minimum((i + 1) * (ts // halo), nblk8 - 1), j)),
            pl.BlockSpec((3, tc), lambda i, j: (0, j)),
        ],
        out_specs=pl.BlockSpec((ts, tc), lambda i, j: (i, j)),
        out_shape=jax.ShapeDtypeStruct((m, Z_WIDTH), F32),
        compiler_params=_cparams(("parallel", "parallel")),
        name="token_shift",
    )(z, z, z, w_pad)


def _mask_lanes(x, m0):
    z = jnp.zeros_like(x)
    return jnp.concatenate([jnp.where(m0, x, z), jnp.where(m0, z, x)], axis=0)


def _dot(a, b):
    return jnp.dot(a.astype(BF16), b.astype(BF16), preferred_element_type=F32)


def _dot_nt(a, b):
    return lax.dot_general(a.astype(BF16), b.astype(BF16), (((1,), (1,)), ((), ())),
                           preferred_element_type=F32)


def _dot_tn(a, b):
    return lax.dot_general(a.astype(BF16), b.astype(BF16), (((0,), (0,)), ((), ())),
                           preferred_element_type=F32)


def _head_sums(x, m0):
    s0 = jnp.sum(jnp.where(m0, x, 0.0), axis=-1, keepdims=True)
    s1 = jnp.sum(jnp.where(m0, 0.0, x), axis=-1, keepdims=True)
    return jnp.where(m0, s0, s1)


def _rwkv_kernel(*refs, reverse, n_chunks):
    if reverse:
        (zr_ref, zk_ref, zv_ref, zl_ref, w0_ref, wt_ref, a0_ref, wa_ref, kk_ref, ka_ref, rk_ref,
         y0_ref, g_ref, bon0_ref, lnw_ref, lnb_ref, out_ref,
         s_ref, kk_s, bt_s, kd_s, rt_s, v_s, y_s, ge_s) = refs
    else:
        (zr_ref, zk_ref, zv_ref, zl_ref, w0_ref, wt_ref, a0_ref, wa_ref, kk_ref, ka_ref, rk_ref, wg_ref,
         y_out_ref, g_out_ref, bon_out_ref,
         s_ref, kk_s, bt_s, kd_s, rt_s, v_s, y_s, ge_s) = refs
    ch = RW_CHUNK
    j = pl.program_id(1)

    @pl.when(j == 0)
    def _():
        s_ref[...] = jnp.zeros_like(s_ref)

    zl = zl_ref[...]
    lora_in = zl[:, 0:LANES]
    wl = w0_ref[...] + _dot(jnp.tanh(lora_in), wt_ref[...])
    wlog = -(jnp.maximum(-wl, 0.0) + jnp.log(1.0 + jnp.exp(-jnp.abs(wl)))) - 0.5
    lw = -jnp.exp(wlog)
    a = jax.nn.sigmoid(a0_ref[...] + _dot(lora_in, wa_ref[...]))
    r = zr_ref[...]
    k = zk_ref[...]
    v = zv_ref[...]
    kd = k * (1.0 + (a - 1.0) * ka_ref[...])
    kraw = k * kk_ref[...]
    rkd = r * kd * rk_ref[...]
    if not reverse:
        g_out_ref[...] = _dot(jax.nn.sigmoid(zl[:, LANES:3 * LANES]), wg_ref[...])

    lane = lax.broadcasted_iota(jnp.int32, (1, LANES), 1)
    m0 = lane < B_HEAD
    ti = lax.broadcasted_iota(jnp.int32, (ch, ch), 0)
    si = lax.broadcasted_iota(jnp.int32, (ch, ch), 1)
    tri = ((si >= ti) if reverse else (si <= ti)).astype(BF16)

    kn_l, b_l = [], []
    for p in range(RW_PAIRS):
        cols = slice(p * LANES, (p + 1) * LANES)
        kp = kraw[:, cols]
        nrm = jnp.sqrt(_head_sums(kp * kp, m0))
        kn = kp / jnp.maximum(nrm, 1e-12)
        kn_l.append(kn)
        b_l.append(kn * a[:, cols])
        v_s[p] = v[:, cols]
        if not reverse:
            bon_out_ref[:, cols] = _head_sums(rkd[:, cols], m0) * v[:, cols]

    for c in range(n_chunks):
        rws = slice(c * ch, (c + 1) * ch)
        lw_c = lw[rws, :]
        hi = lw_c.astype(BF16)
        rem = lw_c - hi.astype(F32)
        mid = rem.astype(BF16)
        lo = (rem - mid.astype(F32)).astype(BF16)
        cs = (jnp.dot(tri, hi, preferred_element_type=F32) + jnp.dot(tri, mid, preferred_element_type=F32)
              + jnp.dot(tri, lo, preferred_element_type=F32))
        total = cs[0:1, :] if reverse else cs[ch - 1:ch, :]
        g_in = jnp.exp(cs)
        g_ex = jnp.exp(cs - lw_c)
        g_inv = jnp.exp(-cs)
        g_end = jnp.exp(total)
        for p in range(RW_PAIRS):
            cols = slice(p * LANES, (p + 1) * LANES)
            kk_s[p, rws, :] = kn_l[p][rws, :] * g_ex[:, cols]
            bt_s[p, rws, :] = b_l[p][rws, :] * g_inv[:, cols]
            kd_s[p, rws, :] = kd[rws, cols] * g_inv[:, cols]
            rt_s[p, rws, :] = r[rws, cols] * g_in[:, cols]
            ge_s[p, c:c + 1, :] = g_end[:, cols]

    ri = lax.broadcasted_iota(jnp.int32, (2 * ch, 2 * ch), 0)
    ci = lax.broadcasted_iota(jnp.int32, (2 * ch, 2 * ch), 1)
    same_head = (ri // ch) == (ci // ch)
    if reverse:
        strict = same_head & (ci > ri)
        incl = same_head & (ci >= ri)
    else:
        strict = same_head & (ci < ri)
        incl = same_head & (ci <= ri)
    eye = (ri == ci).astype(F32)
    zero = jnp.zeros((2 * ch, 2 * ch), F32)
    n_sq = int(math.log2(INV_BLOCK)) - 1
    blk_diag = []
    bs = INV_BLOCK
    while bs <= ch:
        blk_diag.append((ri // bs) == (ci // bs))
        bs *= 2
    pairs = range(RW_PAIRS)

    def chunk_body(ci_, carry):
        c = (n_chunks - 1 - ci_) if reverse else ci_
        rows = pl.ds(pl.multiple_of(c * ch, ch), ch)
        kk_t = [kk_s[p, rows, :] for p in pairs]
        rt = [rt_s[p, rows, :] for p in pairs]
        bt_m = [_mask_lanes(bt_s[p, rows, :], m0) for p in pairs]
        kd_m = [_mask_lanes(kd_s[p, rows, :], m0) for p in pairs]
        v_st = [_mask_lanes(v_s[p, rows, :], m0) for p in pairs]
        g_end = [ge_s[p, pl.ds(c, 1), :] for p in pairs]

        pq = [_dot_nt(jnp.concatenate([kk_t[p], kk_t[p], rt[p], rt[p]], axis=0),
                      jnp.concatenate([bt_m[p], kd_m[p]], axis=0)) for p in pairs]
        a_m = [jnp.where(strict, pq[p][0:2 * ch, 0:2 * ch], zero) for p in pairs]
        b_m = [jnp.where(strict, pq[p][0:2 * ch, 2 * ch:4 * ch], zero) for p in pairs]
        m2 = [jnp.where(incl, pq[p][2 * ch:4 * ch, 0:2 * ch], zero) for p in pairs]
        m1 = [jnp.where(incl, pq[p][2 * ch:4 * ch, 2 * ch:4 * ch], zero) for p in pairs]

        q = [jnp.where(blk_diag[0], -a_m[p], zero) for p in pairs]
        t_m = [eye + q[p] for p in pairs]
        q = [_dot(q[p], q[p]) for p in pairs]
        for _ in range(n_sq - 1):
            both = [_dot(jnp.concatenate([q[p], t_m[p]], axis=0), q[p]) for p in pairs]
            q = [both[p][0:2 * ch] for p in pairs]
            t_m = [t_m[p] + both[p][2 * ch:4 * ch] for p in pairs]
        t_m = [t_m[p] + _dot(t_m[p], q[p]) for p in pairs]
        for lvl in range(1, len(blk_diag)):
            off = [jnp.where(blk_diag[lvl] & ~blk_diag[lvl - 1], a_m[p], zero) for p in pairs]
            t_off = [_dot(t_m[p], off[p]) for p in pairs]
            t_m = [t_m[p] - _dot(t_off[p], t_m[p]) for p in pairs]

        bv = [_dot(b_m[p], v_st[p]) for p in pairs]
        ku = [_dot(t_m[p], jnp.concatenate([_mask_lanes(kk_t[p], m0), bv[p]], axis=1)) for p in pairs]
        m2ku = [_dot(m2[p], ku[p]) for p in pairs]
        m1v = [_dot(m1[p], v_st[p]) for p in pairs]
        s_bd = [s_ref[p] for p in pairs]
        ys = []
        for p in pairs:
            rp_s = _mask_lanes(rt[p], m0) - m2ku[p][:, 0:LANES]
            y0_s = m1v[p] - m2ku[p][:, LANES:2 * LANES]
            rp = rp_s[0:ch] + rp_s[ch:2 * ch]
            y0 = y0_s[0:ch] + y0_s[ch:2 * ch]
            ys.append(_dot_nt(rp, s_bd[p]) + y0)
        u_all = [_dot_nt(ku[p][:, 0:LANES], s_bd[p]) + ku[p][:, LANES:2 * LANES] for p in pairs]
        for p in pairs:
            upd = _dot_tn(jnp.concatenate([v_st[p], -u_all[p]], axis=0),
                          jnp.concatenate([kd_m[p] * g_end[p], bt_m[p] * g_end[p]], axis=0))
            s_ref[p] = s_bd[p] * g_end[p] + upd
            y_s[p, rows, :] = ys[p]
        return carry

    lax.fori_loop(0, n_chunks, chunk_body, 0)

    for p in range(RW_PAIRS):
        cols = slice(p * LANES, (p + 1) * LANES)
        if not reverse:
            y_out_ref[:, cols] = y_s[p]
        else:
            bon1 = _head_sums(rkd[:, cols], m0) * v[:, cols]
            ysum = y0_ref[:, cols] + y_s[p]
            mu = _head_sums(ysum, m0) * (1.0 / B_HEAD)
            dlt = ysum - mu
            var = _head_sums(dlt * dlt, m0) * (1.0 / B_HEAD)
            yn = dlt * lax.rsqrt(var + GN_EPS)
            o = (yn * lnw_ref[:, cols] + lnb_ref[:, cols] + bon0_ref[:, cols] + bon1) * g_ref[:, cols]
            out_ref[:, cols] = o.astype(out_ref.dtype)


def rwkv7_bidir(zs, w0, w_up, a0, a_up, g_up, k_k, k_a, r_k, lnx_w, lnx_b, *, batch, seq, ctx_len):
    m = zs.shape[0]
    n_lat = batch * seq
    tb = RW_TILE
    nct, nlt = ctx_len // tb, seq // tb
    steps = nct + nlt

    def row_block(reverse):
        def f(b, j):
            jc = (nct - 1 - j) if reverse else j
            jl = (nlt - 1 - (j - nct)) if reverse else (j - nct)
            return jnp.where(j < nct, (n_lat + b * ctx_len) // tb + jc, b * nlt + jl)
        return f

    def pad_rows(w, top, total):
        return jnp.pad(w, ((top, total - top - w.shape[0]), (0, 0)))

    wg = pad_rows(g_up, 0, 2 * LANES)
    rk = r_k.reshape(1, MIX_B)
    vec = lambda a: a.reshape(1, MIX_B)
    scratch = ([pltpu.VMEM((RW_PAIRS, LANES, LANES), F32)] + [pltpu.VMEM((RW_PAIRS, tb, LANES), F32)] * 6
               + [pltpu.VMEM((RW_PAIRS, 8, LANES), F32)])
    const = lambda shape: pl.BlockSpec(shape, lambda b, j: (0,) * len(shape))

    def call(reverse, extra_in, extra_specs, out_shapes, out_specs, d):
        rb = row_block(reverse)
        wt = pad_rows(w_up[d], 0, LANES)
        wa = pad_rows(a_up[d], DECAY_LORA, LANES)
        in_specs = [
            pl.BlockSpec((tb, MIX_B), lambda b, j: (rb(b, j), 0)),
            pl.BlockSpec((tb, MIX_B), lambda b, j: (rb(b, j), 1)),
            pl.BlockSpec((tb, MIX_B), lambda b, j: (rb(b, j), 2)),
            pl.BlockSpec((tb, 512), lambda b, j: (rb(b, j), 6)),
            const((1, MIX_B)), const((LANES, MIX_B)), const((1, MIX_B)), const((LANES, MIX_B)),
            const((1, MIX_B)), const((1, MIX_B)), const((1, MIX_B)),
        ] + extra_specs
        args = [zs, zs, zs, zs, vec(w0[d]), wt, vec(a0[d]), wa, vec(k_k), vec(k_a), rk] + extra_in
        return pl.pallas_call(
            functools.partial(_rwkv_kernel, reverse=reverse, n_chunks=tb // RW_CHUNK),
            grid=(batch, steps),
            in_specs=in_specs,
            out_specs=out_specs,
            out_shape=out_shapes,
            scratch_shapes=scratch,
            compiler_params=_cparams(("parallel", "arbitrary")),
            name="rwkv7_rev" if reverse else "rwkv7_fwd",
        )(*args)

    rbf = row_block(False)
    tile_f = pl.BlockSpec((tb, MIX_B), lambda b, j: (rbf(b, j), 0))
    y0, g, bon0 = call(False, [wg], [const((2 * LANES, MIX_B))],
                       [jax.ShapeDtypeStruct((m, MIX_B), F32)] * 3, [tile_f] * 3, 0)
    rbr = row_block(True)
    tile_r = pl.BlockSpec((tb, MIX_B), lambda b, j: (rbr(b, j), 0))
    out = call(True, [y0, g, bon0, vec(lnx_w), vec(lnx_b)],
               [tile_r, tile_r, tile_r, const((1, MIX_B)), const((1, MIX_B))],
               jax.ShapeDtypeStruct((m, MIX_B), BF16), tile_r, 1)
    return out


def _half_rms(x, gain, m0):
    ms = _head_sums(x * x, m0) * (1.0 / C_HEAD)
    return x * lax.rsqrt(ms + EPS) * gain


def _rope(x, cos, sin):
    lane = lax.broadcasted_iota(jnp.int32, x.shape, 1)
    first = (lane % ROPE_AXIS) < (ROPE_AXIS // 2)
    half = ROPE_AXIS // 2
    rot = jnp.where(first, -pltpu.roll(x, LANES - half, axis=1), pltpu.roll(x, half, axis=1))
    return x * cos + rot * sin


def _attn_kernel(q_ref, kl_ref, vl_ref, kc_ref, vc_ref, cq_ref, sq_ref, ck_ref, sk_ref, qg_ref, kg_ref,
                 lam_ref, sg_ref, o_ref, k_s, v_s, *, ctx_len, lam_init, q_blk):
    qi = pl.program_id(2)
    lane = lax.broadcasted_iota(jnp.int32, (1, LANES), 1)
    m0 = lane < C_HEAD

    @pl.when(qi == 0)
    def _():
        kc = _half_rms(kc_ref[...].astype(F32), kg_ref[...], m0)
        kl = _rope(_half_rms(kl_ref[...].astype(F32), kg_ref[...], m0), ck_ref[...], sk_ref[...])
        k_s[0:ctx_len, :] = kc.astype(BF16)
        k_s[ctx_len:, :] = kl.astype(BF16)
        v_s[0:ctx_len, :] = vc_ref[...].astype(BF16)
        v_s[ctx_len:, :] = vl_ref[...].astype(BF16)

    lp = lam_ref[...]
    lam = (jnp.exp(jnp.sum(lp[0:1] * lp[1:2], keepdims=True)) - jnp.exp(jnp.sum(lp[2:3] * lp[3:4], keepdims=True))
           + lam_init)
    q = (_rope(_half_rms(q_ref[...].astype(F32), qg_ref[...], m0), cq_ref[...], sq_ref[...])
         * (C_HEAD ** -0.5 * LOG2E))
    keys = k_s[...]
    vals = v_s[...]
    zq = jnp.zeros_like(q)
    q_sub = [jnp.where(m0, q, zq), jnp.where(m0, zq, q)]
    blocks = [(slice(r0, r0 + q_blk), i) for r0 in range(0, q.shape[0], q_blk) for i in (0, 1)]
    lookahead = 2
    scores = [_dot_nt(q_sub[i][rows], keys) for rows, i in blocks[:lookahead]]
    outs = []
    for n, (rows, i) in enumerate(blocks):
        s = scores[n]
        e = jnp.exp2(s - jnp.max(s, axis=-1, keepdims=True))
        w = (lam if i else 1.0) / jnp.sum(e, axis=-1, keepdims=True)
        if n + lookahead < len(blocks):
            rows2, i2 = blocks[n + lookahead]
            scores.append(_dot_nt(q_sub[i2][rows2], keys))
        outs.append(jnp.dot(e.astype(BF16), vals, preferred_element_type=F32) * w)
    for n, r0 in enumerate(range(0, q.shape[0], q_blk)):
        o = outs[2 * n] - outs[2 * n + 1]
        ms = jnp.mean(o * o, axis=-1, keepdims=True)
        o = o * lax.rsqrt(ms + SUBLN_EPS) * sg_ref[...] * (1.0 - lam_init)
        o_ref[r0:r0 + q_blk, :] = o.astype(o_ref.dtype)


def _rope_tables(n):
    rows = n // GRID_W
    row = jnp.broadcast_to(jnp.arange(rows, dtype=F32)[:, None], (rows, GRID_W)).reshape(-1)
    col = jnp.broadcast_to(jnp.arange(GRID_W, dtype=F32)[None, :], (rows, GRID_W)).reshape(-1)
    inv = ROPE_THETA ** (-jnp.arange(0, ROPE_AXIS, 2, dtype=F32) / ROPE_AXIS)
    ar = row[:, None] * inv
    ac = col[:, None] * inv
    ang = jnp.concatenate([ar, ar, ac, ac, ar, ar, ac, ac], axis=-1)
    return jnp.cos(ang), jnp.sin(ang)


def diff_attention(qkv, q_g, k_g, lam_params, subln_g, lam_init, *, batch, seq, ctx_len, tq=1024):
    n_lat = batch * seq
    tq = min(tq, seq)
    cos, sin = _rope_tables(seq)
    nq = seq // tq
    hq, hk, hv = 0, D_MODEL // LANES, 2 * D_MODEL // LANES
    ctx_blk0 = n_lat // ctx_len
    two = lambda a: jnp.concatenate([a, a]).reshape(1, LANES)
    const = lambda shape: pl.BlockSpec(shape, lambda b, h, i: (0,) * len(shape))
    return pl.pallas_call(
        functools.partial(_attn_kernel, ctx_len=ctx_len, lam_init=lam_init, q_blk=min(256, tq)),
        grid=(batch, C_HEADS, nq),
        in_specs=[
            pl.BlockSpec((tq, LANES), lambda b, h, i: (b * nq + i, hq + h)),
            pl.BlockSpec((seq, LANES), lambda b, h, i: (b, hk + h)),
            pl.BlockSpec((seq, LANES), lambda b, h, i: (b, hv + h)),
            pl.BlockSpec((ctx_len, LANES), lambda b, h, i: (ctx_blk0 + b, hk + h)),
            pl.BlockSpec((ctx_len, LANES), lambda b, h, i: (ctx_blk0 + b, hv + h)),
            pl.BlockSpec((tq, LANES), lambda b, h, i: (i, 0)),
            pl.BlockSpec((tq, LANES), lambda b, h, i: (i, 0)),
            const((seq, LANES)), const((seq, LANES)),
            const((1, LANES)), const((1, LANES)), const((4, C_HEAD)), const((1, LANES)),
        ],
        out_specs=pl.BlockSpec((tq, LANES), lambda b, h, i: (b * nq + i, h)),
        out_shape=jax.ShapeDtypeStruct((n_lat, D_MODEL), BF16),
        scratch_shapes=[pltpu.VMEM((ctx_len + seq, LANES), BF16)] * 2,
        compiler_params=_cparams(("parallel", "parallel", "arbitrary")),
        name="diff_attention",
    )(qkv, qkv, qkv, qkv, qkv, cos, sin, cos, sin, two(q_g), two(k_g), lam_params, subln_g.reshape(1, LANES))


def _route_kernel(x_ref, g_ref, mod_ref, rw_ref, xn_ref, idx_ref, gate_ref, *, s_idx, c_idx):
    x = x_ref[...]
    ms = jnp.mean(x * x, axis=-1, keepdims=True)
    y = x * lax.rsqrt(ms + EPS) * g_ref[...]
    h = y * (1.0 + mod_ref[0, c_idx:c_idx + 1, :]) + mod_ref[0, s_idx:s_idx + 1, :]
    xn_ref[...] = h
    logits = jnp.dot(h, rw_ref[...], precision=lax.Precision.HIGHEST, preferred_element_type=F32)
    lane = lax.broadcasted_iota(jnp.int32, logits.shape, 1)
    neg = jnp.float32(-jnp.inf)
    lg = jnp.where(lane < N_EXPERTS, logits, neg)
    m1 = jnp.max(lg, axis=-1, keepdims=True)
    i1 = jnp.min(jnp.where(lg == m1, lane, LANES), axis=-1, keepdims=True)
    lg2 = jnp.where(lane == i1, neg, lg)
    m2 = jnp.max(lg2, axis=-1, keepdims=True)
    i2 = jnp.min(jnp.where(lg2 == m2, lane, LANES), axis=-1, keepdims=True)
    e2 = jnp.exp(m2 - m1)
    g1 = 1.0 / (1.0 + e2)
    g2 = e2 * g1
    idx_ref[...] = jnp.where(lane == 0, i1, jnp.where(lane == 1, i2, 0))
    gate_ref[...] = jnp.where(lane == 0, g1, jnp.where(lane == 1, g2, 0.0))


def route(x, gain, mods, router, s_idx, c_idx, *, seq, batch, tm=256):
    m, d = x.shape
    rw = jnp.pad(router, ((0, 0), (0, LANES - N_EXPERTS)))
    return pl.pallas_call(
        functools.partial(_route_kernel, s_idx=s_idx, c_idx=c_idx),
        grid=(m // tm,),
        in_specs=[
            pl.BlockSpec((tm, d), lambda i: (i, 0)),
            pl.BlockSpec((1, d), lambda i: (0, 0)),
            pl.BlockSpec((1, 6, d), lambda i: ((i * tm) // seq, 0, 0)),
            pl.BlockSpec((d, LANES), lambda i: (0, 0)),
        ],
        out_specs=[pl.BlockSpec((tm, d), lambda i: (i, 0)),
                   pl.BlockSpec((tm, LANES), lambda i: (i, 0)),
                   pl.BlockSpec((tm, LANES), lambda i: (i, 0))],
        out_shape=[jax.ShapeDtypeStruct((m, d), F32),
                   jax.ShapeDtypeStruct((m, LANES), jnp.int32),
                   jax.ShapeDtypeStruct((m, LANES), F32)],
        compiler_params=_cparams(("parallel",)),
        name="moe_route",
    )(x, gain.reshape(1, d), mods, rw)


def _row_copy(src_hbm, dst_ref, src_row, dst_row, sem):
    return pltpu.make_async_copy(src_hbm.at[pl.ds(src_row, 1)], dst_ref.at[pl.ds(dst_row, 1)], sem)


def _gather_kernel(used_ref, idx_ref, src_hbm, o_ref, buf, sem, *, tg):
    i = pl.program_id(0)

    @pl.when(i * tg >= used_ref[0])
    def _():
        o_ref[...] = jnp.zeros_like(o_ref)

    @pl.when(i * tg < used_ref[0])
    def _():
        def issue(g, c):
            for u in range(DMA_UNROLL):
                r = g * DMA_UNROLL + u
                _row_copy(src_hbm, buf, idx_ref[0, 0, r], r, sem).start(priority=u % 2)
            return c

        def drain(r, c):
            _row_copy(src_hbm, buf, 0, r, sem).wait()
            return c

        lax.fori_loop(0, tg // DMA_UNROLL, issue, 0)
        lax.fori_loop(0, tg, drain, 0, unroll=DMA_UNROLL)
        o_ref[...] = buf[...].astype(o_ref.dtype)


def gather_rows(src, idx, rows_used, *, tg=256):
    mp = idx.shape[0]
    d = src.shape[1]
    return pl.pallas_call(
        functools.partial(_gather_kernel, tg=tg),
        grid_spec=pltpu.PrefetchScalarGridSpec(
            num_scalar_prefetch=1,
            grid=(mp // tg,),
            in_specs=[pl.BlockSpec((1, 1, tg), lambda i, u: (i, 0, 0), memory_space=pltpu.SMEM),
                      pl.BlockSpec(memory_space=pl.ANY)],
            out_specs=pl.BlockSpec((tg, d), lambda i, u: (i, 0)),
            scratch_shapes=[pltpu.VMEM((tg, d), src.dtype), pltpu.SemaphoreType.DMA(())],
        ),
        out_shape=jax.ShapeDtypeStruct((mp, d), BF16),
        compiler_params=_cparams(("arbitrary",)),
        name="moe_gather",
    )(rows_used, idx.reshape(mp // tg, 1, tg), src)


def _gmm_kernel(te_ref, nu_ref, *refs, kind, kc):
    if kind == "swiglu":
        x_ref, w1_ref, w3_ref, o_ref = refs
        w_refs = (w1_ref, w3_ref)
    else:
        x_ref, w1_ref, o_ref = refs
        w_refs = (w1_ref,)
    i = pl.program_id(1)

    @pl.when(i >= nu_ref[0])
    def _():
        o_ref[...] = jnp.zeros_like(o_ref)

    @pl.when(i < nu_ref[0])
    def _():
        kdim = x_ref.shape[1]
        accs = [None] * len(w_refs)
        for k0 in range(0, kdim, kc):
            x = x_ref[:, k0:k0 + kc].astype(BF16)
            for n, w_ref in enumerate(w_refs):
                p = jnp.dot(x, w_ref[0, k0:k0 + kc, :].astype(BF16), preferred_element_type=F32)
                accs[n] = p if accs[n] is None else accs[n] + p
        a = accs[0]
        if kind == "swiglu":
            a = jax.nn.silu(a) * accs[1]
        o_ref[...] = a.astype(o_ref.dtype)


def grouped_matmul(x, w, tile_expert, n_used, *, w3=None, out_dtype=F32, tm=512, tn=512, kc=1024):
    mp, kdim = x.shape
    n = w.shape[2]
    kind = "swiglu" if w3 is not None else "plain"
    w_spec = pl.BlockSpec((1, kdim, tn), lambda j, i, te, nu: (te[i], 0, j))
    in_specs = [pl.BlockSpec((tm, kdim), lambda j, i, te, nu: (i, 0)), w_spec]
    args = [x, w]
    if w3 is not None:
        in_specs.append(w_spec)
        args.append(w3)
    return pl.pallas_call(
        functools.partial(_gmm_kernel, kind=kind, kc=min(kc, kdim)),
        grid_spec=pltpu.PrefetchScalarGridSpec(
            num_scalar_prefetch=2,
            grid=(n // tn, mp // tm),
            in_specs=in_specs,
            out_specs=pl.BlockSpec((tm, tn), lambda j, i, te, nu: (i, j)),
        ),
        out_shape=jax.ShapeDtypeStruct((mp, n), out_dtype),
        compiler_params=pltpu.CompilerParams(dimension_semantics=("arbitrary", "arbitrary"),
                                             vmem_limit_bytes=GMM_VMEM_LIMIT),
        name="moe_gmm_" + kind,
    )(tile_expert, n_used, *args)


def _combine_kernel(pos_ref, ys_hbm, x_ref, gate_ref, mod_ref, o_ref, buf, sem, *, tc, gate_idx):
    def issue(g, c):
        for u in range(DMA_UNROLL):
            r = g * DMA_UNROLL + u
            _row_copy(ys_hbm, buf.at[0], pos_ref[0, 0, 2 * r], r, sem).start(priority=0)
            _row_copy(ys_hbm, buf.at[1], pos_ref[0, 0, 2 * r + 1], r, sem).start(priority=1)
        return c

    def drain(r, c):
        _row_copy(ys_hbm, buf.at[0], 0, r, sem).wait()
        _row_copy(ys_hbm, buf.at[1], 0, r, sem).wait()
        return c

    lax.fori_loop(0, tc // DMA_UNROLL, issue, 0)
    lax.fori_loop(0, tc, drain, 0, unroll=DMA_UNROLL)
    g = gate_ref[...]
    moe = g[:, 0:1] * buf[0] + g[:, 1:2] * buf[1]
    o_ref[...] = x_ref[...] + mod_ref[0, gate_idx:gate_idx + 1, :] * moe


def moe_combine(ys, pos, x, gates, mods, gate_idx, *, seq, tc=256):
    m, d = x.shape
    return pl.pallas_call(
        functools.partial(_combine_kernel, tc=tc, gate_idx=gate_idx),
        grid=(m // tc,),
        in_specs=[pl.BlockSpec((1, 1, 2 * tc), lambda i: (i, 0, 0), memory_space=pltpu.SMEM),
                  pl.BlockSpec(memory_space=pl.ANY),
                  pl.BlockSpec((tc, d), lambda i: (i, 0)),
                  pl.BlockSpec((tc, LANES), lambda i: (i, 0)),
                  pl.BlockSpec((1, 6, d), lambda i: ((i * tc) // seq, 0, 0))],
        out_specs=pl.BlockSpec((tc, d), lambda i: (i, 0)),
        out_shape=jax.ShapeDtypeStruct((m, d), F32),
        scratch_shapes=[pltpu.VMEM((2, tc, d), F32), pltpu.SemaphoreType.DMA(())],
        compiler_params=_cparams(("arbitrary",)),
        name="moe_combine",
    )(pos.reshape(m // tc, 1, 2 * tc), ys, x, gates, mods)


def moe_layer(x, gain, mods, router, w1, w3, w2, *, seq, batch, tm=512):
    n = x.shape[0]
    xn, idx, gates = route(x, gain, mods, router, 3, 4, seq=seq, batch=batch)
    e_flat = idx[:, 0:2].reshape(-1)
    onehot = (e_flat[:, None] == jnp.arange(N_EXPERTS)[None, :]).astype(jnp.int32)
    ranks = jnp.cumsum(onehot, axis=0) - onehot
    rank = jnp.sum(ranks * onehot, axis=1)
    counts = jnp.sum(onehot, axis=0)
    padded = ((counts + tm - 1) // tm) * tm
    starts = jnp.cumsum(padded) - padded
    pos = starts[e_flat] + rank
    mp = 2 * n + N_EXPERTS * tm
    token_of_row = jnp.zeros((mp,), jnp.int32).at[pos].set(jnp.arange(2 * n, dtype=jnp.int32) // 2)
    n_tiles = mp // tm
    ends = jnp.cumsum(padded)
    tile_start = jnp.arange(n_tiles, dtype=jnp.int32) * tm
    tile_expert = jnp.minimum(jnp.sum((tile_start[:, None] >= ends[None, :]).astype(jnp.int32), axis=1),
                              N_EXPERTS - 1).astype(jnp.int32)
    n_used = (ends[-1] // tm).astype(jnp.int32).reshape(1)

    xs = gather_rows(xn, token_of_row, ends[-1].astype(jnp.int32).reshape(1))
    hs = grouped_matmul(xs, w1, tile_expert, n_used, w3=w3, out_dtype=BF16, tm=tm, tn=1024)
    ys = grouped_matmul(hs, w2, tile_expert, n_used, out_dtype=F32, tm=tm, tn=512)
    return moe_combine(ys, pos.astype(jnp.int32), x, gates, mods, 5, seq=seq)


def kernel(x, c, ctx, c_ctx, l0_ada_w, l0_ada_b, l0_norm1_g, l0_norm2_g, l0_w_in, l0_sgu_norm_g, l0_sgu_w, l0_sgu_b, l0_shift_w, l0_w0, l0_w_up, l0_a0, l0_a_up, l0_g_up, l0_k_k, l0_k_a, l0_r_k, l0_lnx_w, l0_lnx_b, l0_w_out, l0_ffn_w1, l0_ffn_w3, l0_ffn_w2, l1_ada_w, l1_ada_b, l1_norm1_g, l1_norm2_g, l1_w_qkv, l1_q_norm_g, l1_k_norm_g, l1_lam_q1, l1_lam_k1, l1_lam_q2, l1_lam_k2, l1_subln_g, l1_w_out, l1_router, l1_exp_w1, l1_exp_w3, l1_exp_w2):
    batch, seq, d = x.shape
    ctx_len = ctx.shape[1]
    n_lat = batch * seq
    tm = min(1024, seq, batch * ctx_len)
    tm_res = min(512, tm)
    geo = dict(n_lat=n_lat, seq=seq, batch=batch)
    bf = lambda w: w.astype(BF16)

    xall = jnp.concatenate([x.reshape(n_lat, d), ctx.reshape(batch * ctx_len, d)], axis=0)
    cond = jnp.concatenate([c, c_ctx[None, :], jnp.zeros((16 - batch - 1, d), F32)], axis=0)

    def ada(w, b):
        return matmul(cond, w, bias=b, silu_x=True, tn=1024).reshape(16, 6, d)

    mods = ada(l0_ada_w, l0_ada_b)
    hn = norm_mod(xall, l0_norm1_g, mods, 0, 1, tm=min(512, tm), **geo)
    w_z = bf(jnp.pad(l0_w_in[:, 2 * MIX_A:], ((0, 0), (0, Z_WIDTH - B_WIDTH))))
    hp_a = matmul(hn, bf(l0_w_in[:, :2 * MIX_A]), out_dtype=BF16, tm=tm)
    hp_z = matmul(hn, w_z, tm=tm)
    a_out = chunk_sgu(hp_a, l0_sgu_norm_g, l0_sgu_w, l0_sgu_b)
    zs = token_shift(hp_z, l0_shift_w, n_lat=n_lat, seq=seq, ctx_len=ctx_len)
    b_out = rwkv7_bidir(zs, l0_w0, l0_w_up, l0_a0, l0_a_up, l0_g_up, l0_k_k, l0_k_a, l0_r_k, l0_lnx_w, l0_lnx_b,
                        batch=batch, seq=seq, ctx_len=ctx_len)
    mixed = jnp.concatenate([a_out, b_out], axis=1)
    x1 = matmul(mixed, bf(l0_w_out), res=xall, mods=mods, gate_idx=2, geo=geo, tm=tm_res, tn=d)
    hn = norm_mod(x1, l0_norm2_g, mods, 3, 4, tm=min(512, tm), **geo)
    hff = matmul(hn, bf(l0_ffn_w1), w3=bf(l0_ffn_w3), out_dtype=BF16, tm=tm)
    x2 = matmul(hff, bf(l0_ffn_w2), res=x1, mods=mods, gate_idx=5, geo=geo, tm=tm, tn=1024,
                tk=l0_ffn_w2.shape[0] // 2)

    mods = ada(l1_ada_w, l1_ada_b)
    hn = norm_mod(x2, l1_norm1_g, mods, 0, 1, tm=min(512, tm), **geo)
    qkv = matmul(hn, bf(l1_w_qkv), out_dtype=BF16, tm=tm)
    lam_params = jnp.stack([l1_lam_q1, l1_lam_k1, l1_lam_q2, l1_lam_k2])
    lam_init = 0.8 - 0.6 * math.exp(-0.3 * 1)
    o = diff_attention(qkv, l1_q_norm_g, l1_k_norm_g, lam_params, l1_subln_g, lam_init,
                       batch=batch, seq=seq, ctx_len=ctx_len)
    x3 = matmul(o, bf(l1_w_out), res=x2, mods=mods, gate_idx=2, geo=geo, tm=tm_res, tn=d)
    x4 = moe_layer(x3, l1_norm2_g, mods, l1_router, l1_exp_w1, l1_exp_w3, l1_exp_w2, seq=seq, batch=batch)
    return x4.reshape(batch, seq, d)
```

```python
import functools
import math

import jax
import jax.numpy as jnp
from jax import lax
from jax.experimental import pallas as pl
from jax.experimental.pallas import tpu as pltpu

F32 = jnp.float32
BF16 = jnp.bfloat16

D_MODEL = 2048
GRID_W = 64
EPS = 1e-6

MIX_A = 1024
SGU_CHUNK = 128
A_GROUPS = 8
MIX_B = 1024
B_HEAD = 64
B_HEADS = 16
DECAY_LORA = 64
AAA_LORA = 64
GATE_LORA = 160
B_WIDTH = 3 * MIX_B + DECAY_LORA + AAA_LORA + GATE_LORA
GN_EPS = B_HEAD * 1e-5

C_HEADS = 16
C_HEAD = 64
ROPE_AXIS = C_HEAD // 2
ROPE_THETA = 10000.0
SUBLN_EPS = 1e-5

N_EXPERTS = 8
D_EXPERT = 7168

LOG2E = 1.4426950408889634
LANES = 128
VMEM_LIMIT = 52 * 1024 * 1024
GMM_VMEM_LIMIT = 58 * 1024 * 1024

RW_CHUNK = 64
INV_BLOCK = 16
DMA_UNROLL = 8
RW_TILE = 256
RW_PAIRS = B_HEADS // 2
Z_WIDTH = 3584


def _cparams(sem):
    return pltpu.CompilerParams(dimension_semantics=sem, vmem_limit_bytes=VMEM_LIMIT)


def _group_of_tile(i, tm, n_lat, seq, batch):
    return jnp.where(i * tm < n_lat, (i * tm) // seq, batch)


def _norm_mod_kernel(x_ref, g_ref, mod_ref, o_ref, *, s_idx, c_idx):
    x = x_ref[...]
    ms = jnp.mean(x * x, axis=-1, keepdims=True)
    y = x * lax.rsqrt(ms + EPS) * g_ref[...]
    shift = mod_ref[0, s_idx:s_idx + 1, :]
    scale = mod_ref[0, c_idx:c_idx + 1, :]
    o_ref[...] = (y * (1.0 + scale) + shift).astype(o_ref.dtype)


def norm_mod(x, gain, mods, s_idx, c_idx, *, n_lat, seq, batch, tm=512):
    m, d = x.shape
    grp = functools.partial(_group_of_tile, tm=tm, n_lat=n_lat, seq=seq, batch=batch)
    return pl.pallas_call(
        functools.partial(_norm_mod_kernel, s_idx=s_idx, c_idx=c_idx),
        grid=(m // tm,),
        in_specs=[
            pl.BlockSpec((tm, d), lambda i: (i, 0)),
            pl.BlockSpec((1, d), lambda i: (0, 0)),
            pl.BlockSpec((1, 6, d), lambda i: (grp(i), 0, 0)),
        ],
        out_specs=pl.BlockSpec((tm, d), lambda i: (i, 0)),
        out_shape=jax.ShapeDtypeStruct((m, d), BF16),
        compiler_params=_cparams(("parallel",)),
        name="norm_mod",
    )(x, gain.reshape(1, d), mods)


def _mm_kernel(*refs, kind, nk, silu_x, has_bias, gate_idx):
    refs = list(refs)
    x_ref = refs.pop(0)
    w_refs = [refs.pop(0)]
    if kind == "swiglu":
        w_refs.append(refs.pop(0))
    bias_ref = refs.pop(0) if has_bias else None
    res_ref = mod_ref = None
    if kind == "res":
        res_ref = refs.pop(0)
        mod_ref = refs.pop(0)
    o_ref = refs.pop(0)
    acc_refs = refs

    x = x_ref[...]
    if silu_x:
        x = jax.nn.silu(x.astype(F32))
    x = x.astype(BF16)
    prods = [jnp.dot(x, w_ref[...].astype(BF16), preferred_element_type=F32) for w_ref in w_refs]

    def epilogue(vals):
        a = vals[0]
        if kind == "swiglu":
            a = jax.nn.silu(a) * vals[1]
        if has_bias:
            a = a + bias_ref[...]
        if kind == "res":
            a = res_ref[...] + mod_ref[0, gate_idx:gate_idx + 1, :] * a
        o_ref[...] = a.astype(o_ref.dtype)

    if nk == 1:
        epilogue(prods)
        return

    k = pl.program_id(2)

    @pl.when(k == 0)
    def _():
        for acc, p in zip(acc_refs, prods):
            acc[...] = p

    @pl.when(k > 0)
    def _():
        for acc, p in zip(acc_refs, prods):
            acc[...] += p

    @pl.when(k == nk - 1)
    def _():
        epilogue([acc[...] for acc in acc_refs])


def matmul(x, w, *, w3=None, bias=None, res=None, mods=None, gate_idx=0, out_dtype=F32, silu_x=False,
           tm=1024, tn=512, tk=None, geo=None):
    m, kdim = x.shape
    n = w.shape[1]
    tm = min(tm, m)
    tk = kdim if tk is None else tk
    nk = kdim // tk
    kind = "swiglu" if w3 is not None else ("res" if res is not None else "plain")
    in_specs = [pl.BlockSpec((tm, tk), lambda i, j, k: (i, k)),
                pl.BlockSpec((tk, tn), lambda i, j, k: (k, j))]
    args = [x, w]
    if w3 is not None:
        in_specs.append(pl.BlockSpec((tk, tn), lambda i, j, k: (k, j)))
        args.append(w3)
    if bias is not None:
        in_specs.append(pl.BlockSpec((1, tn), lambda i, j, k: (0, j)))
        args.append(bias.reshape(1, n))
    if res is not None:
        in_specs.append(pl.BlockSpec((tm, tn), lambda i, j, k: (i, j)))
        group = functools.partial(_group_of_tile, tm=tm, **geo)
        in_specs.append(pl.BlockSpec((1, 6, tn), lambda i, j, k: (group(i), 0, j)))
        args += [res, mods]
    n_acc = 0 if nk == 1 else (2 if w3 is not None else 1)
    return pl.pallas_call(
        functools.partial(_mm_kernel, kind=kind, nk=nk, silu_x=silu_x, has_bias=bias is not None,
                          gate_idx=gate_idx),
        grid=(m // tm, n // tn, nk),
        in_specs=in_specs,
        out_specs=pl.BlockSpec((tm, tn), lambda i, j, k: (i, j)),
        out_shape=jax.ShapeDtypeStruct((m, n), out_dtype),
        scratch_shapes=[pltpu.VMEM((tm, tn), F32)] * n_acc,
        compiler_params=_cparams(("parallel", "parallel", "arbitrary")),
        name="matmul_" + kind,
    )(*args)


def _sgu_kernel(u_ref, v_ref, ng_ref, ws_ref, bs_ref, o_ref, *, chunks):
    for c in range(chunks):
        rows = slice(c * SGU_CHUNK, (c + 1) * SGU_CHUNK)
        for g in range(A_GROUPS):
            cols = slice(g * LANES, (g + 1) * LANES)
            vg = jax.nn.gelu(v_ref[rows, cols].astype(F32))
            ms = jnp.mean(vg * vg, axis=-1, keepdims=True)
            vn = vg * lax.rsqrt(ms + EPS) * ng_ref[:, cols]
            s = jnp.dot(ws_ref[g].astype(BF16), vn.astype(BF16), preferred_element_type=F32)
            u = jax.nn.gelu(u_ref[rows, cols].astype(F32))
            o_ref[rows, cols] = (u * (s + bs_ref[:, cols])).astype(o_ref.dtype)


def chunk_sgu(hp, norm_g, ws, bs, *, tm=256):
    m = hp.shape[0]
    bs_exp = jnp.repeat(bs.T, LANES, axis=1)
    return pl.pallas_call(
        functools.partial(_sgu_kernel, chunks=tm // SGU_CHUNK),
        grid=(m // tm,),
        in_specs=[
            pl.BlockSpec((tm, MIX_A), lambda i: (i, 0)),
            pl.BlockSpec((tm, MIX_A), lambda i: (i, 1)),
            pl.BlockSpec((1, MIX_A), lambda i: (0, 0)),
            pl.BlockSpec((A_GROUPS, SGU_CHUNK, SGU_CHUNK), lambda i: (0, 0, 0)),
            pl.BlockSpec((SGU_CHUNK, MIX_A), lambda i: (0, 0)),
        ],
        out_specs=pl.BlockSpec((tm, MIX_A), lambda i: (i, 0)),
        out_shape=jax.ShapeDtypeStruct((m, MIX_A), BF16),
        compiler_params=_cparams(("parallel",)),
        name="chunk_sgu",
    )(hp, hp, norm_g.reshape(1, MIX_A), ws, bs_exp)


def _shift_kernel(x_ref, p_ref, n_ref, w_ref, o_ref, *, ts, lat_tiles, tiles_per_lat, tiles_per_ctx):
    i = pl.program_id(0)
    is_lat = i < lat_tiles
    pos = jnp.where(is_lat, i % tiles_per_lat, (i - lat_tiles) % tiles_per_ctx)
    last = jnp.where(is_lat, tiles_per_lat - 1, tiles_per_ctx - 1)
    x = x_ref[...]
    prev_row = jnp.where(pos == 0, 0.0, p_ref[7:8, :])
    next_row = jnp.where(pos == last, 0.0, n_ref[0:1, :])
    rows = lax.broadcasted_iota(jnp.int32, x.shape, 0)
    xm = jnp.where(rows == 0, prev_row, pltpu.roll(x, 1, axis=0))
    xp = jnp.where(rows == ts - 1, next_row, pltpu.roll(x, ts - 1, axis=0))
    o_ref[...] = w_ref[0:1, :] * xm + w_ref[1:2, :] * x + w_ref[2:3, :] * xp


def token_shift(z, shift_w, *, n_lat, seq, ctx_len, ts=256, tc=Z_WIDTH):
    m = z.shape[0]
    halo = 8
    nblk8 = m // halo
    w_pad = jnp.pad(shift_w, ((0, 0), (0, Z_WIDTH - B_WIDTH)))
    kern = functools.partial(_shift_kernel, ts=ts, lat_tiles=n_lat // ts, tiles_per_lat=seq // ts,
                             tiles_per_ctx=ctx_len // ts)
    return pl.pallas_call(
        kern,
        grid=(m // ts, Z_WIDTH // tc),
        in_specs=[
            pl.BlockSpec((ts, tc), lambda i, j: (i, j)),
            pl.BlockSpec((halo, tc), lambda i, j: (jnp.maximum(i * (ts // halo) - 1, 0), j)),
            pl.BlockSpec((halo, tc), lambda i, j: (jnp.minimum((i + 1) * (ts // halo), nblk8 - 1), j)),
            pl.BlockSpec((3, tc), lambda i, j: (0, j)),
        ],
        out_specs=pl.BlockSpec((ts, tc), lambda i, j: (i, j)),
        out_shape=jax.ShapeDtypeStruct((m, Z_WIDTH), F32),
        compiler_params=_cparams(("parallel", "parallel")),
        name="token_shift",
    )(z, z, z, w_pad)


def _mask_lanes(x, m0):
    z = jnp.zeros_like(x)
    return jnp.concatenate([jnp.where(m0, x, z), jnp.where(m0, z, x)], axis=0)


def _dot(a, b):
    return jnp.dot(a.astype(BF16), b.astype(BF16), preferred_element_type=F32)


def _dot_nt(a, b):
    return lax.dot_general(a.astype(BF16), b.astype(BF16), (((1,), (1,)), ((), ())),
                           preferred_element_type=F32)


def _dot_tn(a, b):
    return lax.dot_general(a.astype(BF16), b.astype(BF16), (((0,), (0,)), ((), ())),
                           preferred_element_type=F32)


def _head_sums(x, m0):
    s0 = jnp.sum(jnp.where(m0, x, 0.0), axis=-1, keepdims=True)
    s1 = jnp.sum(jnp.where(m0, 0.0, x), axis=-1, keepdims=True)
    return jnp.where(m0, s0, s1)


def _rwkv_kernel(*refs, reverse, n_chunks):
    if reverse:
        (zr_ref, zk_ref, zv_ref, zl_ref, w0_ref, wt_ref, a0_ref, wa_ref, kk_ref, ka_ref, rk_ref,
         y0_ref, g_ref, bon0_ref, lnw_ref, lnb_ref, out_ref,
         s_ref, kk_s, bt_s, kd_s, rt_s, v_s, y_s, ge_s) = refs
    else:
        (zr_ref, zk_ref, zv_ref, zl_ref, w0_ref, wt_ref, a0_ref, wa_ref, kk_ref, ka_ref, rk_ref, wg_ref,
         y_out_ref, g_out_ref, bon_out_ref,
         s_ref, kk_s, bt_s, kd_s, rt_s, v_s, y_s, ge_s) = refs
    ch = RW_CHUNK
    j = pl.program_id(1)

    @pl.when(j == 0)
    def _():
        s_ref[...] = jnp.zeros_like(s_ref)

    zl = zl_ref[...]
    lora_in = zl[:, 0:LANES]
    wl = w0_ref[...] + _dot(jnp.tanh(lora_in), wt_ref[...])
    wlog = -(jnp.maximum(-wl, 0.0) + jnp.log(1.0 + jnp.exp(-jnp.abs(wl)))) - 0.5
    lw = -jnp.exp(wlog)
    a = jax.nn.sigmoid(a0_ref[...] + _dot(lora_in, wa_ref[...]))
    r = zr_ref[...]
    k = zk_ref[...]
    v = zv_ref[...]
    kd = k * (1.0 + (a - 1.0) * ka_ref[...])
    kraw = k * kk_ref[...]
    rkd = r * kd * rk_ref[...]
    if not reverse:
        g_out_ref[...] = _dot(jax.nn.sigmoid(zl[:, LANES:3 * LANES]), wg_ref[...])

    lane = lax.broadcasted_iota(jnp.int32, (1, LANES), 1)
    m0 = lane < B_HEAD
    ti = lax.broadcasted_iota(jnp.int32, (ch, ch), 0)
    si = lax.broadcasted_iota(jnp.int32, (ch, ch), 1)
    tri = ((si >= ti) if reverse else (si <= ti)).astype(BF16)

    kn_l, b_l = [], []
    for p in range(RW_PAIRS):
        cols = slice(p * LANES, (p + 1) * LANES)
        kp = kraw[:, cols]
        nrm = jnp.sqrt(_head_sums(kp * kp, m0))
        kn = kp / jnp.maximum(nrm, 1e-12)
        kn_l.append(kn)
        b_l.append(kn * a[:, cols])
        v_s[p] = v[:, cols]
        if not reverse:
            bon_out_ref[:, cols] = _head_sums(rkd[:, cols], m0) * v[:, cols]

    for c in range(n_chunks):
        rws = slice(c * ch, (c + 1) * ch)
        lw_c = lw[rws, :]
        hi = lw_c.astype(BF16)
        rem = lw_c - hi.astype(F32)
        mid = rem.astype(BF16)
        lo = (rem - mid.astype(F32)).astype(BF16)
        cs = (jnp.dot(tri, hi, preferred_element_type=F32) + jnp.dot(tri, mid, preferred_element_type=F32)
              + jnp.dot(tri, lo, preferred_element_type=F32))
        total = cs[0:1, :] if reverse else cs[ch - 1:ch, :]
        g_in = jnp.exp(cs)
        g_ex = jnp.exp(cs - lw_c)
        g_inv = jnp.exp(-cs)
        g_end = jnp.exp(total)
        for p in range(RW_PAIRS):
            cols = slice(p * LANES, (p + 1) * LANES)
            kk_s[p, rws, :] = kn_l[p][rws, :] * g_ex[:, cols]
            bt_s[p, rws, :] = b_l[p][rws, :] * g_inv[:, cols]
            kd_s[p, rws, :] = kd[rws, cols] * g_inv[:, cols]
            rt_s[p, rws, :] = r[rws, cols] * g_in[:, cols]
            ge_s[p, c:c + 1, :] = g_end[:, cols]

    ri = lax.broadcasted_iota(jnp.int32, (2 * ch, 2 * ch), 0)
    ci = lax.broadcasted_iota(jnp.int32, (2 * ch, 2 * ch), 1)
    same_head = (ri // ch) == (ci // ch)
    if reverse:
        strict = same_head & (ci > ri)
        incl = same_head & (ci >= ri)
    else:
        strict = same_head & (ci < ri)
        incl = same_head & (ci <= ri)
    eye = (ri == ci).astype(F32)
    zero = jnp.zeros((2 * ch, 2 * ch), F32)
    n_sq = int(math.log2(INV_BLOCK)) - 1
    blk_diag = []
    bs = INV_BLOCK
    while bs <= ch:
        blk_diag.append((ri // bs) == (ci // bs))
        bs *= 2
    pairs = range(RW_PAIRS)

    def chunk_body(ci_, carry):
        c = (n_chunks - 1 - ci_) if reverse else ci_
        rows = pl.ds(pl.multiple_of(c * ch, ch), ch)
        kk_t = [kk_s[p, rows, :] for p in pairs]
        rt = [rt_s[p, rows, :] for p in pairs]
        bt_m = [_mask_lanes(bt_s[p, rows, :], m0) for p in pairs]
        kd_m = [_mask_lanes(kd_s[p, rows, :], m0) for p in pairs]
        v_st = [_mask_lanes(v_s[p, rows, :], m0) for p in pairs]
        g_end = [ge_s[p, pl.ds(c, 1), :] for p in pairs]

        pq = [_dot_nt(jnp.concatenate([kk_t[p], kk_t[p], rt[p], rt[p]], axis=0),
                      jnp.concatenate([bt_m[p], kd_m[p]], axis=0)) for p in pairs]
        a_m = [jnp.where(strict, pq[p][0:2 * ch, 0:2 * ch], zero) for p in pairs]
        b_m = [jnp.where(strict, pq[p][0:2 * ch, 2 * ch:4 * ch], zero) for p in pairs]
        m2 = [jnp.where(incl, pq[p][2 * ch:4 * ch, 0:2 * ch], zero) for p in pairs]
        m1 = [jnp.where(incl, pq[p][2 * ch:4 * ch, 2 * ch:4 * ch], zero) for p in pairs]

        q = [jnp.where(blk_diag[0], -a_m[p], zero) for p in pairs]
        t_m = [eye + q[p] for p in pairs]
        q = [_dot(q[p], q[p]) for p in pairs]
        for _ in range(n_sq - 1):
            both = [_dot(jnp.concatenate([q[p], t_m[p]], axis=0), q[p]) for p in pairs]
            q = [both[p][0:2 * ch] for p in pairs]
            t_m = [t_m[p] + both[p][2 * ch:4 * ch] for p in pairs]
        t_m = [t_m[p] + _dot(t_m[p], q[p]) for p in pairs]
        for lvl in range(1, len(blk_diag)):
            off = [jnp.where(blk_diag[lvl] & ~blk_diag[lvl - 1], a_m[p], zero) for p in pairs]
            t_off = [_dot(t_m[p], off[p]) for p in pairs]
            t_m = [t_m[p] - _dot(t_off[p], t_m[p]) for p in pairs]

        bm1v = [_dot(jnp.concatenate([b_m[p], m1[p]], axis=0), v_st[p]) for p in pairs]
        ku = [_dot(t_m[p], jnp.concatenate([_mask_lanes(kk_t[p], m0), bm1v[p][0:2 * ch]], axis=1))
              for p in pairs]
        m2ku = [_dot(m2[p], ku[p]) for p in pairs]
        s_bd = [s_ref[p] for p in pairs]
        rp, y0 = [], []
        for p in pairs:
            rp_s = _mask_lanes(rt[p], m0) - m2ku[p][:, 0:LANES]
            y0_s = bm1v[p][2 * ch:4 * ch] - m2ku[p][:, LANES:2 * LANES]
            rp.append(rp_s[0:ch] + rp_s[ch:2 * ch])
            y0.append(y0_s[0:ch] + y0_s[ch:2 * ch])
        rs = [_dot_nt(jnp.concatenate([rp[p], ku[p][:, 0:LANES]], axis=0), s_bd[p]) for p in pairs]
        ys = [rs[p][0:ch] + y0[p] for p in pairs]
        u_all = [rs[p][ch:3 * ch] + ku[p][:, LANES:2 * LANES] for p in pairs]
        for p in pairs:
            upd = _dot_tn(jnp.concatenate([v_st[p], -u_all[p]], axis=0),
                          jnp.concatenate([kd_m[p] * g_end[p], bt_m[p] * g_end[p]], axis=0))
            s_ref[p] = s_bd[p] * g_end[p] + upd
            y_s[p, rows, :] = ys[p]
        return carry

    lax.fori_loop(0, n_chunks, chunk_body, 0)

    for p in range(RW_PAIRS):
        cols = slice(p * LANES, (p + 1) * LANES)
        if not reverse:
            y_out_ref[:, cols] = y_s[p]
        else:
            bon1 = _head_sums(rkd[:, cols], m0) * v[:, cols]
            ysum = y0_ref[:, cols] + y_s[p]
            mu = _head_sums(ysum, m0) * (1.0 / B_HEAD)
            dlt = ysum - mu
            var = _head_sums(dlt * dlt, m0) * (1.0 / B_HEAD)
            yn = dlt * lax.rsqrt(var + GN_EPS)
            o = (yn * lnw_ref[:, cols] + lnb_ref[:, cols] + bon0_ref[:, cols] + bon1) * g_ref[:, cols]
            out_ref[:, cols] = o.astype(out_ref.dtype)


def rwkv7_bidir(zs, w0, w_up, a0, a_up, g_up, k_k, k_a, r_k, lnx_w, lnx_b, *, batch, seq, ctx_len):
    m = zs.shape[0]
    n_lat = batch * seq
    tb = RW_TILE
    nct, nlt = ctx_len // tb, seq // tb
    steps = nct + nlt

    def row_block(reverse):
        def f(b, j):
            jc = (nct - 1 - j) if reverse else j
            jl = (nlt - 1 - (j - nct)) if reverse else (j - nct)
            return jnp.where(j < nct, (n_lat + b * ctx_len) // tb + jc, b * nlt + jl)
        return f

    def pad_rows(w, top, total):
        return jnp.pad(w, ((top, total - top - w.shape[0]), (0, 0)))

    wg = pad_rows(g_up, 0, 2 * LANES)
    rk = r_k.reshape(1, MIX_B)
    vec = lambda a: a.reshape(1, MIX_B)
    scratch = ([pltpu.VMEM((RW_PAIRS, LANES, LANES), F32)] + [pltpu.VMEM((RW_PAIRS, tb, LANES), F32)] * 6
               + [pltpu.VMEM((RW_PAIRS, 8, LANES), F32)])
    const = lambda shape: pl.BlockSpec(shape, lambda b, j: (0,) * len(shape))

    def call(reverse, extra_in, extra_specs, out_shapes, out_specs, d):
        rb = row_block(reverse)
        wt = pad_rows(w_up[d], 0, LANES)
        wa = pad_rows(a_up[d], DECAY_LORA, LANES)
        in_specs = [
            pl.BlockSpec((tb, MIX_B), lambda b, j: (rb(b, j), 0)),
            pl.BlockSpec((tb, MIX_B), lambda b, j: (rb(b, j), 1)),
            pl.BlockSpec((tb, MIX_B), lambda b, j: (rb(b, j), 2)),
            pl.BlockSpec((tb, 512), lambda b, j: (rb(b, j), 6)),
            const((1, MIX_B)), const((LANES, MIX_B)), const((1, MIX_B)), const((LANES, MIX_B)),
            const((1, MIX_B)), const((1, MIX_B)), const((1, MIX_B)),
        ] + extra_specs
        args = [zs, zs, zs, zs, vec(w0[d]), wt, vec(a0[d]), wa, vec(k_k), vec(k_a), rk] + extra_in
        return pl.pallas_call(
            functools.partial(_rwkv_kernel, reverse=reverse, n_chunks=tb // RW_CHUNK),
            grid=(batch, steps),
            in_specs=in_specs,
            out_specs=out_specs,
            out_shape=out_shapes,
            scratch_shapes=scratch,
            compiler_params=_cparams(("parallel", "arbitrary")),
            name="rwkv7_rev" if reverse else "rwkv7_fwd",
        )(*args)

    rbf = row_block(False)
    tile_f = pl.BlockSpec((tb, MIX_B), lambda b, j: (rbf(b, j), 0))
    y0, g, bon0 = call(False, [wg], [const((2 * LANES, MIX_B))],
                       [jax.ShapeDtypeStruct((m, MIX_B), F32)] * 3, [tile_f] * 3, 0)
    rbr = row_block(True)
    tile_r = pl.BlockSpec((tb, MIX_B), lambda b, j: (rbr(b, j), 0))
    out = call(True, [y0, g, bon0, vec(lnx_w), vec(lnx_b)],
               [tile_r, tile_r, tile_r, const((1, MIX_B)), const((1, MIX_B))],
               jax.ShapeDtypeStruct((m, MIX_B), BF16), tile_r, 1)
    return out


def _half_rms(x, gain, m0):
    ms = _head_sums(x * x, m0) * (1.0 / C_HEAD)
    return x * lax.rsqrt(ms + EPS) * gain


def _rope(x, cos, sin):
    lane = lax.broadcasted_iota(jnp.int32, x.shape, 1)
    first = (lane % ROPE_AXIS) < (ROPE_AXIS // 2)
    half = ROPE_AXIS // 2
    rot = jnp.where(first, -pltpu.roll(x, LANES - half, axis=1), pltpu.roll(x, half, axis=1))
    return x * cos + rot * sin


def _attn_kernel(q_ref, kl_ref, vl_ref, kc_ref, vc_ref, cq_ref, sq_ref, ck_ref, sk_ref, qg_ref, kg_ref,
                 lam_ref, sg_ref, o_ref, k_s, v_s, *, ctx_len, lam_init, q_blk):
    qi = pl.program_id(2)
    lane = lax.broadcasted_iota(jnp.int32, (1, LANES), 1)
    m0 = lane < C_HEAD

    @pl.when(qi == 0)
    def _():
        kc = _half_rms(kc_ref[...].astype(F32), kg_ref[...], m0)
        kl = _rope(_half_rms(kl_ref[...].astype(F32), kg_ref[...], m0), ck_ref[...], sk_ref[...])
        k_s[0:ctx_len, :] = kc.astype(BF16)
        k_s[ctx_len:, :] = kl.astype(BF16)
        v_s[0:ctx_len, :] = vc_ref[...].astype(BF16)
        v_s[ctx_len:, :] = vl_ref[...].astype(BF16)

    lp = lam_ref[...]
    lam = (jnp.exp(jnp.sum(lp[0:1] * lp[1:2], keepdims=True)) - jnp.exp(jnp.sum(lp[2:3] * lp[3:4], keepdims=True))
           + lam_init)
    q = (_rope(_half_rms(q_ref[...].astype(F32), qg_ref[...], m0), cq_ref[...], sq_ref[...])
         * (C_HEAD ** -0.5 * LOG2E))
    keys = k_s[...]
    vals = v_s[...]
    zq = jnp.zeros_like(q)
    q_sub = [jnp.where(m0, q, zq), jnp.where(m0, zq, q)]
    blocks = [(slice(r0, r0 + q_blk), i) for r0 in range(0, q.shape[0], q_blk) for i in (0, 1)]
    lookahead = 2
    scores = [_dot_nt(q_sub[i][rows], keys) for rows, i in blocks[:lookahead]]
    outs = []
    for n, (rows, i) in enumerate(blocks):
        s = scores[n]
        e = jnp.exp2(s - jnp.max(s, axis=-1, keepdims=True))
        w = (lam if i else 1.0) / jnp.sum(e, axis=-1, keepdims=True)
        if n + lookahead < len(blocks):
            rows2, i2 = blocks[n + lookahead]
            scores.append(_dot_nt(q_sub[i2][rows2], keys))
        outs.append(jnp.dot(e.astype(BF16), vals, preferred_element_type=F32) * w)
    for n, r0 in enumerate(range(0, q.shape[0], q_blk)):
        o = outs[2 * n] - outs[2 * n + 1]
        ms = jnp.mean(o * o, axis=-1, keepdims=True)
        o = o * lax.rsqrt(ms + SUBLN_EPS) * sg_ref[...] * (1.0 - lam_init)
        o_ref[r0:r0 + q_blk, :] = o.astype(o_ref.dtype)


def _rope_tables(n):
    rows = n // GRID_W
    row = jnp.broadcast_to(jnp.arange(rows, dtype=F32)[:, None], (rows, GRID_W)).reshape(-1)
    col = jnp.broadcast_to(jnp.arange(GRID_W, dtype=F32)[None, :], (rows, GRID_W)).reshape(-1)
    inv = ROPE_THETA ** (-jnp.arange(0, ROPE_AXIS, 2, dtype=F32) / ROPE_AXIS)
    ar = row[:, None] * inv
    ac = col[:, None] * inv
    ang = jnp.concatenate([ar, ar, ac, ac, ar, ar, ac, ac], axis=-1)
    return jnp.cos(ang), jnp.sin(ang)


def diff_attention(qkv, q_g, k_g, lam_params, subln_g, lam_init, *, batch, seq, ctx_len, tq=1024):
    n_lat = batch * seq
    tq = min(tq, seq)
    cos, sin = _rope_tables(seq)
    nq = seq // tq
    hq, hk, hv = 0, D_MODEL // LANES, 2 * D_MODEL // LANES
    ctx_blk0 = n_lat // ctx_len
    two = lambda a: jnp.concatenate([a, a]).reshape(1, LANES)
    const = lambda shape: pl.BlockSpec(shape, lambda b, h, i: (0,) * len(shape))
    return pl.pallas_call(
        functools.partial(_attn_kernel, ctx_len=ctx_len, lam_init=lam_init, q_blk=min(256, tq)),
        grid=(batch, C_HEADS, nq),
        in_specs=[
            pl.BlockSpec((tq, LANES), lambda b, h, i: (b * nq + i, hq + h)),
            pl.BlockSpec((seq, LANES), lambda b, h, i: (b, hk + h)),
            pl.BlockSpec((seq, LANES), lambda b, h, i: (b, hv + h)),
            pl.BlockSpec((ctx_len, LANES), lambda b, h, i: (ctx_blk0 + b, hk + h)),
            pl.BlockSpec((ctx_len, LANES), lambda b, h, i: (ctx_blk0 + b, hv + h)),
            pl.BlockSpec((tq, LANES), lambda b, h, i: (i, 0)),
            pl.BlockSpec((tq, LANES), lambda b, h, i: (i, 0)),
            const((seq, LANES)), const((seq, LANES)),
            const((1, LANES)), const((1, LANES)), const((4, C_HEAD)), const((1, LANES)),
        ],
        out_specs=pl.BlockSpec((tq, LANES), lambda b, h, i: (b * nq + i, h)),
        out_shape=jax.ShapeDtypeStruct((n_lat, D_MODEL), BF16),
        scratch_shapes=[pltpu.VMEM((ctx_len + seq, LANES), BF16)] * 2,
        compiler_params=_cparams(("parallel", "parallel", "arbitrary")),
        name="diff_attention",
    )(qkv, qkv, qkv, qkv, qkv, cos, sin, cos, sin, two(q_g), two(k_g), lam_params, subln_g.reshape(1, LANES))


def _route_kernel(x_ref, g_ref, mod_ref, rw_ref, xn_ref, idx_ref, gate_ref, *, s_idx, c_idx):
    x = x_ref[...]
    ms = jnp.mean(x * x, axis=-1, keepdims=True)
    y = x * lax.rsqrt(ms + EPS) * g_ref[...]
    h = y * (1.0 + mod_ref[0, c_idx:c_idx + 1, :]) + mod_ref[0, s_idx:s_idx + 1, :]
    xn_ref[...] = h
    logits = jnp.dot(h, rw_ref[...], precision=lax.Precision.HIGHEST, preferred_element_type=F32)
    lane = lax.broadcasted_iota(jnp.int32, logits.shape, 1)
    neg = jnp.float32(-jnp.inf)
    lg = jnp.where(lane < N_EXPERTS, logits, neg)
    m1 = jnp.max(lg, axis=-1, keepdims=True)
    i1 = jnp.min(jnp.where(lg == m1, lane, LANES), axis=-1, keepdims=True)
    lg2 = jnp.where(lane == i1, neg, lg)
    m2 = jnp.max(lg2, axis=-1, keepdims=True)
    i2 = jnp.min(jnp.where(lg2 == m2, lane, LANES), axis=-1, keepdims=True)
    e2 = jnp.exp(m2 - m1)
    g1 = 1.0 / (1.0 + e2)
    g2 = e2 * g1
    idx_ref[...] = jnp.where(lane == 0, i1, jnp.where(lane == 1, i2, 0))
    gate_ref[...] = jnp.where(lane == 0, g1, jnp.where(lane == 1, g2, 0.0))


def route(x, gain, mods, router, s_idx, c_idx, *, seq, batch, tm=256):
    m, d = x.shape
    rw = jnp.pad(router, ((0, 0), (0, LANES - N_EXPERTS)))
    return pl.pallas_call(
        functools.partial(_route_kernel, s_idx=s_idx, c_idx=c_idx),
        grid=(m // tm,),
        in_specs=[
            pl.BlockSpec((tm, d), lambda i: (i, 0)),
            pl.BlockSpec((1, d), lambda i: (0, 0)),
            pl.BlockSpec((1, 6, d), lambda i: ((i * tm) // seq, 0, 0)),
            pl.BlockSpec((d, LANES), lambda i: (0, 0)),
        ],
        out_specs=[pl.BlockSpec((tm, d), lambda i: (i, 0)),
                   pl.BlockSpec((tm, LANES), lambda i: (i, 0)),
                   pl.BlockSpec((tm, LANES), lambda i: (i, 0))],
        out_shape=[jax.ShapeDtypeStruct((m, d), F32),
                   jax.ShapeDtypeStruct((m, LANES), jnp.int32),
                   jax.ShapeDtypeStruct((m, LANES), F32)],
        compiler_params=_cparams(("parallel",)),
        name="moe_route",
    )(x, gain.reshape(1, d), mods, rw)


def _row_copy(src_hbm, dst_ref, src_row, dst_row, sem):
    return pltpu.make_async_copy(src_hbm.at[pl.ds(src_row, 1)], dst_ref.at[pl.ds(dst_row, 1)], sem)


def _gather_kernel(used_ref, idx_ref, nxt_ref, src_hbm, o_ref, buf, sem, *, tg, n_tiles):
    i = pl.program_id(0)
    slot = i % 2

    def issue_tile(ids_ref, s):
        def issue(g, c):
            for u in range(DMA_UNROLL):
                r = g * DMA_UNROLL + u
                _row_copy(src_hbm, buf.at[s], ids_ref[0, 0, r], r, sem.at[s]).start(priority=u % 2)
            return c

        lax.fori_loop(0, tg // DMA_UNROLL, issue, 0)

    @pl.when((i == 0) & (used_ref[0] > 0))
    def _():
        issue_tile(idx_ref, slot)

    @pl.when((i + 1 < n_tiles) & ((i + 1) * tg < used_ref[0]))
    def _():
        issue_tile(nxt_ref, 1 - slot)

    @pl.when(i * tg >= used_ref[0])
    def _():
        o_ref[...] = jnp.zeros_like(o_ref)

    @pl.when(i * tg < used_ref[0])
    def _():
        def drain(r, c):
            _row_copy(src_hbm, buf.at[slot], 0, r, sem.at[slot]).wait()
            return c

        lax.fori_loop(0, tg, drain, 0, unroll=DMA_UNROLL)
        o_ref[...] = buf[slot].astype(o_ref.dtype)


def gather_rows(src, idx, rows_used, *, tg=256):
    mp = idx.shape[0]
    d = src.shape[1]
    n_tiles = mp // tg
    idx3 = idx.reshape(n_tiles, 1, tg)
    return pl.pallas_call(
        functools.partial(_gather_kernel, tg=tg, n_tiles=n_tiles),
        grid_spec=pltpu.PrefetchScalarGridSpec(
            num_scalar_prefetch=1,
            grid=(n_tiles,),
            in_specs=[pl.BlockSpec((1, 1, tg), lambda i, u: (i, 0, 0), memory_space=pltpu.SMEM),
                      pl.BlockSpec((1, 1, tg), lambda i, u: (jnp.minimum(i + 1, n_tiles - 1), 0, 0),
                                   memory_space=pltpu.SMEM),
                      pl.BlockSpec(memory_space=pl.ANY)],
            out_specs=pl.BlockSpec((tg, d), lambda i, u: (i, 0)),
            scratch_shapes=[pltpu.VMEM((2, tg, d), src.dtype), pltpu.SemaphoreType.DMA((2,))],
        ),
        out_shape=jax.ShapeDtypeStruct((mp, d), BF16),
        compiler_params=_cparams(("arbitrary",)),
        name="moe_gather",
    )(rows_used, idx3, idx3, src)


def _gmm_kernel(te_ref, nu_ref, *refs, kind, kc):
    if kind == "swiglu":
        x_ref, w1_ref, w3_ref, o_ref = refs
        w_refs = (w1_ref, w3_ref)
    else:
        x_ref, w1_ref, o_ref = refs
        w_refs = (w1_ref,)
    i = pl.program_id(1)

    @pl.when(i >= nu_ref[0])
    def _():
        o_ref[...] = jnp.zeros_like(o_ref)

    @pl.when(i < nu_ref[0])
    def _():
        kdim = x_ref.shape[1]
        accs = [None] * len(w_refs)
        for k0 in range(0, kdim, kc):
            x = x_ref[:, k0:k0 + kc].astype(BF16)
            for n, w_ref in enumerate(w_refs):
                p = jnp.dot(x, w_ref[0, k0:k0 + kc, :].astype(BF16), preferred_element_type=F32)
                accs[n] = p if accs[n] is None else accs[n] + p
        a = accs[0]
        if kind == "swiglu":
            a = jax.nn.silu(a) * accs[1]
        o_ref[...] = a.astype(o_ref.dtype)


def grouped_matmul(x, w, tile_expert, n_used, *, w3=None, out_dtype=F32, tm=512, tn=512, kc=1024):
    mp, kdim = x.shape
    n = w.shape[2]
    kind = "swiglu" if w3 is not None else "plain"
    w_spec = pl.BlockSpec((1, kdim, tn), lambda j, i, te, nu: (te[i], 0, j))
    in_specs = [pl.BlockSpec((tm, kdim), lambda j, i, te, nu: (i, 0)), w_spec]
    args = [x, w]
    if w3 is not None:
        in_specs.append(w_spec)
        args.append(w3)
    return pl.pallas_call(
        functools.partial(_gmm_kernel, kind=kind, kc=min(kc, kdim)),
        grid_spec=pltpu.PrefetchScalarGridSpec(
            num_scalar_prefetch=2,
            grid=(n // tn, mp // tm),
            in_specs=in_specs,
            out_specs=pl.BlockSpec((tm, tn), lambda j, i, te, nu: (i, j)),
        ),
        out_shape=jax.ShapeDtypeStruct((mp, n), out_dtype),
        compiler_params=pltpu.CompilerParams(dimension_semantics=("arbitrary", "arbitrary"),
                                             vmem_limit_bytes=GMM_VMEM_LIMIT),
        name="moe_gmm_" + kind,
    )(tile_expert, n_used, *args)


def _combine_kernel(pos_ref, nxt_ref, ys_hbm, x_ref, gate_ref, mod_ref, o_ref, buf, sem, *, tc, n_tiles, gate_idx):
    i = pl.program_id(0)
    slot = i % 2

    def issue_tile(p_ref, s):
        def issue(g, c):
            for u in range(DMA_UNROLL):
                r = g * DMA_UNROLL + u
                _row_copy(ys_hbm, buf.at[s, 0], p_ref[0, 0, 2 * r], r, sem.at[s]).start(priority=0)
                _row_copy(ys_hbm, buf.at[s, 1], p_ref[0, 0, 2 * r + 1], r, sem.at[s]).start(priority=1)
            return c

        lax.fori_loop(0, tc // DMA_UNROLL, issue, 0)

    @pl.when(i == 0)
    def _():
        issue_tile(pos_ref, slot)

    @pl.when(i + 1 < n_tiles)
    def _():
        issue_tile(nxt_ref, 1 - slot)

    def drain(r, c):
        _row_copy(ys_hbm, buf.at[slot, 0], 0, r, sem.at[slot]).wait()
        _row_copy(ys_hbm, buf.at[slot, 1], 0, r, sem.at[slot]).wait()
        return c

    lax.fori_loop(0, tc, drain, 0, unroll=DMA_UNROLL)
    g = gate_ref[...]
    moe = g[:, 0:1] * buf[slot, 0] + g[:, 1:2] * buf[slot, 1]
    o_ref[...] = x_ref[...] + mod_ref[0, gate_idx:gate_idx + 1, :] * moe


def moe_combine(ys, pos, x, gates, mods, gate_idx, *, seq, tc=256):
    m, d = x.shape
    n_tiles = m // tc
    pos3 = pos.reshape(n_tiles, 1, 2 * tc)
    return pl.pallas_call(
        functools.partial(_combine_kernel, tc=tc, n_tiles=n_tiles, gate_idx=gate_idx),
        grid=(n_tiles,),
        in_specs=[pl.BlockSpec((1, 1, 2 * tc), lambda i: (i, 0, 0), memory_space=pltpu.SMEM),
                  pl.BlockSpec((1, 1, 2 * tc), lambda i: (jnp.minimum(i + 1, n_tiles - 1), 0, 0),
                               memory_space=pltpu.SMEM),
                  pl.BlockSpec(memory_space=pl.ANY),
                  pl.BlockSpec((tc, d), lambda i: (i, 0)),
                  pl.BlockSpec((tc, LANES), lambda i: (i, 0)),
                  pl.BlockSpec((1, 6, d), lambda i: ((i * tc) // seq, 0, 0))],
        out_specs=pl.BlockSpec((tc, d), lambda i: (i, 0)),
        out_shape=jax.ShapeDtypeStruct((m, d), F32),
        scratch_shapes=[pltpu.VMEM((2, 2, tc, d), F32), pltpu.SemaphoreType.DMA((2,))],
        compiler_params=_cparams(("arbitrary",)),
        name="moe_combine",
    )(pos3, pos3, ys, x, gates, mods)


def moe_layer(x, gain, mods, router, w1, w3, w2, *, seq, batch, tm=512):
    n = x.shape[0]
    xn, idx, gates = route(x, gain, mods, router, 3, 4, seq=seq, batch=batch)
    e_flat = idx[:, 0:2].reshape(-1)
    onehot = (e_flat[:, None] == jnp.arange(N_EXPERTS)[None, :]).astype(jnp.int32)
    ranks = jnp.cumsum(onehot, axis=0) - onehot
    rank = jnp.sum(ranks * onehot, axis=1)
    counts = jnp.sum(onehot, axis=0)
    padded = ((counts + tm - 1) // tm) * tm
    starts = jnp.cumsum(padded) - padded
    pos = starts[e_flat] + rank
    mp = 2 * n + N_EXPERTS * tm
    token_of_row = jnp.zeros((mp,), jnp.int32).at[pos].set(jnp.arange(2 * n, dtype=jnp.int32) // 2)
    n_tiles = mp // tm
    ends = jnp.cumsum(padded)
    tile_start = jnp.arange(n_tiles, dtype=jnp.int32) * tm
    tile_expert = jnp.minimum(jnp.sum((tile_start[:, None] >= ends[None, :]).astype(jnp.int32), axis=1),
                              N_EXPERTS - 1).astype(jnp.int32)
    n_used = (ends[-1] // tm).astype(jnp.int32).reshape(1)

    xs = gather_rows(xn, token_of_row, ends[-1].astype(jnp.int32).reshape(1))
    hs = grouped_matmul(xs, w1, tile_expert, n_used, w3=w3, out_dtype=BF16, tm=tm, tn=1024)
    ys = grouped_matmul(hs, w2, tile_expert, n_used, out_dtype=F32, tm=tm, tn=512)
    return moe_combine(ys, pos.astype(jnp.int32), x, gates, mods, 5, seq=seq)


def kernel(x, c, ctx, c_ctx, l0_ada_w, l0_ada_b, l0_norm1_g, l0_norm2_g, l0_w_in, l0_sgu_norm_g, l0_sgu_w, l0_sgu_b, l0_shift_w, l0_w0, l0_w_up, l0_a0, l0_a_up, l0_g_up, l0_k_k, l0_k_a, l0_r_k, l0_lnx_w, l0_lnx_b, l0_w_out, l0_ffn_w1, l0_ffn_w3, l0_ffn_w2, l1_ada_w, l1_ada_b, l1_norm1_g, l1_norm2_g, l1_w_qkv, l1_q_norm_g, l1_k_norm_g, l1_lam_q1, l1_lam_k1, l1_lam_q2, l1_lam_k2, l1_subln_g, l1_w_out, l1_router, l1_exp_w1, l1_exp_w3, l1_exp_w2):
    batch, seq, d = x.shape
    ctx_len = ctx.shape[1]
    n_lat = batch * seq
    tm = min(1024, seq, batch * ctx_len)
    tm_res = min(512, tm)
    geo = dict(n_lat=n_lat, seq=seq, batch=batch)
    bf = lambda w: w.astype(BF16)

    xall = jnp.concatenate([x.reshape(n_lat, d), ctx.reshape(batch * ctx_len, d)], axis=0)
    cond = jnp.concatenate([c, c_ctx[None, :], jnp.zeros((16 - batch - 1, d), F32)], axis=0)

    def ada(w, b):
        return matmul(cond, w, bias=b, silu_x=True, tn=1024).reshape(16, 6, d)

    mods = ada(l0_ada_w, l0_ada_b)
    hn = norm_mod(xall, l0_norm1_g, mods, 0, 1, tm=min(512, tm), **geo)
    w_z = bf(jnp.pad(l0_w_in[:, 2 * MIX_A:], ((0, 0), (0, Z_WIDTH - B_WIDTH))))
    hp_a = matmul(hn, bf(l0_w_in[:, :2 * MIX_A]), out_dtype=BF16, tm=tm)
    hp_z = matmul(hn, w_z, tm=tm)
    a_out = chunk_sgu(hp_a, l0_sgu_norm_g, l0_sgu_w, l0_sgu_b)
    zs = token_shift(hp_z, l0_shift_w, n_lat=n_lat, seq=seq, ctx_len=ctx_len)
    b_out = rwkv7_bidir(zs, l0_w0, l0_w_up, l0_a0, l0_a_up, l0_g_up, l0_k_k, l0_k_a, l0_r_k, l0_lnx_w, l0_lnx_b,
                        batch=batch, seq=seq, ctx_len=ctx_len)
    mixed = jnp.concatenate([a_out, b_out], axis=1)
    x1 = matmul(mixed, bf(l0_w_out), res=xall, mods=mods, gate_idx=2, geo=geo, tm=tm_res, tn=d)
    hn = norm_mod(x1, l0_norm2_g, mods, 3, 4, tm=min(512, tm), **geo)
    hff = matmul(hn, bf(l0_ffn_w1), w3=bf(l0_ffn_w3), out_dtype=BF16, tm=tm)
    x2 = matmul(hff, bf(l0_ffn_w2), res=x1, mods=mods, gate_idx=5, geo=geo, tm=tm, tn=1024,
                tk=l0_ffn_w2.shape[0] // 2)

    mods = ada(l1_ada_w, l1_ada_b)
    hn = norm_mod(x2, l1_norm1_g, mods, 0, 1, tm=min(512, tm), **geo)
    qkv = matmul(hn, bf(l1_w_qkv), out_dtype=BF16, tm=tm)
    lam_params = jnp.stack([l1_lam_q1, l1_lam_k1, l1_lam_q2, l1_lam_k2])
    lam_init = 0.8 - 0.6 * math.exp(-0.3 * 1)
    o = diff_attention(qkv, l1_q_norm_g, l1_k_norm_g, lam_params, l1_subln_g, lam_init,
                       batch=batch, seq=seq, ctx_len=ctx_len)
    x3 = matmul(o, bf(l1_w_out), res=x2, mods=mods, gate_idx=2, geo=geo, tm=tm_res, tn=d)
    x4 = moe_layer(x3, l1_norm2_g, mods, l1_router, l1_exp_w1, l1_exp_w3, l1_exp_w2, seq=seq, batch=batch)
    return x4.reshape(batch, seq, d)
```

```python
import functools
import math

import jax
import jax.numpy as jnp
from jax import lax
from jax.experimental import pallas as pl
from jax.experimental.pallas import tpu as pltpu

F32 = jnp.float32
BF16 = jnp.bfloat16

D_MODEL = 2048
GRID_W = 64
EPS = 1e-6

MIX_A = 1024
SGU_CHUNK = 128
A_GROUPS = 8
MIX_B = 1024
B_HEAD = 64
B_HEADS = 16
DECAY_LORA = 64
AAA_LORA = 64
GATE_LORA = 160
B_WIDTH = 3 * MIX_B + DECAY_LORA + AAA_LORA + GATE_LORA
GN_EPS = B_HEAD * 1e-5

C_HEADS = 16
C_HEAD = 64
ROPE_AXIS = C_HEAD // 2
ROPE_THETA = 10000.0
SUBLN_EPS = 1e-5

N_EXPERTS = 8
D_EXPERT = 7168

LOG2E = 1.4426950408889634
LANES = 128
VMEM_LIMIT = 52 * 1024 * 1024
GMM_VMEM_LIMIT = 58 * 1024 * 1024

RW_CHUNK = 64
INV_BLOCK = 16
DMA_UNROLL = 8
RW_TILE = 256
RW_PAIRS = B_HEADS // 2
Z_WIDTH = 3584


def _cparams(sem):
    return pltpu.CompilerParams(dimension_semantics=sem, vmem_limit_bytes=VMEM_LIMIT)


def _group_of_tile(i, tm, n_lat, seq, batch):
    return jnp.where(i * tm < n_lat, (i * tm) // seq, batch)


def _norm_mod_rows(x, gain, mod, s_idx, c_idx):
    ms = jnp.mean(x * x, axis=-1, keepdims=True)
    y = x * lax.rsqrt(ms + EPS) * gain
    return y * (1.0 + mod[c_idx:c_idx + 1, :]) + mod[s_idx:s_idx + 1, :]


def _tile_rows(ref, ctx_ref, lat_tiles):
    if ctx_ref is None:
        return ref[...]
    return jnp.where(pl.program_id(0) < lat_tiles, ref[...], ctx_ref[...])


def _split_row_specs(block, lat_tiles, col):
    lat = pl.BlockSpec(block, lambda i, *a: (jnp.minimum(i, lat_tiles - 1), col(*a)))
    ctx = pl.BlockSpec(block, lambda i, *a: (jnp.maximum(i - lat_tiles, 0), col(*a)))
    return [lat, ctx]


def _norm_mod_kernel(*refs, s_idx, c_idx, lat_tiles):
    if lat_tiles is None:
        x_ref, g_ref, mod_ref, o_ref = refs
        c_ref = None
    else:
        x_ref, c_ref, g_ref, mod_ref, o_ref = refs
    x = _tile_rows(x_ref, c_ref, lat_tiles)
    o_ref[...] = _norm_mod_rows(x, g_ref[...], mod_ref[0], s_idx, c_idx).astype(o_ref.dtype)


def norm_mod(x, gain, mods, s_idx, c_idx, *, n_lat, seq, batch, tm=512, x_ctx=None):
    d = x.shape[1]
    m = x.shape[0] + (0 if x_ctx is None else x_ctx.shape[0])
    grp = functools.partial(_group_of_tile, tm=tm, n_lat=n_lat, seq=seq, batch=batch)
    lat_tiles = None if x_ctx is None else n_lat // tm
    if x_ctx is None:
        x_specs, x_args = [pl.BlockSpec((tm, d), lambda i: (i, 0))], [x]
    else:
        x_specs, x_args = _split_row_specs((tm, d), lat_tiles, lambda: 0), [x, x_ctx]
    return pl.pallas_call(
        functools.partial(_norm_mod_kernel, s_idx=s_idx, c_idx=c_idx, lat_tiles=lat_tiles),
        grid=(m // tm,),
        in_specs=x_specs + [
            pl.BlockSpec((1, d), lambda i: (0, 0)),
            pl.BlockSpec((1, 6, d), lambda i: (grp(i), 0, 0)),
        ],
        out_specs=pl.BlockSpec((tm, d), lambda i: (i, 0)),
        out_shape=jax.ShapeDtypeStruct((m, d), BF16),
        compiler_params=_cparams(("parallel",)),
        name="norm_mod",
    )(*x_args, gain.reshape(1, d), mods)


def _mm_kernel(*refs, kind, nk, silu_x, has_bias, gate_idx, lat_tiles, norm_idx):
    refs = list(refs)
    x_ref = refs.pop(0)
    w_refs = [refs.pop(0)]
    if kind == "swiglu":
        w_refs.append(refs.pop(0))
    bias_ref = refs.pop(0) if has_bias else None
    res_ref = res_ctx_ref = mod_ref = gain_ref = hn_ref = None
    if kind == "res":
        res_ref = refs.pop(0)
        if lat_tiles is not None:
            res_ctx_ref = refs.pop(0)
        mod_ref = refs.pop(0)
        if norm_idx is not None:
            gain_ref = refs.pop(0)
    o_ref = refs.pop(0)
    if norm_idx is not None:
        hn_ref = refs.pop(0)
    acc_refs = refs

    x = x_ref[...]
    if silu_x:
        x = jax.nn.silu(x.astype(F32))
    x = x.astype(BF16)
    prods = [jnp.dot(x, w_ref[...].astype(BF16), preferred_element_type=F32) for w_ref in w_refs]

    def epilogue(vals):
        a = vals[0]
        if kind == "swiglu":
            a = jax.nn.silu(a) * vals[1]
        if has_bias:
            a = a + bias_ref[...]
        if kind == "res":
            a = _tile_rows(res_ref, res_ctx_ref, lat_tiles) + mod_ref[0, gate_idx:gate_idx + 1, :] * a
        o_ref[...] = a.astype(o_ref.dtype)
        if norm_idx is not None:
            hn_ref[...] = _norm_mod_rows(a, gain_ref[...], mod_ref[0], *norm_idx).astype(hn_ref.dtype)

    if nk == 1:
        epilogue(prods)
        return

    k = pl.program_id(2)

    @pl.when(k == 0)
    def _():
        for acc, p in zip(acc_refs, prods):
            acc[...] = p

    @pl.when(k > 0)
    def _():
        for acc, p in zip(acc_refs, prods):
            acc[...] += p

    @pl.when(k == nk - 1)
    def _():
        epilogue([acc[...] for acc in acc_refs])


def matmul(x, w, *, w3=None, bias=None, res=None, res_ctx=None, mods=None, gate_idx=0, out_dtype=F32,
           silu_x=False, tm=1024, tn=512, tk=None, geo=None, norm=None):
    m, kdim = x.shape
    n = w.shape[1]
    tm = min(tm, m)
    tk = kdim if tk is None else tk
    nk = kdim // tk
    kind = "swiglu" if w3 is not None else ("res" if res is not None else "plain")
    in_specs = [pl.BlockSpec((tm, tk), lambda i, j, k: (i, k)),
                pl.BlockSpec((tk, tn), lambda i, j, k: (k, j))]
    args = [x, w]
    if w3 is not None:
        in_specs.append(pl.BlockSpec((tk, tn), lambda i, j, k: (k, j)))
        args.append(w3)
    if bias is not None:
        in_specs.append(pl.BlockSpec((1, tn), lambda i, j, k: (0, j)))
        args.append(bias.reshape(1, n))
    lat_tiles = None
    if res is not None:
        if res_ctx is None:
            in_specs.append(pl.BlockSpec((tm, tn), lambda i, j, k: (i, j)))
            args.append(res)
        else:
            lat_tiles = geo["n_lat"] // tm
            in_specs += _split_row_specs((tm, tn), lat_tiles, lambda j, k: j)
            args += [res, res_ctx]
        group = functools.partial(_group_of_tile, tm=tm, **geo)
        in_specs.append(pl.BlockSpec((1, 6, tn), lambda i, j, k: (group(i), 0, j)))
        args.append(mods)
    out_specs = pl.BlockSpec((tm, tn), lambda i, j, k: (i, j))
    out_shape = jax.ShapeDtypeStruct((m, n), out_dtype)
    norm_idx = None
    if norm is not None:
        assert kind == "res" and tn == n
        in_specs.append(pl.BlockSpec((1, n), lambda i, j, k: (0, 0)))
        args.append(norm[0].reshape(1, n))
        norm_idx = (norm[1], norm[2])
        out_specs = [out_specs, pl.BlockSpec((tm, tn), lambda i, j, k: (i, j))]
        out_shape = [out_shape, jax.ShapeDtypeStruct((m, n), BF16)]
    n_acc = 0 if nk == 1 else (2 if w3 is not None else 1)
    return pl.pallas_call(
        functools.partial(_mm_kernel, kind=kind, nk=nk, silu_x=silu_x, has_bias=bias is not None,
                          gate_idx=gate_idx, lat_tiles=lat_tiles, norm_idx=norm_idx),
        grid=(m // tm, n // tn, nk),
        in_specs=in_specs,
        out_specs=out_specs,
        out_shape=out_shape,
        scratch_shapes=[pltpu.VMEM((tm, tn), F32)] * n_acc,
        compiler_params=_cparams(("parallel", "parallel", "arbitrary")),
        name="matmul_" + kind,
    )(*args)


def _sgu_kernel(u_ref, v_ref, ng_ref, ws_ref, bs_ref, o_ref, *, chunks):
    for c in range(chunks):
        rows = slice(c * SGU_CHUNK, (c + 1) * SGU_CHUNK)
        for g in range(A_GROUPS):
            cols = slice(g * LANES, (g + 1) * LANES)
            vg = jax.nn.gelu(v_ref[rows, cols].astype(F32))
            ms = jnp.mean(vg * vg, axis=-1, keepdims=True)
            vn = vg * lax.rsqrt(ms + EPS) * ng_ref[:, cols]
            s = jnp.dot(ws_ref[g].astype(BF16), vn.astype(BF16), preferred_element_type=F32)
            u = jax.nn.gelu(u_ref[rows, cols].astype(F32))
            o_ref[rows, cols] = (u * (s + bs_ref[:, cols])).astype(o_ref.dtype)


def chunk_sgu(hp, norm_g, ws, bs, *, tm=256):
    m = hp.shape[0]
    bs_exp = jnp.repeat(bs.T, LANES, axis=1)
    return pl.pallas_call(
        functools.partial(_sgu_kernel, chunks=tm // SGU_CHUNK),
        grid=(m // tm,),
        in_specs=[
            pl.BlockSpec((tm, MIX_A), lambda i: (i, 0)),
            pl.BlockSpec((tm, MIX_A), lambda i: (i, 1)),
            pl.BlockSpec((1, MIX_A), lambda i: (0, 0)),
            pl.BlockSpec((A_GROUPS, SGU_CHUNK, SGU_CHUNK), lambda i: (0, 0, 0)),
            pl.BlockSpec((SGU_CHUNK, MIX_A), lambda i: (0, 0)),
        ],
        out_specs=pl.BlockSpec((tm, MIX_A), lambda i: (i, 0)),
        out_shape=jax.ShapeDtypeStruct((m, MIX_A), BF16),
        compiler_params=_cparams(("parallel",)),
        name="chunk_sgu",
    )(hp, hp, norm_g.reshape(1, MIX_A), ws, bs_exp)


def _shift_kernel(x_ref, p_ref, n_ref, w_ref, o_ref, *, ts, lat_tiles, tiles_per_lat, tiles_per_ctx):
    i = pl.program_id(0)
    is_lat = i < lat_tiles
    pos = jnp.where(is_lat, i % tiles_per_lat, (i - lat_tiles) % tiles_per_ctx)
    last = jnp.where(is_lat, tiles_per_lat - 1, tiles_per_ctx - 1)
    x = x_ref[...]
    prev_row = jnp.where(pos == 0, 0.0, p_ref[7:8, :])
    next_row = jnp.where(pos == last, 0.0, n_ref[0:1, :])
    rows = lax.broadcasted_iota(jnp.int32, x.shape, 0)
    xm = jnp.where(rows == 0, prev_row, pltpu.roll(x, 1, axis=0))
    xp = jnp.where(rows == ts - 1, next_row, pltpu.roll(x, ts - 1, axis=0))
    o_ref[...] = (w_ref[0:1, :] * xm + w_ref[1:2, :] * x + w_ref[2:3, :] * xp).astype(o_ref.dtype)


def token_shift(z, shift_w, *, n_lat, seq, ctx_len, ts=256, tc=Z_WIDTH):
    m = z.shape[0]
    halo = 8
    nblk8 = m // halo
    w_pad = jnp.pad(shift_w, ((0, 0), (0, Z_WIDTH - B_WIDTH)))
    kern = functools.partial(_shift_kernel, ts=ts, lat_tiles=n_lat // ts, tiles_per_lat=seq // ts,
                             tiles_per_ctx=ctx_len // ts)
    return pl.pallas_call(
        kern,
        grid=(m // ts, Z_WIDTH // tc),
        in_specs=[
            pl.BlockSpec((ts, tc), lambda i, j: (i, j)),
            pl.BlockSpec((halo, tc), lambda i, j: (jnp.maximum(i * (ts // halo) - 1, 0), j)),
            pl.BlockSpec((halo, tc), lambda i, j: (jnp.minimum((i + 1) * (ts // halo), nblk8 - 1), j)),
            pl.BlockSpec((3, tc), lambda i, j: (0, j)),
        ],
        out_specs=pl.BlockSpec((ts, tc), lambda i, j: (i, j)),
        out_shape=jax.ShapeDtypeStruct((m, Z_WIDTH), BF16),
        compiler_params=_cparams(("parallel", "parallel")),
        name="token_shift",
    )(z, z, z, w_pad)


def _mask_lanes(x, m0):
    z = jnp.zeros_like(x)
    return jnp.concatenate([jnp.where(m0, x, z), jnp.where(m0, z, x)], axis=0)


def _dot(a, b):
    return jnp.dot(a.astype(BF16), b.astype(BF16), preferred_element_type=F32)


def _dot_nt(a, b):
    return lax.dot_general(a.astype(BF16), b.astype(BF16), (((1,), (1,)), ((), ())),
                           preferred_element_type=F32)


def _dot_tn(a, b):
    return lax.dot_general(a.astype(BF16), b.astype(BF16), (((0,), (0,)), ((), ())),
                           preferred_element_type=F32)


def _head_sums(x, m0):
    s0 = jnp.sum(jnp.where(m0, x, 0.0), axis=-1, keepdims=True)
    s1 = jnp.sum(jnp.where(m0, 0.0, x), axis=-1, keepdims=True)
    return jnp.where(m0, s0, s1)


def _rwkv_kernel(*refs, reverse, n_chunks):
    if reverse:
        (zr_ref, zk_ref, zv_ref, zl_ref, w0_ref, wt_ref, a0_ref, wa_ref, kk_ref, ka_ref, rk_ref,
         y0_ref, g_ref, bon0_ref, lnw_ref, lnb_ref, out_ref,
         s_ref, kk_s, bt_s, kd_s, rt_s, v_s, y_s, ge_s) = refs
    else:
        (zr_ref, zk_ref, zv_ref, zl_ref, w0_ref, wt_ref, a0_ref, wa_ref, kk_ref, ka_ref, rk_ref, wg_ref,
         y_out_ref, g_out_ref, bon_out_ref,
         s_ref, kk_s, bt_s, kd_s, rt_s, v_s, y_s, ge_s) = refs
    ch = RW_CHUNK
    j = pl.program_id(1)

    @pl.when(j == 0)
    def _():
        s_ref[...] = jnp.zeros_like(s_ref)

    zl = zl_ref[...].astype(F32)
    lora_in = zl[:, 0:LANES]
    wl = w0_ref[...] + _dot(jnp.tanh(lora_in), wt_ref[...])
    wlog = -(jnp.maximum(-wl, 0.0) + jnp.log(1.0 + jnp.exp(-jnp.abs(wl)))) - 0.5
    lw = -jnp.exp(wlog)
    a = jax.nn.sigmoid(a0_ref[...] + _dot(lora_in, wa_ref[...]))
    r = zr_ref[...].astype(F32)
    k = zk_ref[...].astype(F32)
    v = zv_ref[...].astype(F32)
    kd = k * (1.0 + (a - 1.0) * ka_ref[...])
    kraw = k * kk_ref[...]
    rkd = r * kd * rk_ref[...]
    if not reverse:
        g_out_ref[...] = _dot(jax.nn.sigmoid(zl[:, LANES:3 * LANES]), wg_ref[...])

    lane = lax.broadcasted_iota(jnp.int32, (1, LANES), 1)
    m0 = lane < B_HEAD
    ti = lax.broadcasted_iota(jnp.int32, (ch, ch), 0)
    si = lax.broadcasted_iota(jnp.int32, (ch, ch), 1)
    tri = ((si >= ti) if reverse else (si <= ti)).astype(BF16)

    kn_l, b_l = [], []
    for p in range(RW_PAIRS):
        cols = slice(p * LANES, (p + 1) * LANES)
        kp = kraw[:, cols]
        nrm = jnp.sqrt(_head_sums(kp * kp, m0))
        kn = kp / jnp.maximum(nrm, 1e-12)
        kn_l.append(kn)
        b_l.append(kn * a[:, cols])
        v_s[p] = v[:, cols]
        if not reverse:
            bon_out_ref[:, cols] = _head_sums(rkd[:, cols], m0) * v[:, cols]

    for c in range(n_chunks):
        rws = slice(c * ch, (c + 1) * ch)
        lw_c = lw[rws, :]
        hi = lw_c.astype(BF16)
        rem = lw_c - hi.astype(F32)
        mid = rem.astype(BF16)
        lo = (rem - mid.astype(F32)).astype(BF16)
        cs = (jnp.dot(tri, hi, preferred_element_type=F32) + jnp.dot(tri, mid, preferred_element_type=F32)
              + jnp.dot(tri, lo, preferred_element_type=F32))
        total = cs[0:1, :] if reverse else cs[ch - 1:ch, :]
        g_in = jnp.exp(cs)
        g_ex = jnp.exp(cs - lw_c)
        g_inv = jnp.exp(-cs)
        g_end = jnp.exp(total)
        for p in range(RW_PAIRS):
            cols = slice(p * LANES, (p + 1) * LANES)
            kk_s[p, rws, :] = kn_l[p][rws, :] * g_ex[:, cols]
            bt_s[p, rws, :] = b_l[p][rws, :] * g_inv[:, cols]
            kd_s[p, rws, :] = kd[rws, cols] * g_inv[:, cols]
            rt_s[p, rws, :] = r[rws, cols] * g_in[:, cols]
            ge_s[p, c:c + 1, :] = g_end[:, cols]

    ri = lax.broadcasted_iota(jnp.int32, (2 * ch, 2 * ch), 0)
    ci = lax.broadcasted_iota(jnp.int32, (2 * ch, 2 * ch), 1)
    same_head = (ri // ch) == (ci // ch)
    if reverse:
        strict = same_head & (ci > ri)
        incl = same_head & (ci >= ri)
    else:
        strict = same_head & (ci < ri)
        incl = same_head & (ci <= ri)
    eye = (ri == ci).astype(F32)
    zero = jnp.zeros((2 * ch, 2 * ch), F32)
    n_sq = int(math.log2(INV_BLOCK)) - 1
    blk_diag = []
    bs = INV_BLOCK
    while bs <= ch:
        blk_diag.append((ri // bs) == (ci // bs))
        bs *= 2
    pairs = range(RW_PAIRS)

    def chunk_body(ci_, carry):
        c = (n_chunks - 1 - ci_) if reverse else ci_
        rows = pl.ds(pl.multiple_of(c * ch, ch), ch)
        kk_t = [kk_s[p, rows, :] for p in pairs]
        rt = [rt_s[p, rows, :] for p in pairs]
        bt_m = [_mask_lanes(bt_s[p, rows, :], m0) for p in pairs]
        kd_m = [_mask_lanes(kd_s[p, rows, :], m0) for p in pairs]
        v_st = [_mask_lanes(v_s[p, rows, :], m0) for p in pairs]
        g_end = [ge_s[p, pl.ds(c, 1), :] for p in pairs]

        pq = [_dot_nt(jnp.concatenate([kk_t[p], kk_t[p], rt[p], rt[p]], axis=0),
                      jnp.concatenate([bt_m[p], kd_m[p]], axis=0)) for p in pairs]
        a_m = [jnp.where(strict, pq[p][0:2 * ch, 0:2 * ch], zero) for p in pairs]
        b_m = [jnp.where(strict, pq[p][0:2 * ch, 2 * ch:4 * ch], zero) for p in pairs]
        m2 = [jnp.where(incl, pq[p][2 * ch:4 * ch, 0:2 * ch], zero) for p in pairs]
        m1 = [jnp.where(incl, pq[p][2 * ch:4 * ch, 2 * ch:4 * ch], zero) for p in pairs]

        q = [jnp.where(blk_diag[0], -a_m[p], zero) for p in pairs]
        t_m = [eye + q[p] for p in pairs]
        q = [_dot(q[p], q[p]) for p in pairs]
        for _ in range(n_sq - 1):
            both = [_dot(jnp.concatenate([q[p], t_m[p]], axis=0), q[p]) for p in pairs]
            q = [both[p][0:2 * ch] for p in pairs]
            t_m = [t_m[p] + both[p][2 * ch:4 * ch] for p in pairs]
        t_m = [t_m[p] + _dot(t_m[p], q[p]) for p in pairs]
        for lvl in range(1, len(blk_diag)):
            off = [jnp.where(blk_diag[lvl] & ~blk_diag[lvl - 1], a_m[p], zero) for p in pairs]
            t_off = [_dot(t_m[p], off[p]) for p in pairs]
            t_m = [t_m[p] - _dot(t_off[p], t_m[p]) for p in pairs]

        bm1v = [_dot(jnp.concatenate([b_m[p], m1[p]], axis=0), v_st[p]) for p in pairs]
        ku = [_dot(t_m[p], jnp.concatenate([_mask_lanes(kk_t[p], m0), bm1v[p][0:2 * ch]], axis=1))
              for p in pairs]
        m2ku = [_dot(m2[p], ku[p]) for p in pairs]
        s_bd = [s_ref[p] for p in pairs]
        rp, y0 = [], []
        for p in pairs:
            rp_s = _mask_lanes(rt[p], m0) - m2ku[p][:, 0:LANES]
            y0_s = bm1v[p][2 * ch:4 * ch] - m2ku[p][:, LANES:2 * LANES]
            rp.append(rp_s[0:ch] + rp_s[ch:2 * ch])
            y0.append(y0_s[0:ch] + y0_s[ch:2 * ch])
        rs = [_dot_nt(jnp.concatenate([rp[p], ku[p][:, 0:LANES]], axis=0), s_bd[p]) for p in pairs]
        ys = [rs[p][0:ch] + y0[p] for p in pairs]
        u_all = [rs[p][ch:3 * ch] + ku[p][:, LANES:2 * LANES] for p in pairs]
        for p in pairs:
            upd = _dot_tn(jnp.concatenate([v_st[p], -u_all[p]], axis=0),
                          jnp.concatenate([kd_m[p] * g_end[p], bt_m[p] * g_end[p]], axis=0))
            s_ref[p] = s_bd[p] * g_end[p] + upd
            y_s[p, rows, :] = ys[p]
        return carry

    lax.fori_loop(0, n_chunks, chunk_body, 0)

    for p in range(RW_PAIRS):
        cols = slice(p * LANES, (p + 1) * LANES)
        if not reverse:
            y_out_ref[:, cols] = y_s[p]
        else:
            bon1 = _head_sums(rkd[:, cols], m0) * v[:, cols]
            ysum = y0_ref[:, cols] + y_s[p]
            mu = _head_sums(ysum, m0) * (1.0 / B_HEAD)
            dlt = ysum - mu
            var = _head_sums(dlt * dlt, m0) * (1.0 / B_HEAD)
            yn = dlt * lax.rsqrt(var + GN_EPS)
            o = (yn * lnw_ref[:, cols] + lnb_ref[:, cols] + bon0_ref[:, cols] + bon1) * g_ref[:, cols]
            out_ref[:, cols] = o.astype(out_ref.dtype)


def rwkv7_bidir(zs, w0, w_up, a0, a_up, g_up, k_k, k_a, r_k, lnx_w, lnx_b, *, batch, seq, ctx_len):
    m = zs.shape[0]
    n_lat = batch * seq
    tb = RW_TILE
    nct, nlt = ctx_len // tb, seq // tb
    steps = nct + nlt

    def row_block(reverse):
        def f(b, j):
            jc = (nct - 1 - j) if reverse else j
            jl = (nlt - 1 - (j - nct)) if reverse else (j - nct)
            return jnp.where(j < nct, (n_lat + b * ctx_len) // tb + jc, b * nlt + jl)
        return f

    def pad_rows(w, top, total):
        return jnp.pad(w, ((top, total - top - w.shape[0]), (0, 0)))

    wg = pad_rows(g_up, 0, 2 * LANES)
    rk = r_k.reshape(1, MIX_B)
    vec = lambda a: a.reshape(1, MIX_B)
    scratch = ([pltpu.VMEM((RW_PAIRS, LANES, LANES), F32)] + [pltpu.VMEM((RW_PAIRS, tb, LANES), F32)] * 6
               + [pltpu.VMEM((RW_PAIRS, 8, LANES), F32)])
    const = lambda shape: pl.BlockSpec(shape, lambda b, j: (0,) * len(shape))

    def call(reverse, extra_in, extra_specs, out_shapes, out_specs, d):
        rb = row_block(reverse)
        wt = pad_rows(w_up[d], 0, LANES)
        wa = pad_rows(a_up[d], DECAY_LORA, LANES)
        in_specs = [
            pl.BlockSpec((tb, MIX_B), lambda b, j: (rb(b, j), 0)),
            pl.BlockSpec((tb, MIX_B), lambda b, j: (rb(b, j), 1)),
            pl.BlockSpec((tb, MIX_B), lambda b, j: (rb(b, j), 2)),
            pl.BlockSpec((tb, 512), lambda b, j: (rb(b, j), 6)),
            const((1, MIX_B)), const((LANES, MIX_B)), const((1, MIX_B)), const((LANES, MIX_B)),
            const((1, MIX_B)), const((1, MIX_B)), const((1, MIX_B)),
        ] + extra_specs
        args = [zs, zs, zs, zs, vec(w0[d]), wt, vec(a0[d]), wa, vec(k_k), vec(k_a), rk] + extra_in
        return pl.pallas_call(
            functools.partial(_rwkv_kernel, reverse=reverse, n_chunks=tb // RW_CHUNK),
            grid=(batch, steps),
            in_specs=in_specs,
            out_specs=out_specs,
            out_shape=out_shapes,
            scratch_shapes=scratch,
            compiler_params=_cparams(("parallel", "arbitrary")),
            name="rwkv7_rev" if reverse else "rwkv7_fwd",
        )(*args)

    rbf = row_block(False)
    tile_f = pl.BlockSpec((tb, MIX_B), lambda b, j: (rbf(b, j), 0))
    y0, g, bon0 = call(False, [wg], [const((2 * LANES, MIX_B))],
                       [jax.ShapeDtypeStruct((m, MIX_B), F32)] * 3, [tile_f] * 3, 0)
    rbr = row_block(True)
    tile_r = pl.BlockSpec((tb, MIX_B), lambda b, j: (rbr(b, j), 0))
    out = call(True, [y0, g, bon0, vec(lnx_w), vec(lnx_b)],
               [tile_r, tile_r, tile_r, const((1, MIX_B)), const((1, MIX_B))],
               jax.ShapeDtypeStruct((m, MIX_B), BF16), tile_r, 1)
    return out


def _half_rms(x, gain, m0):
    ms = _head_sums(x * x, m0) * (1.0 / C_HEAD)
    return x * lax.rsqrt(ms + EPS) * gain


def _rope(x, cos, sin):
    lane = lax.broadcasted_iota(jnp.int32, x.shape, 1)
    first = (lane % ROPE_AXIS) < (ROPE_AXIS // 2)
    half = ROPE_AXIS // 2
    rot = jnp.where(first, -pltpu.roll(x, LANES - half, axis=1), pltpu.roll(x, half, axis=1))
    return x * cos + rot * sin


def _attn_kernel(q_ref, kl_ref, vl_ref, kc_ref, vc_ref, cq_ref, sq_ref, ck_ref, sk_ref, qg_ref, kg_ref,
                 lam_ref, sg_ref, o_ref, k_s, v_s, *, ctx_len, lam_init, q_blk):
    qi = pl.program_id(2)
    lane = lax.broadcasted_iota(jnp.int32, (1, LANES), 1)
    m0 = lane < C_HEAD

    @pl.when(qi == 0)
    def _():
        kc = _half_rms(kc_ref[...].astype(F32), kg_ref[...], m0)
        kl = _rope(_half_rms(kl_ref[...].astype(F32), kg_ref[...], m0), ck_ref[...], sk_ref[...])
        k_s[0:ctx_len, :] = kc.astype(BF16)
        k_s[ctx_len:, :] = kl.astype(BF16)
        v_s[0:ctx_len, :] = vc_ref[...].astype(BF16)
        v_s[ctx_len:, :] = vl_ref[...].astype(BF16)

    lp = lam_ref[...]
    lam = (jnp.exp(jnp.sum(lp[0:1] * lp[1:2], keepdims=True)) - jnp.exp(jnp.sum(lp[2:3] * lp[3:4], keepdims=True))
           + lam_init)
    q = (_rope(_half_rms(q_ref[...].astype(F32), qg_ref[...], m0), cq_ref[...], sq_ref[...])
         * (C_HEAD ** -0.5 * LOG2E))
    keys = k_s[...]
    vals = v_s[...]
    zq = jnp.zeros_like(q)
    q_sub = [jnp.where(m0, q, zq), jnp.where(m0, zq, q)]
    blocks = [(slice(r0, r0 + q_blk), i) for r0 in range(0, q.shape[0], q_blk) for i in (0, 1)]
    lookahead = 2
    scores = [_dot_nt(q_sub[i][rows], keys) for rows, i in blocks[:lookahead]]
    outs = []
    for n, (rows, i) in enumerate(blocks):
        s = scores[n]
        e = jnp.exp2(s - jnp.max(s, axis=-1, keepdims=True))
        w = (lam if i else 1.0) / jnp.sum(e, axis=-1, keepdims=True)
        if n + lookahead < len(blocks):
            rows2, i2 = blocks[n + lookahead]
            scores.append(_dot_nt(q_sub[i2][rows2], keys))
        outs.append(jnp.dot(e.astype(BF16), vals, preferred_element_type=F32) * w)
    for n, r0 in enumerate(range(0, q.shape[0], q_blk)):
        o = outs[2 * n] - outs[2 * n + 1]
        ms = jnp.mean(o * o, axis=-1, keepdims=True)
        o = o * lax.rsqrt(ms + SUBLN_EPS) * sg_ref[...] * (1.0 - lam_init)
        o_ref[r0:r0 + q_blk, :] = o.astype(o_ref.dtype)


def _rope_tables(n):
    rows = n // GRID_W
    row = jnp.broadcast_to(jnp.arange(rows, dtype=F32)[:, None], (rows, GRID_W)).reshape(-1)
    col = jnp.broadcast_to(jnp.arange(GRID_W, dtype=F32)[None, :], (rows, GRID_W)).reshape(-1)
    inv = ROPE_THETA ** (-jnp.arange(0, ROPE_AXIS, 2, dtype=F32) / ROPE_AXIS)
    ar = row[:, None] * inv
    ac = col[:, None] * inv
    ang = jnp.concatenate([ar, ar, ac, ac, ar, ar, ac, ac], axis=-1)
    return jnp.cos(ang), jnp.sin(ang)


def diff_attention(qkv, q_g, k_g, lam_params, subln_g, lam_init, *, batch, seq, ctx_len, tq=1024):
    n_lat = batch * seq
    tq = min(tq, seq)
    cos, sin = _rope_tables(seq)
    nq = seq // tq
    hq, hk, hv = 0, D_MODEL // LANES, 2 * D_MODEL // LANES
    ctx_blk0 = n_lat // ctx_len
    two = lambda a: jnp.concatenate([a, a]).reshape(1, LANES)
    const = lambda shape: pl.BlockSpec(shape, lambda b, h, i: (0,) * len(shape))
    return pl.pallas_call(
        functools.partial(_attn_kernel, ctx_len=ctx_len, lam_init=lam_init, q_blk=min(128, tq)),
        grid=(batch, C_HEADS, nq),
        in_specs=[
            pl.BlockSpec((tq, LANES), lambda b, h, i: (b * nq + i, hq + h)),
            pl.BlockSpec((seq, LANES), lambda b, h, i: (b, hk + h)),
            pl.BlockSpec((seq, LANES), lambda b, h, i: (b, hv + h)),
            pl.BlockSpec((ctx_len, LANES), lambda b, h, i: (ctx_blk0 + b, hk + h)),
            pl.BlockSpec((ctx_len, LANES), lambda b, h, i: (ctx_blk0 + b, hv + h)),
            pl.BlockSpec((tq, LANES), lambda b, h, i: (i, 0)),
            pl.BlockSpec((tq, LANES), lambda b, h, i: (i, 0)),
            const((seq, LANES)), const((seq, LANES)),
            const((1, LANES)), const((1, LANES)), const((4, C_HEAD)), const((1, LANES)),
        ],
        out_specs=pl.BlockSpec((tq, LANES), lambda b, h, i: (b * nq + i, h)),
        out_shape=jax.ShapeDtypeStruct((n_lat, D_MODEL), BF16),
        scratch_shapes=[pltpu.VMEM((ctx_len + seq, LANES), BF16)] * 2,
        compiler_params=_cparams(("parallel", "parallel", "arbitrary")),
        name="diff_attention",
    )(qkv, qkv, qkv, qkv, qkv, cos, sin, cos, sin, two(q_g), two(k_g), lam_params, subln_g.reshape(1, LANES))


def _route_kernel(x_ref, g_ref, mod_ref, rw_ref, xn_ref, idx_ref, gate_ref, *, s_idx, c_idx):
    h = _norm_mod_rows(x_ref[...], g_ref[...], mod_ref[0], s_idx, c_idx)
    xn_ref[...] = h
    logits = jnp.dot(h, rw_ref[...], precision=lax.Precision.HIGHEST, preferred_element_type=F32)
    lane = lax.broadcasted_iota(jnp.int32, logits.shape, 1)
    neg = jnp.float32(-jnp.inf)
    lg = jnp.where(lane < N_EXPERTS, logits, neg)
    m1 = jnp.max(lg, axis=-1, keepdims=True)
    i1 = jnp.min(jnp.where(lg == m1, lane, LANES), axis=-1, keepdims=True)
    lg2 = jnp.where(lane == i1, neg, lg)
    m2 = jnp.max(lg2, axis=-1, keepdims=True)
    i2 = jnp.min(jnp.where(lg2 == m2, lane, LANES), axis=-1, keepdims=True)
    e2 = jnp.exp(m2 - m1)
    g1 = 1.0 / (1.0 + e2)
    g2 = e2 * g1
    idx_ref[...] = jnp.where(lane == 0, i1, jnp.where(lane == 1, i2, 0))
    gate_ref[...] = jnp.where(lane == 0, g1, jnp.where(lane == 1, g2, 0.0))


def route(x, gain, mods, router, s_idx, c_idx, *, seq, batch, tm=256):
    m, d = x.shape
    rw = jnp.pad(router, ((0, 0), (0, LANES - N_EXPERTS)))
    return pl.pallas_call(
        functools.partial(_route_kernel, s_idx=s_idx, c_idx=c_idx),
        grid=(m // tm,),
        in_specs=[
            pl.BlockSpec((tm, d), lambda i: (i, 0)),
            pl.BlockSpec((1, d), lambda i: (0, 0)),
            pl.BlockSpec((1, 6, d), lambda i: ((i * tm) // seq, 0, 0)),
            pl.BlockSpec((d, LANES), lambda i: (0, 0)),
        ],
        out_specs=[pl.BlockSpec((tm, d), lambda i: (i, 0)),
                   pl.BlockSpec((tm, LANES), lambda i: (i, 0)),
                   pl.BlockSpec((tm, LANES), lambda i: (i, 0))],
        out_shape=[jax.ShapeDtypeStruct((m, d), F32),
                   jax.ShapeDtypeStruct((m, LANES), jnp.int32),
                   jax.ShapeDtypeStruct((m, LANES), F32)],
        compiler_params=_cparams(("parallel",)),
        name="moe_route",
    )(x, gain.reshape(1, d), mods, rw)


def _row_copy(src_hbm, dst_ref, src_row, dst_row, sem):
    return pltpu.make_async_copy(src_hbm.at[pl.ds(src_row, 1)], dst_ref.at[pl.ds(dst_row, 1)], sem)


def _gather_kernel(used_ref, idx_ref, nxt_ref, src_hbm, o_ref, buf, sem, *, tg, n_tiles):
    i = pl.program_id(0)
    slot = i % 2

    def issue_tile(ids_ref, s):
        def issue(g, c):
            for u in range(DMA_UNROLL):
                r = g * DMA_UNROLL + u
                _row_copy(src_hbm, buf.at[s], ids_ref[0, 0, r], r, sem.at[s]).start(priority=u % 2)
            return c

        lax.fori_loop(0, tg // DMA_UNROLL, issue, 0)

    @pl.when((i == 0) & (used_ref[0] > 0))
    def _():
        issue_tile(idx_ref, slot)

    @pl.when((i + 1 < n_tiles) & ((i + 1) * tg < used_ref[0]))
    def _():
        issue_tile(nxt_ref, 1 - slot)

    @pl.when(i * tg >= used_ref[0])
    def _():
        o_ref[...] = jnp.zeros_like(o_ref)

    @pl.when(i * tg < used_ref[0])
    def _():
        def drain(r, c):
            _row_copy(src_hbm, buf.at[slot], 0, r, sem.at[slot]).wait()
            return c

        lax.fori_loop(0, tg, drain, 0, unroll=DMA_UNROLL)
        o_ref[...] = buf[slot].astype(o_ref.dtype)


def gather_rows(src, idx, rows_used, *, tg=256):
    mp = idx.shape[0]
    d = src.shape[1]
    n_tiles = mp // tg
    idx3 = idx.reshape(n_tiles, 1, tg)
    return pl.pallas_call(
        functools.partial(_gather_kernel, tg=tg, n_tiles=n_tiles),
        grid_spec=pltpu.PrefetchScalarGridSpec(
            num_scalar_prefetch=1,
            grid=(n_tiles,),
            in_specs=[pl.BlockSpec((1, 1, tg), lambda i, u: (i, 0, 0), memory_space=pltpu.SMEM),
                      pl.BlockSpec((1, 1, tg), lambda i, u: (jnp.minimum(i + 1, n_tiles - 1), 0, 0),
                                   memory_space=pltpu.SMEM),
                      pl.BlockSpec(memory_space=pl.ANY)],
            out_specs=pl.BlockSpec((tg, d), lambda i, u: (i, 0)),
            scratch_shapes=[pltpu.VMEM((2, tg, d), src.dtype), pltpu.SemaphoreType.DMA((2,))],
        ),
        out_shape=jax.ShapeDtypeStruct((mp, d), BF16),
        compiler_params=_cparams(("arbitrary",)),
        name="moe_gather",
    )(rows_used, idx3, idx3, src)


def _gmm_kernel(te_ref, nu_ref, *refs, kind, kc):
    if kind == "swiglu":
        x_ref, w1_ref, w3_ref, o_ref = refs
        w_refs = (w1_ref, w3_ref)
    else:
        x_ref, w1_ref, o_ref = refs
        w_refs = (w1_ref,)
    i = pl.program_id(1)

    @pl.when(i >= nu_ref[0])
    def _():
        o_ref[...] = jnp.zeros_like(o_ref)

    @pl.when(i < nu_ref[0])
    def _():
        kdim = x_ref.shape[1]
        accs = [None] * len(w_refs)
        for k0 in range(0, kdim, kc):
            x = x_ref[:, k0:k0 + kc].astype(BF16)
            for n, w_ref in enumerate(w_refs):
                p = jnp.dot(x, w_ref[0, k0:k0 + kc, :].astype(BF16), preferred_element_type=F32)
                accs[n] = p if accs[n] is None else accs[n] + p
        a = accs[0]
        if kind == "swiglu":
            a = jax.nn.silu(a) * accs[1]
        o_ref[...] = a.astype(o_ref.dtype)


def grouped_matmul(x, w, tile_expert, n_used, *, w3=None, out_dtype=F32, tm=512, tn=512, kc=1024):
    mp, kdim = x.shape
    n = w.shape[2]
    kind = "swiglu" if w3 is not None else "plain"
    w_spec = pl.BlockSpec((1, kdim, tn), lambda j, i, te, nu: (te[i], 0, j))
    in_specs = [pl.BlockSpec((tm, kdim), lambda j, i, te, nu: (i, 0)), w_spec]
    args = [x, w]
    if w3 is not None:
        in_specs.append(w_spec)
        args.append(w3)
    return pl.pallas_call(
        functools.partial(_gmm_kernel, kind=kind, kc=min(kc, kdim)),
        grid_spec=pltpu.PrefetchScalarGridSpec(
            num_scalar_prefetch=2,
            grid=(n // tn, mp // tm),
            in_specs=in_specs,
            out_specs=pl.BlockSpec((tm, tn), lambda j, i, te, nu: (i, j)),
        ),
        out_shape=jax.ShapeDtypeStruct((mp, n), out_dtype),
        compiler_params=pltpu.CompilerParams(dimension_semantics=("arbitrary", "arbitrary"),
                                             vmem_limit_bytes=GMM_VMEM_LIMIT),
        name="moe_gmm_" + kind,
    )(tile_expert, n_used, *args)


def _combine_kernel(pos_ref, nxt_ref, ys_hbm, x_ref, gate_ref, mod_ref, o_ref, buf, sem, *, tc, n_tiles, gate_idx):
    i = pl.program_id(0)
    slot = i % 2

    def issue_tile(p_ref, s):
        def issue(g, c):
            for u in range(DMA_UNROLL):
                r = g * DMA_UNROLL + u
                _row_copy(ys_hbm, buf.at[s, 0], p_ref[0, 0, 2 * r], r, sem.at[s]).start(priority=0)
                _row_copy(ys_hbm, buf.at[s, 1], p_ref[0, 0, 2 * r + 1], r, sem.at[s]).start(priority=1)
            return c

        lax.fori_loop(0, tc // DMA_UNROLL, issue, 0)

    @pl.when(i == 0)
    def _():
        issue_tile(pos_ref, slot)

    @pl.when(i + 1 < n_tiles)
    def _():
        issue_tile(nxt_ref, 1 - slot)

    def drain(r, c):
        _row_copy(ys_hbm, buf.at[slot, 0], 0, r, sem.at[slot]).wait()
        _row_copy(ys_hbm, buf.at[slot, 1], 0, r, sem.at[slot]).wait()
        return c

    lax.fori_loop(0, tc, drain, 0, unroll=DMA_UNROLL)
    g = gate_ref[...]
    moe = g[:, 0:1] * buf[slot, 0] + g[:, 1:2] * buf[slot, 1]
    o_ref[...] = x_ref[...] + mod_ref[0, gate_idx:gate_idx + 1, :] * moe


def moe_combine(ys, pos, x, gates, mods, gate_idx, *, seq, tc=256):
    m, d = x.shape
    n_tiles = m // tc
    pos3 = pos.reshape(n_tiles, 1, 2 * tc)
    return pl.pallas_call(
        functools.partial(_combine_kernel, tc=tc, n_tiles=n_tiles, gate_idx=gate_idx),
        grid=(n_tiles,),
        in_specs=[pl.BlockSpec((1, 1, 2 * tc), lambda i: (i, 0, 0), memory_space=pltpu.SMEM),
                  pl.BlockSpec((1, 1, 2 * tc), lambda i: (jnp.minimum(i + 1, n_tiles - 1), 0, 0),
                               memory_space=pltpu.SMEM),
                  pl.BlockSpec(memory_space=pl.ANY),
                  pl.BlockSpec((tc, d), lambda i: (i, 0)),
                  pl.BlockSpec((tc, LANES), lambda i: (i, 0)),
                  pl.BlockSpec((1, 6, d), lambda i: ((i * tc) // seq, 0, 0))],
        out_specs=pl.BlockSpec((tc, d), lambda i: (i, 0)),
        out_shape=jax.ShapeDtypeStruct((m, d), F32),
        scratch_shapes=[pltpu.VMEM((2, 2, tc, d), F32), pltpu.SemaphoreType.DMA((2,))],
        compiler_params=_cparams(("arbitrary",)),
        name="moe_combine",
    )(pos3, pos3, ys, x, gates, mods)


def moe_layer(x, gain, mods, router, w1, w3, w2, *, seq, batch, tm=512):
    n = x.shape[0]
    xn, idx, gates = route(x, gain, mods, router, 3, 4, seq=seq, batch=batch)
    e_flat = idx[:, 0:2].reshape(-1)
    onehot = (e_flat[:, None] == jnp.arange(N_EXPERTS)[None, :]).astype(jnp.int32)
    ranks = jnp.cumsum(onehot, axis=0) - onehot
    rank = jnp.sum(ranks * onehot, axis=1)
    counts = jnp.sum(onehot, axis=0)
    padded = ((counts + tm - 1) // tm) * tm
    starts = jnp.cumsum(padded) - padded
    pos = starts[e_flat] + rank
    mp = 2 * n + N_EXPERTS * tm
    token_of_row = jnp.zeros((mp,), jnp.int32).at[pos].set(jnp.arange(2 * n, dtype=jnp.int32) // 2)
    n_tiles = mp // tm
    ends = jnp.cumsum(padded)
    tile_start = jnp.arange(n_tiles, dtype=jnp.int32) * tm
    tile_expert = jnp.minimum(jnp.sum((tile_start[:, None] >= ends[None, :]).astype(jnp.int32), axis=1),
                              N_EXPERTS - 1).astype(jnp.int32)
    n_used = (ends[-1] // tm).astype(jnp.int32).reshape(1)

    xs = gather_rows(xn, token_of_row, ends[-1].astype(jnp.int32).reshape(1))
    hs = grouped_matmul(xs, w1, tile_expert, n_used, w3=w3, out_dtype=BF16, tm=tm, tn=1024)
    ys = grouped_matmul(hs, w2, tile_expert, n_used, out_dtype=F32, tm=tm, tn=512)
    return moe_combine(ys, pos.astype(jnp.int32), x, gates, mods, 5, seq=seq)


def kernel(x, c, ctx, c_ctx, l0_ada_w, l0_ada_b, l0_norm1_g, l0_norm2_g, l0_w_in, l0_sgu_norm_g, l0_sgu_w, l0_sgu_b, l0_shift_w, l0_w0, l0_w_up, l0_a0, l0_a_up, l0_g_up, l0_k_k, l0_k_a, l0_r_k, l0_lnx_w, l0_lnx_b, l0_w_out, l0_ffn_w1, l0_ffn_w3, l0_ffn_w2, l1_ada_w, l1_ada_b, l1_norm1_g, l1_norm2_g, l1_w_qkv, l1_q_norm_g, l1_k_norm_g, l1_lam_q1, l1_lam_k1, l1_lam_q2, l1_lam_k2, l1_subln_g, l1_w_out, l1_router, l1_exp_w1, l1_exp_w3, l1_exp_w2):
    batch, seq, d = x.shape
    ctx_len = ctx.shape[1]
    n_lat = batch * seq
    tm = min(1024, seq, batch * ctx_len)
    tm_res = min(256, tm)
    geo = dict(n_lat=n_lat, seq=seq, batch=batch)
    bf = lambda w: w.astype(BF16)

    x_lat = x.reshape(n_lat, d)
    x_ctx = ctx.reshape(batch * ctx_len, d)
    cond = jnp.concatenate([c, c_ctx[None, :], jnp.zeros((16 - batch - 1, d), F32)], axis=0)

    def ada(w, b):
        return matmul(cond, w, bias=b, silu_x=True, tn=1024).reshape(16, 6, d)

    mods = ada(l0_ada_w, l0_ada_b)
    hn = norm_mod(x_lat, l0_norm1_g, mods, 0, 1, tm=min(512, tm), x_ctx=x_ctx, **geo)
    w_z = bf(jnp.pad(l0_w_in[:, 2 * MIX_A:], ((0, 0), (0, Z_WIDTH - B_WIDTH))))
    hp_a = matmul(hn, bf(l0_w_in[:, :2 * MIX_A]), out_dtype=BF16, tm=tm)
    hp_z = matmul(hn, w_z, tm=tm)
    a_out = chunk_sgu(hp_a, l0_sgu_norm_g, l0_sgu_w, l0_sgu_b)
    zs = token_shift(hp_z, l0_shift_w, n_lat=n_lat, seq=seq, ctx_len=ctx_len)
    b_out = rwkv7_bidir(zs, l0_w0, l0_w_up, l0_a0, l0_a_up, l0_g_up, l0_k_k, l0_k_a, l0_r_k, l0_lnx_w, l0_lnx_b,
                        batch=batch, seq=seq, ctx_len=ctx_len)
    mixed = jnp.concatenate([a_out, b_out], axis=1)
    x1, hn = matmul(mixed, bf(l0_w_out), res=x_lat, res_ctx=x_ctx, mods=mods, gate_idx=2, geo=geo, tm=tm_res,
                    tn=d, norm=(l0_norm2_g, 3, 4))
    hff = matmul(hn, bf(l0_ffn_w1), w3=bf(l0_ffn_w3), out_dtype=BF16, tm=tm)
    x2 = matmul(hff, bf(l0_ffn_w2), res=x1, mods=mods, gate_idx=5, geo=geo, tm=tm, tn=1024,
                tk=l0_ffn_w2.shape[0] // 2)

    mods = ada(l1_ada_w, l1_ada_b)
    hn = norm_mod(x2, l1_norm1_g, mods, 0, 1, tm=min(512, tm), **geo)
    qkv = matmul(hn, bf(l1_w_qkv), out_dtype=BF16, tm=tm)
    lam_params = jnp.stack([l1_lam_q1, l1_lam_k1, l1_lam_q2, l1_lam_k2])
    lam_init = 0.8 - 0.6 * math.exp(-0.3 * 1)
    o = diff_attention(qkv, l1_q_norm_g, l1_k_norm_g, lam_params, l1_subln_g, lam_init,
                       batch=batch, seq=seq, ctx_len=ctx_len)
    x3 = matmul(o, bf(l1_w_out), res=x2, mods=mods, gate_idx=2, geo=geo, tm=tm_res, tn=d)
    x4 = moe_layer(x3, l1_norm2_g, mods, l1_router, l1_exp_w1, l1_exp_w3, l1_exp_w2, seq=seq, batch=batch)
    return x4.reshape(batch, seq, d)
```

```python
import functools
import math

import jax
import jax.numpy as jnp
from jax import lax
from jax.experimental import pallas as pl
from jax.experimental.pallas import tpu as pltpu

F32 = jnp.float32
BF16 = jnp.bfloat16

D_MODEL = 2048
GRID_W = 64
EPS = 1e-6

MIX_A = 1024
SGU_CHUNK = 128
A_GROUPS = 8
MIX_B = 1024
B_HEAD = 64
B_HEADS = 16
DECAY_LORA = 64
AAA_LORA = 64
GATE_LORA = 160
B_WIDTH = 3 * MIX_B + DECAY_LORA + AAA_LORA + GATE_LORA
GN_EPS = B_HEAD * 1e-5

C_HEADS = 16
C_HEAD = 64
ROPE_AXIS = C_HEAD // 2
ROPE_THETA = 10000.0
SUBLN_EPS = 1e-5

N_EXPERTS = 8
D_EXPERT = 7168

LOG2E = 1.4426950408889634
LANES = 128
VMEM_LIMIT = 52 * 1024 * 1024
GMM_VMEM_LIMIT = 58 * 1024 * 1024

RW_CHUNK = 64
INV_BLOCK = 16
DMA_UNROLL = 8
RW_TILE = 256
RW_PAIRS = B_HEADS // 2
Z_WIDTH = 3584


def _cparams(sem):
    return pltpu.CompilerParams(dimension_semantics=sem, vmem_limit_bytes=VMEM_LIMIT)


def _group_of_tile(i, tm, n_lat, seq, batch):
    return jnp.where(i * tm < n_lat, (i * tm) // seq, batch)


def _norm_mod_rows(x, gain, mod, s_idx, c_idx):
    ms = jnp.mean(x * x, axis=-1, keepdims=True)
    y = x * lax.rsqrt(ms + EPS) * gain
    return y * (1.0 + mod[c_idx:c_idx + 1, :]) + mod[s_idx:s_idx + 1, :]


def _tile_rows(ref, ctx_ref, lat_tiles):
    if ctx_ref is None:
        return ref[...]
    return jnp.where(pl.program_id(0) < lat_tiles, ref[...], ctx_ref[...])


def _split_row_specs(block, lat_tiles, col):
    lat = pl.BlockSpec(block, lambda i, *a: (jnp.minimum(i, lat_tiles - 1), col(*a)))
    ctx = pl.BlockSpec(block, lambda i, *a: (jnp.maximum(i - lat_tiles, 0), col(*a)))
    return [lat, ctx]


def _norm_mod_kernel(*refs, s_idx, c_idx, lat_tiles):
    if lat_tiles is None:
        x_ref, g_ref, mod_ref, o_ref = refs
        c_ref = None
    else:
        x_ref, c_ref, g_ref, mod_ref, o_ref = refs
    x = _tile_rows(x_ref, c_ref, lat_tiles)
    o_ref[...] = _norm_mod_rows(x, g_ref[...], mod_ref[0], s_idx, c_idx).astype(o_ref.dtype)


def norm_mod(x, gain, mods, s_idx, c_idx, *, n_lat, seq, batch, tm=512, x_ctx=None):
    d = x.shape[1]
    m = x.shape[0] + (0 if x_ctx is None else x_ctx.shape[0])
    grp = functools.partial(_group_of_tile, tm=tm, n_lat=n_lat, seq=seq, batch=batch)
    lat_tiles = None if x_ctx is None else n_lat // tm
    if x_ctx is None:
        x_specs, x_args = [pl.BlockSpec((tm, d), lambda i: (i, 0))], [x]
    else:
        x_specs, x_args = _split_row_specs((tm, d), lat_tiles, lambda: 0), [x, x_ctx]
    return pl.pallas_call(
        functools.partial(_norm_mod_kernel, s_idx=s_idx, c_idx=c_idx, lat_tiles=lat_tiles),
        grid=(m // tm,),
        in_specs=x_specs + [
            pl.BlockSpec((1, d), lambda i: (0, 0)),
            pl.BlockSpec((1, 6, d), lambda i: (grp(i), 0, 0)),
        ],
        out_specs=pl.BlockSpec((tm, d), lambda i: (i, 0)),
        out_shape=jax.ShapeDtypeStruct((m, d), BF16),
        compiler_params=_cparams(("parallel",)),
        name="norm_mod",
    )(*x_args, gain.reshape(1, d), mods)


def _mm_kernel(*refs, kind, nk, silu_x, has_bias, gate_idx, lat_tiles, norm_idx, has_router):
    refs = list(refs)
    x_ref = refs.pop(0)
    w_refs = [refs.pop(0)]
    if kind == "swiglu":
        w_refs.append(refs.pop(0))
    bias_ref = refs.pop(0) if has_bias else None
    res_ref = res_ctx_ref = mod_ref = gain_ref = hn_ref = None
    if kind == "res":
        res_ref = refs.pop(0)
        if lat_tiles is not None:
            res_ctx_ref = refs.pop(0)
        mod_ref = refs.pop(0)
        if norm_idx is not None:
            gain_ref = refs.pop(0)
        if has_router:
            rw_ref = refs.pop(0)
    o_ref = refs.pop(0)
    if norm_idx is not None:
        hn_ref = refs.pop(0)
    if has_router:
        idx_ref = refs.pop(0)
        gates_ref = refs.pop(0)
    acc_refs = refs

    x = x_ref[...]
    if silu_x:
        x = jax.nn.silu(x.astype(F32))
    x = x.astype(BF16)
    prods = [jnp.dot(x, w_ref[...].astype(BF16), preferred_element_type=F32) for w_ref in w_refs]

    def epilogue(vals):
        a = vals[0]
        if kind == "swiglu":
            a = jax.nn.silu(a) * vals[1]
        if has_bias:
            a = a + bias_ref[...]
        if kind == "res":
            a = _tile_rows(res_ref, res_ctx_ref, lat_tiles) + mod_ref[0, gate_idx:gate_idx + 1, :] * a
        o_ref[...] = a.astype(o_ref.dtype)
        if norm_idx is not None:
            h = _norm_mod_rows(a, gain_ref[...], mod_ref[0], *norm_idx)
            hn_ref[...] = h.astype(hn_ref.dtype)
            if has_router:
                idx_ref[...], gates_ref[...] = _top2_route(h, rw_ref[...])

    if nk == 1:
        epilogue(prods)
        return

    k = pl.program_id(2)

    @pl.when(k == 0)
    def _():
        for acc, p in zip(acc_refs, prods):
            acc[...] = p

    @pl.when(k > 0)
    def _():
        for acc, p in zip(acc_refs, prods):
            acc[...] += p

    @pl.when(k == nk - 1)
    def _():
        epilogue([acc[...] for acc in acc_refs])


def matmul(x, w, *, w3=None, bias=None, res=None, res_ctx=None, mods=None, gate_idx=0, out_dtype=F32,
           silu_x=False, tm=1024, tn=512, tk=None, geo=None, norm=None, router=None):
    m, kdim = x.shape
    n = w.shape[1]
    tm = min(tm, m)
    tk = kdim if tk is None else tk
    nk = kdim // tk
    kind = "swiglu" if w3 is not None else ("res" if res is not None else "plain")
    in_specs = [pl.BlockSpec((tm, tk), lambda i, j, k: (i, k)),
                pl.BlockSpec((tk, tn), lambda i, j, k: (k, j))]
    args = [x, w]
    if w3 is not None:
        in_specs.append(pl.BlockSpec((tk, tn), lambda i, j, k: (k, j)))
        args.append(w3)
    if bias is not None:
        in_specs.append(pl.BlockSpec((1, tn), lambda i, j, k: (0, j)))
        args.append(bias.reshape(1, n))
    lat_tiles = None
    if res is not None:
        if res_ctx is None:
            in_specs.append(pl.BlockSpec((tm, tn), lambda i, j, k: (i, j)))
            args.append(res)
        else:
            lat_tiles = geo["n_lat"] // tm
            in_specs += _split_row_specs((tm, tn), lat_tiles, lambda j, k: j)
            args += [res, res_ctx]
        group = functools.partial(_group_of_tile, tm=tm, **geo)
        in_specs.append(pl.BlockSpec((1, 6, tn), lambda i, j, k: (group(i), 0, j)))
        args.append(mods)
    out_specs = pl.BlockSpec((tm, tn), lambda i, j, k: (i, j))
    out_shape = jax.ShapeDtypeStruct((m, n), out_dtype)
    norm_idx = None
    if norm is not None:
        assert kind == "res" and tn == n
        in_specs.append(pl.BlockSpec((1, n), lambda i, j, k: (0, 0)))
        args.append(norm[0].reshape(1, n))
        norm_idx = (norm[1], norm[2])
        out_specs = [out_specs, pl.BlockSpec((tm, tn), lambda i, j, k: (i, j))]
        out_shape = [out_shape, jax.ShapeDtypeStruct((m, n), BF16 if router is None else F32)]
        if router is not None:
            in_specs.append(pl.BlockSpec((n, LANES), lambda i, j, k: (0, 0)))
            args.append(jnp.pad(router, ((0, 0), (0, LANES - N_EXPERTS))))
            out_specs += [pl.BlockSpec((tm, LANES), lambda i, j, k: (i, 0))] * 2
            out_shape += [jax.ShapeDtypeStruct((m, LANES), jnp.int32), jax.ShapeDtypeStruct((m, LANES), F32)]
    n_acc = 0 if nk == 1 else (2 if w3 is not None else 1)
    return pl.pallas_call(
        functools.partial(_mm_kernel, kind=kind, nk=nk, silu_x=silu_x, has_bias=bias is not None,
                          gate_idx=gate_idx, lat_tiles=lat_tiles, norm_idx=norm_idx,
                          has_router=router is not None),
        grid=(m // tm, n // tn, nk),
        in_specs=in_specs,
        out_specs=out_specs,
        out_shape=out_shape,
        scratch_shapes=[pltpu.VMEM((tm, tn), F32)] * n_acc,
        compiler_params=_cparams(("parallel", "parallel", "arbitrary")),
        name="matmul_" + kind,
    )(*args)


def _sgu_kernel(u_ref, v_ref, ng_ref, ws_ref, bs_ref, o_ref, *, chunks):
    for c in range(chunks):
        rows = slice(c * SGU_CHUNK, (c + 1) * SGU_CHUNK)
        for g in range(A_GROUPS):
            cols = slice(g * LANES, (g + 1) * LANES)
            vg = jax.nn.gelu(v_ref[rows, cols].astype(F32))
            ms = jnp.mean(vg * vg, axis=-1, keepdims=True)
            vn = vg * lax.rsqrt(ms + EPS) * ng_ref[:, cols]
            s = jnp.dot(ws_ref[g].astype(BF16), vn.astype(BF16), preferred_element_type=F32)
            u = jax.nn.gelu(u_ref[rows, cols].astype(F32))
            o_ref[rows, cols] = (u * (s + bs_ref[:, cols])).astype(o_ref.dtype)


def chunk_sgu(hp, norm_g, ws, bs, *, tm=256):
    m = hp.shape[0]
    bs_exp = jnp.repeat(bs.T, LANES, axis=1)
    return pl.pallas_call(
        functools.partial(_sgu_kernel, chunks=tm // SGU_CHUNK),
        grid=(m // tm,),
        in_specs=[
            pl.BlockSpec((tm, MIX_A), lambda i: (i, 0)),
            pl.BlockSpec((tm, MIX_A), lambda i: (i, 1)),
            pl.BlockSpec((1, MIX_A), lambda i: (0, 0)),
            pl.BlockSpec((A_GROUPS, SGU_CHUNK, SGU_CHUNK), lambda i: (0, 0, 0)),
            pl.BlockSpec((SGU_CHUNK, MIX_A), lambda i: (0, 0)),
        ],
        out_specs=pl.BlockSpec((tm, MIX_A), lambda i: (i, 0)),
        out_shape=jax.ShapeDtypeStruct((m, MIX_A), BF16),
        compiler_params=_cparams(("parallel",)),
        name="chunk_sgu",
    )(hp, hp, norm_g.reshape(1, MIX_A), ws, bs_exp)


def _shift_kernel(x_ref, p_ref, n_ref, w_ref, o_ref, *, ts, lat_tiles, tiles_per_lat, tiles_per_ctx):
    i = pl.program_id(0)
    is_lat = i < lat_tiles
    pos = jnp.where(is_lat, i % tiles_per_lat, (i - lat_tiles) % tiles_per_ctx)
    last = jnp.where(is_lat, tiles_per_lat - 1, tiles_per_ctx - 1)
    x = x_ref[...]
    prev_row = jnp.where(pos == 0, 0.0, p_ref[7:8, :])
    next_row = jnp.where(pos == last, 0.0, n_ref[0:1, :])
    rows = lax.broadcasted_iota(jnp.int32, x.shape, 0)
    xm = jnp.where(rows == 0, prev_row, pltpu.roll(x, 1, axis=0))
    xp = jnp.where(rows == ts - 1, next_row, pltpu.roll(x, ts - 1, axis=0))
    o_ref[...] = (w_ref[0:1, :] * xm + w_ref[1:2, :] * x + w_ref[2:3, :] * xp).astype(o_ref.dtype)


def token_shift(z, shift_w, *, n_lat, seq, ctx_len, ts=256, tc=Z_WIDTH):
    m = z.shape[0]
    halo = 8
    nblk8 = m // halo
    w_pad = jnp.pad(shift_w, ((0, 0), (0, Z_WIDTH - B_WIDTH)))
    kern = functools.partial(_shift_kernel, ts=ts, lat_tiles=n_lat // ts, tiles_per_lat=seq // ts,
                             tiles_per_ctx=ctx_len // ts)
    return pl.pallas_call(
        kern,
        grid=(m // ts, Z_WIDTH // tc),
        in_specs=[
            pl.BlockSpec((ts, tc), lambda i, j: (i, j)),
            pl.BlockSpec((halo, tc), lambda i, j: (jnp.maximum(i * (ts // halo) - 1, 0), j)),
            pl.BlockSpec((halo, tc), lambda i, j: (jnp.minimum((i + 1) * (ts // halo), nblk8 - 1), j)),
            pl.BlockSpec((3, tc), lambda i, j: (0, j)),
        ],
        out_specs=pl.BlockSpec((ts, tc), lambda i, j: (i, j)),
        out_shape=jax.ShapeDtypeStruct((m, Z_WIDTH), BF16),
        compiler_params=_cparams(("parallel", "parallel")),
        name="token_shift",
    )(z, z, z, w_pad)


def _mask_lanes(x, m0):
    z = jnp.zeros_like(x)
    return jnp.concatenate([jnp.where(m0, x, z), jnp.where(m0, z, x)], axis=0)


def _dot(a, b):
    return jnp.dot(a.astype(BF16), b.astype(BF16), preferred_element_type=F32)


def _dot_nt(a, b):
    return lax.dot_general(a.astype(BF16), b.astype(BF16), (((1,), (1,)), ((), ())),
                           preferred_element_type=F32)


def _dot_tn(a, b):
    return lax.dot_general(a.astype(BF16), b.astype(BF16), (((0,), (0,)), ((), ())),
                           preferred_element_type=F32)


def _head_sums(x, m0):
    s0 = jnp.sum(jnp.where(m0, x, 0.0), axis=-1, keepdims=True)
    s1 = jnp.sum(jnp.where(m0, 0.0, x), axis=-1, keepdims=True)
    return jnp.where(m0, s0, s1)


def _rwkv_kernel(*refs, reverse, n_chunks):
    if reverse:
        (zr_ref, zk_ref, zv_ref, zl_ref, w0_ref, wt_ref, a0_ref, wa_ref, kk_ref, ka_ref, rk_ref,
         y0_ref, g_ref, bon0_ref, lnw_ref, lnb_ref, out_ref,
         s_ref, kk_s, bt_s, kd_s, rt_s, v_s, y_s, ge_s) = refs
    else:
        (zr_ref, zk_ref, zv_ref, zl_ref, w0_ref, wt_ref, a0_ref, wa_ref, kk_ref, ka_ref, rk_ref, wg_ref,
         y_out_ref, g_out_ref, bon_out_ref,
         s_ref, kk_s, bt_s, kd_s, rt_s, v_s, y_s, ge_s) = refs
    ch = RW_CHUNK
    j = pl.program_id(1)

    @pl.when(j == 0)
    def _():
        s_ref[...] = jnp.zeros_like(s_ref)

    zl = zl_ref[...].astype(F32)
    lora_in = zl[:, 0:LANES]
    wl = w0_ref[...] + _dot(jnp.tanh(lora_in), wt_ref[...])
    wlog = -(jnp.maximum(-wl, 0.0) + jnp.log(1.0 + jnp.exp(-jnp.abs(wl)))) - 0.5
    lw = -jnp.exp(wlog)
    a = jax.nn.sigmoid(a0_ref[...] + _dot(lora_in, wa_ref[...]))
    r = zr_ref[...].astype(F32)
    k = zk_ref[...].astype(F32)
    v = zv_ref[...].astype(F32)
    kd = k * (1.0 + (a - 1.0) * ka_ref[...])
    kraw = k * kk_ref[...]
    rkd = r * kd * rk_ref[...]
    if not reverse:
        g_out_ref[...] = _dot(jax.nn.sigmoid(zl[:, LANES:3 * LANES]), wg_ref[...])

    lane = lax.broadcasted_iota(jnp.int32, (1, LANES), 1)
    m0 = lane < B_HEAD
    ti = lax.broadcasted_iota(jnp.int32, (ch, ch), 0)
    si = lax.broadcasted_iota(jnp.int32, (ch, ch), 1)
    tri = ((si >= ti) if reverse else (si <= ti)).astype(BF16)

    kn_l, b_l = [], []
    for p in range(RW_PAIRS):
        cols = slice(p * LANES, (p + 1) * LANES)
        kp = kraw[:, cols]
        nrm = jnp.sqrt(_head_sums(kp * kp, m0))
        kn = kp / jnp.maximum(nrm, 1e-12)
        kn_l.append(kn)
        b_l.append(kn * a[:, cols])
        v_s[p] = v[:, cols]
        if not reverse:
            bon_out_ref[:, cols] = _head_sums(rkd[:, cols], m0) * v[:, cols]

    for c in range(n_chunks):
        rws = slice(c * ch, (c + 1) * ch)
        lw_c = lw[rws, :]
        hi = lw_c.astype(BF16)
        rem = lw_c - hi.astype(F32)
        mid = rem.astype(BF16)
        lo = (rem - mid.astype(F32)).astype(BF16)
        cs = (jnp.dot(tri, hi, preferred_element_type=F32) + jnp.dot(tri, mid, preferred_element_type=F32)
              + jnp.dot(tri, lo, preferred_element_type=F32))
        total = cs[0:1, :] if reverse else cs[ch - 1:ch, :]
        g_in = jnp.exp(cs)
        g_ex = jnp.exp(cs - lw_c)
        g_inv = jnp.exp(-cs)
        g_end = jnp.exp(total)
        for p in range(RW_PAIRS):
            cols = slice(p * LANES, (p + 1) * LANES)
            kk_s[p, rws, :] = kn_l[p][rws, :] * g_ex[:, cols]
            bt_s[p, rws, :] = b_l[p][rws, :] * g_inv[:, cols]
            kd_s[p, rws, :] = kd[rws, cols] * g_inv[:, cols]
            rt_s[p, rws, :] = r[rws, cols] * g_in[:, cols]
            ge_s[p, c:c + 1, :] = g_end[:, cols]

    ri = lax.broadcasted_iota(jnp.int32, (2 * ch, 2 * ch), 0)
    ci = lax.broadcasted_iota(jnp.int32, (2 * ch, 2 * ch), 1)
    same_head = (ri // ch) == (ci // ch)
    if reverse:
        strict = same_head & (ci > ri)
        incl = same_head & (ci >= ri)
    else:
        strict = same_head & (ci < ri)
        incl = same_head & (ci <= ri)
    eye = (ri == ci).astype(F32)
    zero = jnp.zeros((2 * ch, 2 * ch), F32)
    n_sq = int(math.log2(INV_BLOCK)) - 1
    blk_diag = []
    bs = INV_BLOCK
    while bs <= ch:
        blk_diag.append((ri // bs) == (ci // bs))
        bs *= 2
    pairs = range(RW_PAIRS)

    def chunk_body(ci_, carry):
        c = (n_chunks - 1 - ci_) if reverse else ci_
        rows = pl.ds(pl.multiple_of(c * ch, ch), ch)
        kk_t = [kk_s[p, rows, :] for p in pairs]
        rt = [rt_s[p, rows, :] for p in pairs]
        bt_m = [_mask_lanes(bt_s[p, rows, :], m0) for p in pairs]
        kd_m = [_mask_lanes(kd_s[p, rows, :], m0) for p in pairs]
        v_st = [_mask_lanes(v_s[p, rows, :], m0) for p in pairs]
        g_end = [ge_s[p, pl.ds(c, 1), :] for p in pairs]

        pq = [_dot_nt(jnp.concatenate([kk_t[p], kk_t[p], rt[p], rt[p]], axis=0),
                      jnp.concatenate([bt_m[p], kd_m[p]], axis=0)) for p in pairs]
        a_m = [jnp.where(strict, pq[p][0:2 * ch, 0:2 * ch], zero) for p in pairs]
        b_m = [jnp.where(strict, pq[p][0:2 * ch, 2 * ch:4 * ch], zero) for p in pairs]
        m2 = [jnp.where(incl, pq[p][2 * ch:4 * ch, 0:2 * ch], zero) for p in pairs]
        m1 = [jnp.where(incl, pq[p][2 * ch:4 * ch, 2 * ch:4 * ch], zero) for p in pairs]

        q = [jnp.where(blk_diag[0], -a_m[p], zero) for p in pairs]
        t_m = [eye + q[p] for p in pairs]
        q = [_dot(q[p], q[p]) for p in pairs]
        for _ in range(n_sq - 1):
            both = [_dot(jnp.concatenate([q[p], t_m[p]], axis=0), q[p]) for p in pairs]
            q = [both[p][0:2 * ch] for p in pairs]
            t_m = [t_m[p] + both[p][2 * ch:4 * ch] for p in pairs]
        t_m = [t_m[p] + _dot(t_m[p], q[p]) for p in pairs]
        for lvl in range(1, len(blk_diag)):
            off = [jnp.where(blk_diag[lvl] & ~blk_diag[lvl - 1], a_m[p], zero) for p in pairs]
            t_off = [_dot(t_m[p], off[p]) for p in pairs]
            t_m = [t_m[p] - _dot(t_off[p], t_m[p]) for p in pairs]

        bm1v = [_dot(jnp.concatenate([b_m[p], m1[p]], axis=0), v_st[p]) for p in pairs]
        ku = [_dot(t_m[p], jnp.concatenate([_mask_lanes(kk_t[p], m0), bm1v[p][0:2 * ch]], axis=1))
              for p in pairs]
        m2ku = [_dot(m2[p], ku[p]) for p in pairs]
        s_bd = [s_ref[p] for p in pairs]
        rp, y0 = [], []
        for p in pairs:
            rp_s = _mask_lanes(rt[p], m0) - m2ku[p][:, 0:LANES]
            y0_s = bm1v[p][2 * ch:4 * ch] - m2ku[p][:, LANES:2 * LANES]
            rp.append(rp_s[0:ch] + rp_s[ch:2 * ch])
            y0.append(y0_s[0:ch] + y0_s[ch:2 * ch])
        rs = [_dot_nt(jnp.concatenate([rp[p], ku[p][:, 0:LANES]], axis=0), s_bd[p]) for p in pairs]
        ys = [rs[p][0:ch] + y0[p] for p in pairs]
        u_all = [rs[p][ch:3 * ch] + ku[p][:, LANES:2 * LANES] for p in pairs]
        for p in pairs:
            upd = _dot_tn(jnp.concatenate([v_st[p], -u_all[p]], axis=0),
                          jnp.concatenate([kd_m[p] * g_end[p], bt_m[p] * g_end[p]], axis=0))
            s_ref[p] = s_bd[p] * g_end[p] + upd
            y_s[p, rows, :] = ys[p]
        return carry

    lax.fori_loop(0, n_chunks, chunk_body, 0)

    for p in range(RW_PAIRS):
        cols = slice(p * LANES, (p + 1) * LANES)
        if not reverse:
            y_out_ref[:, cols] = y_s[p]
        else:
            bon1 = _head_sums(rkd[:, cols], m0) * v[:, cols]
            ysum = y0_ref[:, cols] + y_s[p]
            mu = _head_sums(ysum, m0) * (1.0 / B_HEAD)
            dlt = ysum - mu
            var = _head_sums(dlt * dlt, m0) * (1.0 / B_HEAD)
            yn = dlt * lax.rsqrt(var + GN_EPS)
            o = (yn * lnw_ref[:, cols] + lnb_ref[:, cols] + bon0_ref[:, cols] + bon1) * g_ref[:, cols]
            out_ref[:, cols] = o.astype(out_ref.dtype)


def rwkv7_bidir(zs, w0, w_up, a0, a_up, g_up, k_k, k_a, r_k, lnx_w, lnx_b, *, batch, seq, ctx_len):
    m = zs.shape[0]
    n_lat = batch * seq
    tb = RW_TILE
    nct, nlt = ctx_len // tb, seq // tb
    steps = nct + nlt

    def row_block(reverse):
        def f(b, j):
            jc = (nct - 1 - j) if reverse else j
            jl = (nlt - 1 - (j - nct)) if reverse else (j - nct)
            return jnp.where(j < nct, (n_lat + b * ctx_len) // tb + jc, b * nlt + jl)
        return f

    def pad_rows(w, top, total):
        return jnp.pad(w, ((top, total - top - w.shape[0]), (0, 0)))

    wg = pad_rows(g_up, 0, 2 * LANES)
    rk = r_k.reshape(1, MIX_B)
    vec = lambda a: a.reshape(1, MIX_B)
    scratch = ([pltpu.VMEM((RW_PAIRS, LANES, LANES), F32)] + [pltpu.VMEM((RW_PAIRS, tb, LANES), F32)] * 6
               + [pltpu.VMEM((RW_PAIRS, 8, LANES), F32)])
    const = lambda shape: pl.BlockSpec(shape, lambda b, j: (0,) * len(shape))

    def call(reverse, extra_in, extra_specs, out_shapes, out_specs, d):
        rb = row_block(reverse)
        wt = pad_rows(w_up[d], 0, LANES)
        wa = pad_rows(a_up[d], DECAY_LORA, LANES)
        in_specs = [
            pl.BlockSpec((tb, MIX_B), lambda b, j: (rb(b, j), 0)),
            pl.BlockSpec((tb, MIX_B), lambda b, j: (rb(b, j), 1)),
            pl.BlockSpec((tb, MIX_B), lambda b, j: (rb(b, j), 2)),
            pl.BlockSpec((tb, 512), lambda b, j: (rb(b, j), 6)),
            const((1, MIX_B)), const((LANES, MIX_B)), const((1, MIX_B)), const((LANES, MIX_B)),
            const((1, MIX_B)), const((1, MIX_B)), const((1, MIX_B)),
        ] + extra_specs
        args = [zs, zs, zs, zs, vec(w0[d]), wt, vec(a0[d]), wa, vec(k_k), vec(k_a), rk] + extra_in
        return pl.pallas_call(
            functools.partial(_rwkv_kernel, reverse=reverse, n_chunks=tb // RW_CHUNK),
            grid=(batch, steps),
            in_specs=in_specs,
            out_specs=out_specs,
            out_shape=out_shapes,
            scratch_shapes=scratch,
            compiler_params=_cparams(("parallel", "arbitrary")),
            name="rwkv7_rev" if reverse else "rwkv7_fwd",
        )(*args)

    rbf = row_block(False)
    tile_f = pl.BlockSpec((tb, MIX_B), lambda b, j: (rbf(b, j), 0))
    y0, g, bon0 = call(False, [wg], [const((2 * LANES, MIX_B))],
                       [jax.ShapeDtypeStruct((m, MIX_B), F32)] * 3, [tile_f] * 3, 0)
    rbr = row_block(True)
    tile_r = pl.BlockSpec((tb, MIX_B), lambda b, j: (rbr(b, j), 0))
    out = call(True, [y0, g, bon0, vec(lnx_w), vec(lnx_b)],
               [tile_r, tile_r, tile_r, const((1, MIX_B)), const((1, MIX_B))],
               jax.ShapeDtypeStruct((m, MIX_B), BF16), tile_r, 1)
    return out


def _half_rms(x, gain, m0):
    ms = _head_sums(x * x, m0) * (1.0 / C_HEAD)
    return x * lax.rsqrt(ms + EPS) * gain


def _rope(x, cos, sin):
    lane = lax.broadcasted_iota(jnp.int32, x.shape, 1)
    first = (lane % ROPE_AXIS) < (ROPE_AXIS // 2)
    half = ROPE_AXIS // 2
    rot = jnp.where(first, -pltpu.roll(x, LANES - half, axis=1), pltpu.roll(x, half, axis=1))
    return x * cos + rot * sin


def _attn_kernel(q_ref, kl_ref, vl_ref, kc_ref, vc_ref, cq_ref, sq_ref, ck_ref, sk_ref, qg_ref, kg_ref,
                 lam_ref, sg_ref, o_ref, k_s, v_s, *, ctx_len, lam_init, q_blk):
    qi = pl.program_id(2)
    lane = lax.broadcasted_iota(jnp.int32, (1, LANES), 1)
    m0 = lane < C_HEAD

    @pl.when(qi == 0)
    def _():
        kc = _half_rms(kc_ref[...].astype(F32), kg_ref[...], m0)
        kl = _rope(_half_rms(kl_ref[...].astype(F32), kg_ref[...], m0), ck_ref[...], sk_ref[...])
        k_s[0:ctx_len, :] = kc.astype(BF16)
        k_s[ctx_len:, :] = kl.astype(BF16)
        v_s[0:ctx_len, :] = vc_ref[...].astype(BF16)
        v_s[ctx_len:, :] = vl_ref[...].astype(BF16)

    lp = lam_ref[...]
    lam = (jnp.exp(jnp.sum(lp[0:1] * lp[1:2], keepdims=True)) - jnp.exp(jnp.sum(lp[2:3] * lp[3:4], keepdims=True))
           + lam_init)
    q = (_rope(_half_rms(q_ref[...].astype(F32), qg_ref[...], m0), cq_ref[...], sq_ref[...])
         * (C_HEAD ** -0.5 * LOG2E))
    keys = k_s[...]
    vals = v_s[...]
    zq = jnp.zeros_like(q)
    q_sub = [jnp.where(m0, q, zq), jnp.where(m0, zq, q)]
    blocks = [(slice(r0, r0 + q_blk), i) for r0 in range(0, q.shape[0], q_blk) for i in (0, 1)]
    lookahead = 2
    scores = [_dot_nt(q_sub[i][rows], keys) for rows, i in blocks[:lookahead]]
    outs = []
    for n, (rows, i) in enumerate(blocks):
        s = scores[n]
        e = jnp.exp2(s - jnp.max(s, axis=-1, keepdims=True))
        w = (lam if i else 1.0) / jnp.sum(e, axis=-1, keepdims=True)
        if n + lookahead < len(blocks):
            rows2, i2 = blocks[n + lookahead]
            scores.append(_dot_nt(q_sub[i2][rows2], keys))
        outs.append(jnp.dot(e.astype(BF16), vals, preferred_element_type=F32) * w)
    for n, r0 in enumerate(range(0, q.shape[0], q_blk)):
        o = outs[2 * n] - outs[2 * n + 1]
        ms = jnp.mean(o * o, axis=-1, keepdims=True)
        o = o * lax.rsqrt(ms + SUBLN_EPS) * sg_ref[...] * (1.0 - lam_init)
        o_ref[r0:r0 + q_blk, :] = o.astype(o_ref.dtype)


def _rope_tables(n):
    rows = n // GRID_W
    row = jnp.broadcast_to(jnp.arange(rows, dtype=F32)[:, None], (rows, GRID_W)).reshape(-1)
    col = jnp.broadcast_to(jnp.arange(GRID_W, dtype=F32)[None, :], (rows, GRID_W)).reshape(-1)
    inv = ROPE_THETA ** (-jnp.arange(0, ROPE_AXIS, 2, dtype=F32) / ROPE_AXIS)
    ar = row[:, None] * inv
    ac = col[:, None] * inv
    ang = jnp.concatenate([ar, ar, ac, ac, ar, ar, ac, ac], axis=-1)
    return jnp.cos(ang), jnp.sin(ang)


def diff_attention(qkv, q_g, k_g, lam_params, subln_g, lam_init, *, batch, seq, ctx_len, tq=1024):
    n_lat = batch * seq
    tq = min(tq, seq)
    cos, sin = _rope_tables(seq)
    nq = seq // tq
    hq, hk, hv = 0, D_MODEL // LANES, 2 * D_MODEL // LANES
    ctx_blk0 = n_lat // ctx_len
    two = lambda a: jnp.concatenate([a, a]).reshape(1, LANES)
    const = lambda shape: pl.BlockSpec(shape, lambda b, h, i: (0,) * len(shape))
    return pl.pallas_call(
        functools.partial(_attn_kernel, ctx_len=ctx_len, lam_init=lam_init, q_blk=min(128, tq)),
        grid=(batch, C_HEADS, nq),
        in_specs=[
            pl.BlockSpec((tq, LANES), lambda b, h, i: (b * nq + i, hq + h)),
            pl.BlockSpec((seq, LANES), lambda b, h, i: (b, hk + h)),
            pl.BlockSpec((seq, LANES), lambda b, h, i: (b, hv + h)),
            pl.BlockSpec((ctx_len, LANES), lambda b, h, i: (ctx_blk0 + b, hk + h)),
            pl.BlockSpec((ctx_len, LANES), lambda b, h, i: (ctx_blk0 + b, hv + h)),
            pl.BlockSpec((tq, LANES), lambda b, h, i: (i, 0)),
            pl.BlockSpec((tq, LANES), lambda b, h, i: (i, 0)),
            const((seq, LANES)), const((seq, LANES)),
            const((1, LANES)), const((1, LANES)), const((4, C_HEAD)), const((1, LANES)),
        ],
        out_specs=pl.BlockSpec((tq, LANES), lambda b, h, i: (b * nq + i, h)),
        out_shape=jax.ShapeDtypeStruct((n_lat, D_MODEL), BF16),
        scratch_shapes=[pltpu.VMEM((ctx_len + seq, LANES), BF16)] * 2,
        compiler_params=_cparams(("parallel", "parallel", "arbitrary")),
        name="diff_attention",
    )(qkv, qkv, qkv, qkv, qkv, cos, sin, cos, sin, two(q_g), two(k_g), lam_params, subln_g.reshape(1, LANES))


def _top2_route(h, rw):
    logits = jnp.dot(h, rw, precision=lax.Precision.HIGHEST, preferred_element_type=F32)
    lane = lax.broadcasted_iota(jnp.int32, logits.shape, 1)
    neg = jnp.float32(-jnp.inf)
    lg = jnp.where(lane < N_EXPERTS, logits, neg)
    m1 = jnp.max(lg, axis=-1, keepdims=True)
    i1 = jnp.min(jnp.where(lg == m1, lane, LANES), axis=-1, keepdims=True)
    lg2 = jnp.where(lane == i1, neg, lg)
    m2 = jnp.max(lg2, axis=-1, keepdims=True)
    i2 = jnp.min(jnp.where(lg2 == m2, lane, LANES), axis=-1, keepdims=True)
    e2 = jnp.exp(m2 - m1)
    g1 = 1.0 / (1.0 + e2)
    g2 = e2 * g1
    idx = jnp.where(lane == 0, i1, jnp.where(lane == 1, i2, 0))
    gates = jnp.where(lane == 0, g1, jnp.where(lane == 1, g2, 0.0))
    return idx, gates


def _row_copy(src_hbm, dst_ref, src_row, dst_row, sem):
    return pltpu.make_async_copy(src_hbm.at[pl.ds(src_row, 1)], dst_ref.at[pl.ds(dst_row, 1)], sem)


def _gather_kernel(used_ref, idx_ref, nxt_ref, src_hbm, o_ref, buf, sem, *, tg, n_tiles):
    i = pl.program_id(0)
    slot = i % 2

    def issue_tile(ids_ref, s):
        def issue(g, c):
            for u in range(DMA_UNROLL):
                r = g * DMA_UNROLL + u
                _row_copy(src_hbm, buf.at[s], ids_ref[0, 0, r], r, sem.at[s]).start(priority=u % 2)
            return c

        lax.fori_loop(0, tg // DMA_UNROLL, issue, 0)

    @pl.when((i == 0) & (used_ref[0] > 0))
    def _():
        issue_tile(idx_ref, slot)

    @pl.when((i + 1 < n_tiles) & ((i + 1) * tg < used_ref[0]))
    def _():
        issue_tile(nxt_ref, 1 - slot)

    @pl.when(i * tg >= used_ref[0])
    def _():
        o_ref[...] = jnp.zeros_like(o_ref)

    @pl.when(i * tg < used_ref[0])
    def _():
        def drain(r, c):
            _row_copy(src_hbm, buf.at[slot], 0, r, sem.at[slot]).wait()
            return c

        lax.fori_loop(0, tg, drain, 0, unroll=DMA_UNROLL)
        o_ref[...] = buf[slot].astype(o_ref.dtype)


def gather_rows(src, idx, rows_used, *, tg=256):
    mp = idx.shape[0]
    d = src.shape[1]
    n_tiles = mp // tg
    idx3 = idx.reshape(n_tiles, 1, tg)
    return pl.pallas_call(
        functools.partial(_gather_kernel, tg=tg, n_tiles=n_tiles),
        grid_spec=pltpu.PrefetchScalarGridSpec(
            num_scalar_prefetch=1,
            grid=(n_tiles,),
            in_specs=[pl.BlockSpec((1, 1, tg), lambda i, u: (i, 0, 0), memory_space=pltpu.SMEM),
                      pl.BlockSpec((1, 1, tg), lambda i, u: (jnp.minimum(i + 1, n_tiles - 1), 0, 0),
                                   memory_space=pltpu.SMEM),
                      pl.BlockSpec(memory_space=pl.ANY)],
            out_specs=pl.BlockSpec((tg, d), lambda i, u: (i, 0)),
            scratch_shapes=[pltpu.VMEM((2, tg, d), src.dtype), pltpu.SemaphoreType.DMA((2,))],
        ),
        out_shape=jax.ShapeDtypeStruct((mp, d), BF16),
        compiler_params=_cparams(("arbitrary",)),
        name="moe_gather",
    )(rows_used, idx3, idx3, src)


def _gmm_kernel(te_ref, nu_ref, *refs, kind, kc):
    if kind == "swiglu":
        x_ref, w1_ref, w3_ref, o_ref = refs
        w_refs = (w1_ref, w3_ref)
    else:
        x_ref, w1_ref, o_ref = refs
        w_refs = (w1_ref,)
    i = pl.program_id(1)

    @pl.when(i >= nu_ref[0])
    def _():
        o_ref[...] = jnp.zeros_like(o_ref)

    @pl.when(i < nu_ref[0])
    def _():
        kdim = x_ref.shape[1]
        accs = [None] * len(w_refs)
        for k0 in range(0, kdim, kc):
            x = x_ref[:, k0:k0 + kc].astype(BF16)
            for n, w_ref in enumerate(w_refs):
                p = jnp.dot(x, w_ref[0, k0:k0 + kc, :].astype(BF16), preferred_element_type=F32)
                accs[n] = p if accs[n] is None else accs[n] + p
        a = accs[0]
        if kind == "swiglu":
            a = jax.nn.silu(a) * accs[1]
        o_ref[...] = a.astype(o_ref.dtype)


def grouped_matmul(x, w, tile_expert, n_used, *, w3=None, out_dtype=F32, tm=512, tn=512, kc=1024):
    mp, kdim = x.shape
    n = w.shape[2]
    kind = "swiglu" if w3 is not None else "plain"
    w_spec = pl.BlockSpec((1, kdim, tn), lambda j, i, te, nu: (te[i], 0, j))
    in_specs = [pl.BlockSpec((tm, kdim), lambda j, i, te, nu: (i, 0)), w_spec]
    args = [x, w]
    if w3 is not None:
        in_specs.append(w_spec)
        args.append(w3)
    return pl.pallas_call(
        functools.partial(_gmm_kernel, kind=kind, kc=min(kc, kdim)),
        grid_spec=pltpu.PrefetchScalarGridSpec(
            num_scalar_prefetch=2,
            grid=(n // tn, mp // tm),
            in_specs=in_specs,
            out_specs=pl.BlockSpec((tm, tn), lambda j, i, te, nu: (i, j)),
        ),
        out_shape=jax.ShapeDtypeStruct((mp, n), out_dtype),
        compiler_params=pltpu.CompilerParams(dimension_semantics=("arbitrary", "arbitrary"),
                                             vmem_limit_bytes=GMM_VMEM_LIMIT),
        name="moe_gmm_" + kind,
    )(tile_expert, n_used, *args)


def _combine_kernel(pos_ref, nxt_ref, ys_hbm, x_ref, gate_ref, mod_ref, o_ref, buf, sem, *, tc, n_tiles, gate_idx):
    i = pl.program_id(0)
    slot = i % 2

    def issue_tile(p_ref, s):
        def issue(g, c):
            for u in range(DMA_UNROLL):
                r = g * DMA_UNROLL + u
                _row_copy(ys_hbm, buf.at[s, 0], p_ref[0, 0, 2 * r], r, sem.at[s]).start(priority=0)
                _row_copy(ys_hbm, buf.at[s, 1], p_ref[0, 0, 2 * r + 1], r, sem.at[s]).start(priority=1)
            return c

        lax.fori_loop(0, tc // DMA_UNROLL, issue, 0)

    @pl.when(i == 0)
    def _():
        issue_tile(pos_ref, slot)

    @pl.when(i + 1 < n_tiles)
    def _():
        issue_tile(nxt_ref, 1 - slot)

    def drain(r, c):
        _row_copy(ys_hbm, buf.at[slot, 0], 0, r, sem.at[slot]).wait()
        _row_copy(ys_hbm, buf.at[slot, 1], 0, r, sem.at[slot]).wait()
        return c

    lax.fori_loop(0, tc, drain, 0, unroll=DMA_UNROLL)
    g = gate_ref[...]
    moe = g[:, 0:1] * buf[slot, 0] + g[:, 1:2] * buf[slot, 1]
    o_ref[...] = x_ref[...] + mod_ref[0, gate_idx:gate_idx + 1, :] * moe


def moe_combine(ys, pos, x, gates, mods, gate_idx, *, seq, tc=256):
    m, d = x.shape
    n_tiles = m // tc
    pos3 = pos.reshape(n_tiles, 1, 2 * tc)
    return pl.pallas_call(
        functools.partial(_combine_kernel, tc=tc, n_tiles=n_tiles, gate_idx=gate_idx),
        grid=(n_tiles,),
        in_specs=[pl.BlockSpec((1, 1, 2 * tc), lambda i: (i, 0, 0), memory_space=pltpu.SMEM),
                  pl.BlockSpec((1, 1, 2 * tc), lambda i: (jnp.minimum(i + 1, n_tiles - 1), 0, 0),
                               memory_space=pltpu.SMEM),
                  pl.BlockSpec(memory_space=pl.ANY),
                  pl.BlockSpec((tc, d), lambda i: (i, 0)),
                  pl.BlockSpec((tc, LANES), lambda i: (i, 0)),
                  pl.BlockSpec((1, 6, d), lambda i: ((i * tc) // seq, 0, 0))],
        out_specs=pl.BlockSpec((tc, d), lambda i: (i, 0)),
        out_shape=jax.ShapeDtypeStruct((m, d), F32),
        scratch_shapes=[pltpu.VMEM((2, 2, tc, d), F32), pltpu.SemaphoreType.DMA((2,))],
        compiler_params=_cparams(("arbitrary",)),
        name="moe_combine",
    )(pos3, pos3, ys, x, gates, mods)


def moe_layer(x, xn, idx, gates, mods, w1, w3, w2, *, seq, tm=512):
    n = x.shape[0]
    e_flat = idx[:, 0:2].reshape(-1)
    onehot = (e_flat[:, None] == jnp.arange(N_EXPERTS)[None, :]).astype(jnp.int32)
    ranks = jnp.cumsum(onehot, axis=0) - onehot
    rank = jnp.sum(ranks * onehot, axis=1)
    counts = jnp.sum(onehot, axis=0)
    padded = ((counts + tm - 1) // tm) * tm
    starts = jnp.cumsum(padded) - padded
    pos = starts[e_flat] + rank
    mp = 2 * n + N_EXPERTS * tm
    token_of_row = jnp.zeros((mp,), jnp.int32).at[pos].set(jnp.arange(2 * n, dtype=jnp.int32) // 2)
    n_tiles = mp // tm
    ends = jnp.cumsum(padded)
    tile_start = jnp.arange(n_tiles, dtype=jnp.int32) * tm
    tile_expert = jnp.minimum(jnp.sum((tile_start[:, None] >= ends[None, :]).astype(jnp.int32), axis=1),
                              N_EXPERTS - 1).astype(jnp.int32)
    n_used = (ends[-1] // tm).astype(jnp.int32).reshape(1)

    xs = gather_rows(xn, token_of_row, ends[-1].astype(jnp.int32).reshape(1))
    hs = grouped_matmul(xs, w1, tile_expert, n_used, w3=w3, out_dtype=BF16, tm=tm, tn=1024)
    ys = grouped_matmul(hs, w2, tile_expert, n_used, out_dtype=F32, tm=tm, tn=512)
    return moe_combine(ys, pos.astype(jnp.int32), x, gates, mods, 5, seq=seq)


def kernel(x, c, ctx, c_ctx, l0_ada_w, l0_ada_b, l0_norm1_g, l0_norm2_g, l0_w_in, l0_sgu_norm_g, l0_sgu_w, l0_sgu_b, l0_shift_w, l0_w0, l0_w_up, l0_a0, l0_a_up, l0_g_up, l0_k_k, l0_k_a, l0_r_k, l0_lnx_w, l0_lnx_b, l0_w_out, l0_ffn_w1, l0_ffn_w3, l0_ffn_w2, l1_ada_w, l1_ada_b, l1_norm1_g, l1_norm2_g, l1_w_qkv, l1_q_norm_g, l1_k_norm_g, l1_lam_q1, l1_lam_k1, l1_lam_q2, l1_lam_k2, l1_subln_g, l1_w_out, l1_router, l1_exp_w1, l1_exp_w3, l1_exp_w2):
    batch, seq, d = x.shape
    ctx_len = ctx.shape[1]
    n_lat = batch * seq
    tm = min(1024, seq, batch * ctx_len)
    tm_res = min(256, tm)
    geo = dict(n_lat=n_lat, seq=seq, batch=batch)
    bf = lambda w: w.astype(BF16)

    x_lat = x.reshape(n_lat, d)
    x_ctx = ctx.reshape(batch * ctx_len, d)
    cond = jnp.concatenate([c, c_ctx[None, :], jnp.zeros((16 - batch - 1, d), F32)], axis=0)

    def ada(w, b):
        return matmul(cond, w, bias=b, silu_x=True, tn=1024).reshape(16, 6, d)

    mods = ada(l0_ada_w, l0_ada_b)
    hn = norm_mod(x_lat, l0_norm1_g, mods, 0, 1, tm=min(512, tm), x_ctx=x_ctx, **geo)
    w_z = bf(jnp.pad(l0_w_in[:, 2 * MIX_A:], ((0, 0), (0, Z_WIDTH - B_WIDTH))))
    hp_a = matmul(hn, bf(l0_w_in[:, :2 * MIX_A]), out_dtype=BF16, tm=tm, tn=1024)
    hp_z = matmul(hn, w_z, tm=tm, tn=Z_WIDTH // 4)
    a_out = chunk_sgu(hp_a, l0_sgu_norm_g, l0_sgu_w, l0_sgu_b)
    zs = token_shift(hp_z, l0_shift_w, n_lat=n_lat, seq=seq, ctx_len=ctx_len)
    b_out = rwkv7_bidir(zs, l0_w0, l0_w_up, l0_a0, l0_a_up, l0_g_up, l0_k_k, l0_k_a, l0_r_k, l0_lnx_w, l0_lnx_b,
                        batch=batch, seq=seq, ctx_len=ctx_len)
    mixed = jnp.concatenate([a_out, b_out], axis=1)
    x1, hn = matmul(mixed, bf(l0_w_out), res=x_lat, res_ctx=x_ctx, mods=mods, gate_idx=2, geo=geo, tm=tm_res,
                    tn=d, norm=(l0_norm2_g, 3, 4))
    hff = matmul(hn, bf(l0_ffn_w1), w3=bf(l0_ffn_w3), out_dtype=BF16, tm=tm, tn=l0_ffn_w1.shape[1] // 4)
    x2 = matmul(hff, bf(l0_ffn_w2), res=x1, mods=mods, gate_idx=5, geo=geo, tm=tm, tn=1024,
                tk=l0_ffn_w2.shape[0] // 2)

    mods = ada(l1_ada_w, l1_ada_b)
    hn = norm_mod(x2, l1_norm1_g, mods, 0, 1, tm=min(512, tm), **geo)
    qkv = matmul(hn, bf(l1_w_qkv), out_dtype=BF16, tm=tm, tn=1024)
    lam_params = jnp.stack([l1_lam_q1, l1_lam_k1, l1_lam_q2, l1_lam_k2])
    lam_init = 0.8 - 0.6 * math.exp(-0.3 * 1)
    o = diff_attention(qkv, l1_q_norm_g, l1_k_norm_g, lam_params, l1_subln_g, lam_init,
                       batch=batch, seq=seq, ctx_len=ctx_len)
    x3, xn, idx, gates = matmul(o, bf(l1_w_out), res=x2, mods=mods, gate_idx=2, geo=geo, tm=tm_res, tn=d,
                                norm=(l1_norm2_g, 3, 4), router=l1_router)
    x4 = moe_layer(x3, xn, idx, gates, mods, l1_exp_w1, l1_exp_w3, l1_exp_w2, seq=seq)
    return x4.reshape(batch, seq, d)
```

```python
import functools
import math

import jax
import jax.numpy as jnp
from jax import lax
from jax.experimental import pallas as pl
from jax.experimental.pallas import tpu as pltpu

F32 = jnp.float32
BF16 = jnp.bfloat16

D_MODEL = 2048
GRID_W = 64
EPS = 1e-6

MIX_A = 1024
SGU_CHUNK = 128
A_GROUPS = 8
MIX_B = 1024
B_HEAD = 64
B_HEADS = 16
DECAY_LORA = 64
AAA_LORA = 64
GATE_LORA = 160
B_WIDTH = 3 * MIX_B + DECAY_LORA + AAA_LORA + GATE_LORA
GN_EPS = B_HEAD * 1e-5

C_HEADS = 16
C_HEAD = 64
ROPE_AXIS = C_HEAD // 2
ROPE_THETA = 10000.0
SUBLN_EPS = 1e-5

N_EXPERTS = 8
D_EXPERT = 7168

LOG2E = 1.4426950408889634
LANES = 128
VMEM_LIMIT = 52 * 1024 * 1024
GMM_VMEM_LIMIT = 58 * 1024 * 1024

RW_CHUNK = 64
INV_BLOCK = 16
DMA_UNROLL = 8
RW_TILE = 256
RW_PAIRS = B_HEADS // 2
Z_WIDTH = 3584


def _cparams(sem):
    return pltpu.CompilerParams(dimension_semantics=sem, vmem_limit_bytes=VMEM_LIMIT)


def _group_of_tile(i, tm, n_lat, seq, batch):
    return jnp.where(i * tm < n_lat, (i * tm) // seq, batch)


def _norm_mod_rows(x, gain, mod, s_idx, c_idx):
    ms = jnp.mean(x * x, axis=-1, keepdims=True)
    y = x * lax.rsqrt(ms + EPS) * gain
    return y * (1.0 + mod[c_idx:c_idx + 1, :]) + mod[s_idx:s_idx + 1, :]


def _tile_rows(ref, ctx_ref, lat_tiles):
    if ctx_ref is None:
        return ref[...]
    return jnp.where(pl.program_id(0) < lat_tiles, ref[...], ctx_ref[...])


def _split_row_specs(block, lat_tiles, col):
    lat = pl.BlockSpec(block, lambda i, *a: (jnp.minimum(i, lat_tiles - 1), col(*a)))
    ctx = pl.BlockSpec(block, lambda i, *a: (jnp.maximum(i - lat_tiles, 0), col(*a)))
    return [lat, ctx]


def _norm_mod_kernel(*refs, s_idx, c_idx, lat_tiles):
    if lat_tiles is None:
        x_ref, g_ref, mod_ref, o_ref = refs
        c_ref = None
    else:
        x_ref, c_ref, g_ref, mod_ref, o_ref = refs
    x = _tile_rows(x_ref, c_ref, lat_tiles)
    o_ref[...] = _norm_mod_rows(x, g_ref[...], mod_ref[0], s_idx, c_idx).astype(o_ref.dtype)


def norm_mod(x, gain, mods, s_idx, c_idx, *, n_lat, seq, batch, tm=512, x_ctx=None):
    d = x.shape[1]
    m = x.shape[0] + (0 if x_ctx is None else x_ctx.shape[0])
    grp = functools.partial(_group_of_tile, tm=tm, n_lat=n_lat, seq=seq, batch=batch)
    lat_tiles = None if x_ctx is None else n_lat // tm
    if x_ctx is None:
        x_specs, x_args = [pl.BlockSpec((tm, d), lambda i: (i, 0))], [x]
    else:
        x_specs, x_args = _split_row_specs((tm, d), lat_tiles, lambda: 0), [x, x_ctx]
    return pl.pallas_call(
        functools.partial(_norm_mod_kernel, s_idx=s_idx, c_idx=c_idx, lat_tiles=lat_tiles),
        grid=(m // tm,),
        in_specs=x_specs + [
            pl.BlockSpec((1, d), lambda i: (0, 0)),
            pl.BlockSpec((1, 6, d), lambda i: (grp(i), 0, 0)),
        ],
        out_specs=pl.BlockSpec((tm, d), lambda i: (i, 0)),
        out_shape=jax.ShapeDtypeStruct((m, d), BF16),
        compiler_params=_cparams(("parallel",)),
        name="norm_mod",
    )(*x_args, gain.reshape(1, d), mods)


def _mm_kernel(*refs, kind, nk, silu_x, has_bias, gate_idx, lat_tiles, norm_idx):
    refs = list(refs)
    x_ref = refs.pop(0)
    w_refs = [refs.pop(0)]
    if kind == "swiglu":
        w_refs.append(refs.pop(0))
    bias_ref = refs.pop(0) if has_bias else None
    res_ref = res_ctx_ref = mod_ref = gain_ref = hn_ref = None
    if kind == "res":
        res_ref = refs.pop(0)
        if lat_tiles is not None:
            res_ctx_ref = refs.pop(0)
        mod_ref = refs.pop(0)
        if norm_idx is not None:
            gain_ref = refs.pop(0)
    o_ref = refs.pop(0)
    if norm_idx is not None:
        hn_ref = refs.pop(0)
    acc_refs = refs

    x = x_ref[...]
    if silu_x:
        x = jax.nn.silu(x.astype(F32))
    x = x.astype(BF16)
    prods = [jnp.dot(x, w_ref[...].astype(BF16), preferred_element_type=F32) for w_ref in w_refs]

    def epilogue(vals):
        a = vals[0]
        if kind == "swiglu":
            a = jax.nn.silu(a) * vals[1]
        if has_bias:
            a = a + bias_ref[...]
        if kind == "res":
            a = _tile_rows(res_ref, res_ctx_ref, lat_tiles) + mod_ref[0, gate_idx:gate_idx + 1, :] * a
        o_ref[...] = a.astype(o_ref.dtype)
        if norm_idx is not None:
            hn_ref[...] = _norm_mod_rows(a, gain_ref[...], mod_ref[0], *norm_idx).astype(hn_ref.dtype)

    if nk == 1:
        epilogue(prods)
        return

    k = pl.program_id(2)

    @pl.when(k == 0)
    def _():
        for acc, p in zip(acc_refs, prods):
            acc[...] = p

    @pl.when(k > 0)
    def _():
        for acc, p in zip(acc_refs, prods):
            acc[...] += p

    @pl.when(k == nk - 1)
    def _():
        epilogue([acc[...] for acc in acc_refs])


def matmul(x, w, *, w3=None, bias=None, res=None, res_ctx=None, mods=None, gate_idx=0, out_dtype=F32,
           silu_x=False, tm=1024, tn=512, tk=None, geo=None, norm=None):
    m, kdim = x.shape
    n = w.shape[1]
    tm = min(tm, m)
    tk = kdim if tk is None else tk
    nk = kdim // tk
    kind = "swiglu" if w3 is not None else ("res" if res is not None else "plain")
    in_specs = [pl.BlockSpec((tm, tk), lambda i, j, k: (i, k)),
                pl.BlockSpec((tk, tn), lambda i, j, k: (k, j))]
    args = [x, w]
    if w3 is not None:
        in_specs.append(pl.BlockSpec((tk, tn), lambda i, j, k: (k, j)))
        args.append(w3)
    if bias is not None:
        in_specs.append(pl.BlockSpec((1, tn), lambda i, j, k: (0, j)))
        args.append(bias.reshape(1, n))
    lat_tiles = None
    if res is not None:
        if res_ctx is None:
            in_specs.append(pl.BlockSpec((tm, tn), lambda i, j, k: (i, j)))
            args.append(res)
        else:
            lat_tiles = geo["n_lat"] // tm
            in_specs += _split_row_specs((tm, tn), lat_tiles, lambda j, k: j)
            args += [res, res_ctx]
        group = functools.partial(_group_of_tile, tm=tm, **geo)
        in_specs.append(pl.BlockSpec((1, 6, tn), lambda i, j, k: (group(i), 0, j)))
        args.append(mods)
    out_specs = pl.BlockSpec((tm, tn), lambda i, j, k: (i, j))
    out_shape = jax.ShapeDtypeStruct((m, n), out_dtype)
    norm_idx = None
    if norm is not None:
        assert kind == "res" and tn == n
        in_specs.append(pl.BlockSpec((1, n), lambda i, j, k: (0, 0)))
        args.append(norm[0].reshape(1, n))
        norm_idx = (norm[1], norm[2])
        out_specs = [out_specs, pl.BlockSpec((tm, tn), lambda i, j, k: (i, j))]
        out_shape = [out_shape, jax.ShapeDtypeStruct((m, n), BF16)]
    n_acc = 0 if nk == 1 else (2 if w3 is not None else 1)
    return pl.pallas_call(
        functools.partial(_mm_kernel, kind=kind, nk=nk, silu_x=silu_x, has_bias=bias is not None,
                          gate_idx=gate_idx, lat_tiles=lat_tiles, norm_idx=norm_idx),
        grid=(m // tm, n // tn, nk),
        in_specs=in_specs,
        out_specs=out_specs,
        out_shape=out_shape,
        scratch_shapes=[pltpu.VMEM((tm, tn), F32)] * n_acc,
        compiler_params=_cparams(("parallel", "parallel", "arbitrary")),
        name="matmul_" + kind,
    )(*args)


def _sgu_kernel(u_ref, v_ref, ng_ref, ws_ref, bs_ref, o_ref, *, chunks):
    for c in range(chunks):
        rows = slice(c * SGU_CHUNK, (c + 1) * SGU_CHUNK)
        for g in range(A_GROUPS):
            cols = slice(g * LANES, (g + 1) * LANES)
            vg = jax.nn.gelu(v_ref[rows, cols].astype(F32))
            ms = jnp.mean(vg * vg, axis=-1, keepdims=True)
            vn = vg * lax.rsqrt(ms + EPS) * ng_ref[:, cols]
            s = jnp.dot(ws_ref[g].astype(BF16), vn.astype(BF16), preferred_element_type=F32)
            u = jax.nn.gelu(u_ref[rows, cols].astype(F32))
            o_ref[rows, cols] = (u * (s + bs_ref[:, cols])).astype(o_ref.dtype)


def chunk_sgu(hp, norm_g, ws, bs, *, tm=256):
    m = hp.shape[0]
    bs_exp = jnp.repeat(bs.T, LANES, axis=1)
    return pl.pallas_call(
        functools.partial(_sgu_kernel, chunks=tm // SGU_CHUNK),
        grid=(m // tm,),
        in_specs=[
            pl.BlockSpec((tm, MIX_A), lambda i: (i, 0)),
            pl.BlockSpec((tm, MIX_A), lambda i: (i, 1)),
            pl.BlockSpec((1, MIX_A), lambda i: (0, 0)),
            pl.BlockSpec((A_GROUPS, SGU_CHUNK, SGU_CHUNK), lambda i: (0, 0, 0)),
            pl.BlockSpec((SGU_CHUNK, MIX_A), lambda i: (0, 0)),
        ],
        out_specs=pl.BlockSpec((tm, MIX_A), lambda i: (i, 0)),
        out_shape=jax.ShapeDtypeStruct((m, MIX_A), BF16),
        compiler_params=_cparams(("parallel",)),
        name="chunk_sgu",
    )(hp, hp, norm_g.reshape(1, MIX_A), ws, bs_exp)


def _shift_kernel(x_ref, p_ref, n_ref, w_ref, o_ref, *, ts, lat_tiles, tiles_per_lat, tiles_per_ctx):
    i = pl.program_id(0)
    is_lat = i < lat_tiles
    pos = jnp.where(is_lat, i % tiles_per_lat, (i - lat_tiles) % tiles_per_ctx)
    last = jnp.where(is_lat, tiles_per_lat - 1, tiles_per_ctx - 1)
    x = x_ref[...]
    prev_row = jnp.where(pos == 0, 0.0, p_ref[7:8, :])
    next_row = jnp.where(pos == last, 0.0, n_ref[0:1, :])
    rows = lax.broadcasted_iota(jnp.int32, x.shape, 0)
    xm = jnp.where(rows == 0, prev_row, pltpu.roll(x, 1, axis=0))
    xp = jnp.where(rows == ts - 1, next_row, pltpu.roll(x, ts - 1, axis=0))
    o_ref[...] = (w_ref[0:1, :] * xm + w_ref[1:2, :] * x + w_ref[2:3, :] * xp).astype(o_ref.dtype)


def token_shift(z, shift_w, *, n_lat, seq, ctx_len, ts=256, tc=Z_WIDTH):
    m = z.shape[0]
    halo = 8
    nblk8 = m // halo
    w_pad = jnp.pad(shift_w, ((0, 0), (0, Z_WIDTH - B_WIDTH)))
    kern = functools.partial(_shift_kernel, ts=ts, lat_tiles=n_lat // ts, tiles_per_lat=seq // ts,
                             tiles_per_ctx=ctx_len // ts)
    return pl.pallas_call(
        kern,
        grid=(m // ts, Z_WIDTH // tc),
        in_specs=[
            pl.BlockSpec((ts, tc), lambda i, j: (i, j)),
            pl.BlockSpec((halo, tc), lambda i, j: (jnp.maximum(i * (ts // halo) - 1, 0), j)),
            pl.BlockSpec((halo, tc), lambda i, j: (jnp.minimum((i + 1) * (ts // halo), nblk8 - 1), j)),
            pl.BlockSpec((3, tc), lambda i, j: (0, j)),
        ],
        out_specs=pl.BlockSpec((ts, tc), lambda i, j: (i, j)),
        out_shape=jax.ShapeDtypeStruct((m, Z_WIDTH), BF16),
        compiler_params=_cparams(("parallel", "parallel")),
        name="token_shift",
    )(z, z, z, w_pad)


def _mask_lanes(x, m0):
    z = jnp.zeros_like(x)
    return jnp.concatenate([jnp.where(m0, x, z), jnp.where(m0, z, x)], axis=0)


def _dot(a, b):
    return jnp.dot(a.astype(BF16), b.astype(BF16), preferred_element_type=F32)


def _dot_nt(a, b):
    return lax.dot_general(a.astype(BF16), b.astype(BF16), (((1,), (1,)), ((), ())),
                           preferred_element_type=F32)


def _dot_tn(a, b):
    return lax.dot_general(a.astype(BF16), b.astype(BF16), (((0,), (0,)), ((), ())),
                           preferred_element_type=F32)


def _head_sums(x, m0):
    s0 = jnp.sum(jnp.where(m0, x, 0.0), axis=-1, keepdims=True)
    s1 = jnp.sum(jnp.where(m0, 0.0, x), axis=-1, keepdims=True)
    return jnp.where(m0, s0, s1)


def _rwkv_kernel(*refs, reverse, n_chunks):
    if reverse:
        (zr_ref, zk_ref, zv_ref, zl_ref, w0_ref, wt_ref, a0_ref, wa_ref, kk_ref, ka_ref, rk_ref,
         y0_ref, g_ref, bon0_ref, lnw_ref, lnb_ref, out_ref,
         s_ref, kk_s, bt_s, kd_s, rt_s, v_s, y_s, ge_s) = refs
    else:
        (zr_ref, zk_ref, zv_ref, zl_ref, w0_ref, wt_ref, a0_ref, wa_ref, kk_ref, ka_ref, rk_ref, wg_ref,
         y_out_ref, g_out_ref, bon_out_ref,
         s_ref, kk_s, bt_s, kd_s, rt_s, v_s, y_s, ge_s) = refs
    ch = RW_CHUNK
    j = pl.program_id(1)

    @pl.when(j == 0)
    def _():
        s_ref[...] = jnp.zeros_like(s_ref)

    zl = zl_ref[...].astype(F32)
    lora_in = zl[:, 0:LANES]
    wl = w0_ref[...] + _dot(jnp.tanh(lora_in), wt_ref[...])
    wlog = -(jnp.maximum(-wl, 0.0) + jnp.log(1.0 + jnp.exp(-jnp.abs(wl)))) - 0.5
    lw = -jnp.exp(wlog)
    a = jax.nn.sigmoid(a0_ref[...] + _dot(lora_in, wa_ref[...]))
    r = zr_ref[...].astype(F32)
    k = zk_ref[...].astype(F32)
    v = zv_ref[...].astype(F32)
    kd = k * (1.0 + (a - 1.0) * ka_ref[...])
    kraw = k * kk_ref[...]
    rkd = r * kd * rk_ref[...]
    if not reverse:
        g_out_ref[...] = _dot(jax.nn.sigmoid(zl[:, LANES:3 * LANES]), wg_ref[...])

    lane = lax.broadcasted_iota(jnp.int32, (1, LANES), 1)
    m0 = lane < B_HEAD
    ti = lax.broadcasted_iota(jnp.int32, (ch, ch), 0)
    si = lax.broadcasted_iota(jnp.int32, (ch, ch), 1)
    tri = ((si >= ti) if reverse else (si <= ti)).astype(BF16)

    kn_l, b_l = [], []
    for p in range(RW_PAIRS):
        cols = slice(p * LANES, (p + 1) * LANES)
        kp = kraw[:, cols]
        nrm = jnp.sqrt(_head_sums(kp * kp, m0))
        kn = kp / jnp.maximum(nrm, 1e-12)
        kn_l.append(kn)
        b_l.append(kn * a[:, cols])
        v_s[p] = v[:, cols]
        if not reverse:
            bon_out_ref[:, cols] = _head_sums(rkd[:, cols], m0) * v[:, cols]

    for c in range(n_chunks):
        rws = slice(c * ch, (c + 1) * ch)
        lw_c = lw[rws, :]
        hi = lw_c.astype(BF16)
        rem = lw_c - hi.astype(F32)
        mid = rem.astype(BF16)
        lo = (rem - mid.astype(F32)).astype(BF16)
        cs = (jnp.dot(tri, hi, preferred_element_type=F32) + jnp.dot(tri, mid, preferred_element_type=F32)
              + jnp.dot(tri, lo, preferred_element_type=F32))
        total = cs[0:1, :] if reverse else cs[ch - 1:ch, :]
        g_in = jnp.exp(cs)
        g_ex = jnp.exp(cs - lw_c)
        g_inv = jnp.exp(-cs)
        g_end = jnp.exp(total)
        for p in range(RW_PAIRS):
            cols = slice(p * LANES, (p + 1) * LANES)
            kk_s[p, rws, :] = kn_l[p][rws, :] * g_ex[:, cols]
            bt_s[p, rws, :] = b_l[p][rws, :] * g_inv[:, cols]
            kd_s[p, rws, :] = kd[rws, cols] * g_inv[:, cols]
            rt_s[p, rws, :] = r[rws, cols] * g_in[:, cols]
            ge_s[p, c:c + 1, :] = g_end[:, cols]

    ri = lax.broadcasted_iota(jnp.int32, (2 * ch, 2 * ch), 0)
    ci = lax.broadcasted_iota(jnp.int32, (2 * ch, 2 * ch), 1)
    same_head = (ri // ch) == (ci // ch)
    if reverse:
        strict = same_head & (ci > ri)
        incl = same_head & (ci >= ri)
    else:
        strict = same_head & (ci < ri)
        incl = same_head & (ci <= ri)
    eye = (ri == ci).astype(F32)
    zero = jnp.zeros((2 * ch, 2 * ch), F32)
    n_sq = int(math.log2(INV_BLOCK)) - 1
    blk_diag = []
    bs = INV_BLOCK
    while bs <= ch:
        blk_diag.append((ri // bs) == (ci // bs))
        bs *= 2
    pairs = range(RW_PAIRS)

    def chunk_body(ci_, carry):
        c = (n_chunks - 1 - ci_) if reverse else ci_
        rows = pl.ds(pl.multiple_of(c * ch, ch), ch)
        kk_t = [kk_s[p, rows, :] for p in pairs]
        rt = [rt_s[p, rows, :] for p in pairs]
        bt_m = [_mask_lanes(bt_s[p, rows, :], m0) for p in pairs]
        kd_m = [_mask_lanes(kd_s[p, rows, :], m0) for p in pairs]
        v_st = [_mask_lanes(v_s[p, rows, :], m0) for p in pairs]
        g_end = [ge_s[p, pl.ds(c, 1), :] for p in pairs]

        pq = [_dot_nt(jnp.concatenate([kk_t[p], kk_t[p], rt[p], rt[p]], axis=0),
                      jnp.concatenate([bt_m[p], kd_m[p]], axis=0)) for p in pairs]
        a_m = [jnp.where(strict, pq[p][0:2 * ch, 0:2 * ch], zero) for p in pairs]
        b_m = [jnp.where(strict, pq[p][0:2 * ch, 2 * ch:4 * ch], zero) for p in pairs]
        m2 = [jnp.where(incl, pq[p][2 * ch:4 * ch, 0:2 * ch], zero) for p in pairs]
        m1 = [jnp.where(incl, pq[p][2 * ch:4 * ch, 2 * ch:4 * ch], zero) for p in pairs]

        q = [jnp.where(blk_diag[0], -a_m[p], zero) for p in pairs]
        t_m = [eye + q[p] for p in pairs]
        q = [_dot(q[p], q[p]) for p in pairs]
        for _ in range(n_sq - 1):
            both = [_dot(jnp.concatenate([q[p], t_m[p]], axis=0), q[p]) for p in pairs]
            q = [both[p][0:2 * ch] for p in pairs]
            t_m = [t_m[p] + both[p][2 * ch:4 * ch] for p in pairs]
        t_m = [t_m[p] + _dot(t_m[p], q[p]) for p in pairs]
        for lvl in range(1, len(blk_diag)):
            off = [jnp.where(blk_diag[lvl] & ~blk_diag[lvl - 1], a_m[p], zero) for p in pairs]
            t_off = [_dot(t_m[p], off[p]) for p in pairs]
            t_m = [t_m[p] - _dot(t_off[p], t_m[p]) for p in pairs]

        bm1v = [_dot(jnp.concatenate([b_m[p], m1[p]], axis=0), v_st[p]) for p in pairs]
        ku = [_dot(t_m[p], jnp.concatenate([_mask_lanes(kk_t[p], m0), bm1v[p][0:2 * ch]], axis=1))
              for p in pairs]
        m2ku = [_dot(m2[p], ku[p]) for p in pairs]
        s_bd = [s_ref[p] for p in pairs]
        rp, y0 = [], []
        for p in pairs:
            rp_s = _mask_lanes(rt[p], m0) - m2ku[p][:, 0:LANES]
            y0_s = bm1v[p][2 * ch:4 * ch] - m2ku[p][:, LANES:2 * LANES]
            rp.append(rp_s[0:ch] + rp_s[ch:2 * ch])
            y0.append(y0_s[0:ch] + y0_s[ch:2 * ch])
        rs = [_dot_nt(jnp.concatenate([rp[p], ku[p][:, 0:LANES]], axis=0), s_bd[p]) for p in pairs]
        ys = [rs[p][0:ch] + y0[p] for p in pairs]
        u_all = [rs[p][ch:3 * ch] + ku[p][:, LANES:2 * LANES] for p in pairs]
        for p in pairs:
            upd = _dot_tn(jnp.concatenate([v_st[p], -u_all[p]], axis=0),
                          jnp.concatenate([kd_m[p] * g_end[p], bt_m[p] * g_end[p]], axis=0))
            s_ref[p] = s_bd[p] * g_end[p] + upd
            y_s[p, rows, :] = ys[p]
        return carry

    lax.fori_loop(0, n_chunks, chunk_body, 0)

    for p in range(RW_PAIRS):
        cols = slice(p * LANES, (p + 1) * LANES)
        if not reverse:
            y_out_ref[:, cols] = y_s[p]
        else:
            bon1 = _head_sums(rkd[:, cols], m0) * v[:, cols]
            ysum = y0_ref[:, cols] + y_s[p]
            mu = _head_sums(ysum, m0) * (1.0 / B_HEAD)
            dlt = ysum - mu
            var = _head_sums(dlt * dlt, m0) * (1.0 / B_HEAD)
            yn = dlt * lax.rsqrt(var + GN_EPS)
            o = (yn * lnw_ref[:, cols] + lnb_ref[:, cols] + bon0_ref[:, cols] + bon1) * g_ref[:, cols]
            out_ref[:, cols] = o.astype(out_ref.dtype)


def rwkv7_bidir(zs, w0, w_up, a0, a_up, g_up, k_k, k_a, r_k, lnx_w, lnx_b, *, batch, seq, ctx_len):
    m = zs.shape[0]
    n_lat = batch * seq
    tb = RW_TILE
    nct, nlt = ctx_len // tb, seq // tb
    steps = nct + nlt

    def row_block(reverse):
        def f(b, j):
            jc = (nct - 1 - j) if reverse else j
            jl = (nlt - 1 - (j - nct)) if reverse else (j - nct)
            return jnp.where(j < nct, (n_lat + b * ctx_len) // tb + jc, b * nlt + jl)
        return f

    def pad_rows(w, top, total):
        return jnp.pad(w, ((top, total - top - w.shape[0]), (0, 0)))

    wg = pad_rows(g_up, 0, 2 * LANES)
    rk = r_k.reshape(1, MIX_B)
    vec = lambda a: a.reshape(1, MIX_B)
    scratch = ([pltpu.VMEM((RW_PAIRS, LANES, LANES), F32)] + [pltpu.VMEM((RW_PAIRS, tb, LANES), F32)] * 6
               + [pltpu.VMEM((RW_PAIRS, 8, LANES), F32)])
    const = lambda shape: pl.BlockSpec(shape, lambda b, j: (0,) * len(shape))

    def call(reverse, extra_in, extra_specs, out_shapes, out_specs, d):
        rb = row_block(reverse)
        wt = pad_rows(w_up[d], 0, LANES)
        wa = pad_rows(a_up[d], DECAY_LORA, LANES)
        in_specs = [
            pl.BlockSpec((tb, MIX_B), lambda b, j: (rb(b, j), 0)),
            pl.BlockSpec((tb, MIX_B), lambda b, j: (rb(b, j), 1)),
            pl.BlockSpec((tb, MIX_B), lambda b, j: (rb(b, j), 2)),
            pl.BlockSpec((tb, 512), lambda b, j: (rb(b, j), 6)),
            const((1, MIX_B)), const((LANES, MIX_B)), const((1, MIX_B)), const((LANES, MIX_B)),
            const((1, MIX_B)), const((1, MIX_B)), const((1, MIX_B)),
        ] + extra_specs
        args = [zs, zs, zs, zs, vec(w0[d]), wt, vec(a0[d]), wa, vec(k_k), vec(k_a), rk] + extra_in
        return pl.pallas_call(
            functools.partial(_rwkv_kernel, reverse=reverse, n_chunks=tb // RW_CHUNK),
            grid=(batch, steps),
            in_specs=in_specs,
            out_specs=out_specs,
            out_shape=out_shapes,
            scratch_shapes=scratch,
            compiler_params=_cparams(("parallel", "arbitrary")),
            name="rwkv7_rev" if reverse else "rwkv7_fwd",
        )(*args)

    rbf = row_block(False)
    tile_f = pl.BlockSpec((tb, MIX_B), lambda b, j: (rbf(b, j), 0))
    y0, g, bon0 = call(False, [wg], [const((2 * LANES, MIX_B))],
                       [jax.ShapeDtypeStruct((m, MIX_B), F32)] * 3, [tile_f] * 3, 0)
    rbr = row_block(True)
    tile_r = pl.BlockSpec((tb, MIX_B), lambda b, j: (rbr(b, j), 0))
    out = call(True, [y0, g, bon0, vec(lnx_w), vec(lnx_b)],
               [tile_r, tile_r, tile_r, const((1, MIX_B)), const((1, MIX_B))],
               jax.ShapeDtypeStruct((m, MIX_B), BF16), tile_r, 1)
    return out


def _half_rms(x, gain, m0):
    ms = _head_sums(x * x, m0) * (1.0 / C_HEAD)
    return x * lax.rsqrt(ms + EPS) * gain


def _rope(x, cos, sin):
    lane = lax.broadcasted_iota(jnp.int32, x.shape, 1)
    first = (lane % ROPE_AXIS) < (ROPE_AXIS // 2)
    half = ROPE_AXIS // 2
    rot = jnp.where(first, -pltpu.roll(x, LANES - half, axis=1), pltpu.roll(x, half, axis=1))
    return x * cos + rot * sin


def _attn_kernel(q_ref, kl_ref, vl_ref, kc_ref, vc_ref, cq_ref, sq_ref, ck_ref, sk_ref, qg_ref, kg_ref,
                 lam_ref, sg_ref, o_ref, k_s, v_s, *, ctx_len, lam_init, q_blk):
    qi = pl.program_id(2)
    lane = lax.broadcasted_iota(jnp.int32, (1, LANES), 1)
    m0 = lane < C_HEAD

    @pl.when(qi == 0)
    def _():
        kc = _half_rms(kc_ref[...].astype(F32), kg_ref[...], m0)
        kl = _rope(_half_rms(kl_ref[...].astype(F32), kg_ref[...], m0), ck_ref[...], sk_ref[...])
        k_s[0:ctx_len, :] = kc.astype(BF16)
        k_s[ctx_len:, :] = kl.astype(BF16)
        v_s[0:ctx_len, :] = vc_ref[...].astype(BF16)
        v_s[ctx_len:, :] = vl_ref[...].astype(BF16)

    lp = lam_ref[...]
    lam = (jnp.exp(jnp.sum(lp[0:1] * lp[1:2], keepdims=True)) - jnp.exp(jnp.sum(lp[2:3] * lp[3:4], keepdims=True))
           + lam_init)
    q = (_rope(_half_rms(q_ref[...].astype(F32), qg_ref[...], m0), cq_ref[...], sq_ref[...])
         * (C_HEAD ** -0.5 * LOG2E))
    keys = k_s[...]
    vals = v_s[...]
    zq = jnp.zeros_like(q)
    q_sub = [jnp.where(m0, q, zq), jnp.where(m0, zq, q)]
    blocks = [(slice(r0, r0 + q_blk), i) for r0 in range(0, q.shape[0], q_blk) for i in (0, 1)]
    lookahead = 2
    scores = [_dot_nt(q_sub[i][rows], keys) for rows, i in blocks[:lookahead]]
    outs = []
    for n, (rows, i) in enumerate(blocks):
        s = scores[n]
        e = jnp.exp2(s - jnp.max(s, axis=-1, keepdims=True))
        w = (lam if i else 1.0) / jnp.sum(e, axis=-1, keepdims=True)
        if n + lookahead < len(blocks):
            rows2, i2 = blocks[n + lookahead]
            scores.append(_dot_nt(q_sub[i2][rows2], keys))
        outs.append(jnp.dot(e.astype(BF16), vals, preferred_element_type=F32) * w)
    for n, r0 in enumerate(range(0, q.shape[0], q_blk)):
        o = outs[2 * n] - outs[2 * n + 1]
        ms = jnp.mean(o * o, axis=-1, keepdims=True)
        o = o * lax.rsqrt(ms + SUBLN_EPS) * sg_ref[...] * (1.0 - lam_init)
        o_ref[r0:r0 + q_blk, :] = o.astype(o_ref.dtype)


def _rope_tables(n):
    rows = n // GRID_W
    row = jnp.broadcast_to(jnp.arange(rows, dtype=F32)[:, None], (rows, GRID_W)).reshape(-1)
    col = jnp.broadcast_to(jnp.arange(GRID_W, dtype=F32)[None, :], (rows, GRID_W)).reshape(-1)
    inv = ROPE_THETA ** (-jnp.arange(0, ROPE_AXIS, 2, dtype=F32) / ROPE_AXIS)
    ar = row[:, None] * inv
    ac = col[:, None] * inv
    ang = jnp.concatenate([ar, ar, ac, ac, ar, ar, ac, ac], axis=-1)
    return jnp.cos(ang), jnp.sin(ang)


def diff_attention(qkv, q_g, k_g, lam_params, subln_g, lam_init, *, batch, seq, ctx_len, tq=1024):
    n_lat = batch * seq
    tq = min(tq, seq)
    cos, sin = _rope_tables(seq)
    nq = seq // tq
    hq, hk, hv = 0, D_MODEL // LANES, 2 * D_MODEL // LANES
    ctx_blk0 = n_lat // ctx_len
    two = lambda a: jnp.concatenate([a, a]).reshape(1, LANES)
    const = lambda shape: pl.BlockSpec(shape, lambda b, h, i: (0,) * len(shape))
    return pl.pallas_call(
        functools.partial(_attn_kernel, ctx_len=ctx_len, lam_init=lam_init, q_blk=min(128, tq)),
        grid=(batch, C_HEADS, nq),
        in_specs=[
            pl.BlockSpec((tq, LANES), lambda b, h, i: (b * nq + i, hq + h)),
            pl.BlockSpec((seq, LANES), lambda b, h, i: (b, hk + h)),
            pl.BlockSpec((seq, LANES), lambda b, h, i: (b, hv + h)),
            pl.BlockSpec((ctx_len, LANES), lambda b, h, i: (ctx_blk0 + b, hk + h)),
            pl.BlockSpec((ctx_len, LANES), lambda b, h, i: (ctx_blk0 + b, hv + h)),
            pl.BlockSpec((tq, LANES), lambda b, h, i: (i, 0)),
            pl.BlockSpec((tq, LANES), lambda b, h, i: (i, 0)),
            const((seq, LANES)), const((seq, LANES)),
            const((1, LANES)), const((1, LANES)), const((4, C_HEAD)), const((1, LANES)),
        ],
        out_specs=pl.BlockSpec((tq, LANES), lambda b, h, i: (b * nq + i, h)),
        out_shape=jax.ShapeDtypeStruct((n_lat, D_MODEL), BF16),
        scratch_shapes=[pltpu.VMEM((ctx_len + seq, LANES), BF16)] * 2,
        compiler_params=_cparams(("parallel", "parallel", "arbitrary")),
        name="diff_attention",
    )(qkv, qkv, qkv, qkv, qkv, cos, sin, cos, sin, two(q_g), two(k_g), lam_params, subln_g.reshape(1, LANES))


def _route_kernel(x_ref, g_ref, mod_ref, rw_ref, xn_ref, idx_ref, gate_ref, *, s_idx, c_idx):
    h = _norm_mod_rows(x_ref[...], g_ref[...], mod_ref[0], s_idx, c_idx)
    xn_ref[...] = h
    logits = jnp.dot(h, rw_ref[...], precision=lax.Precision.HIGHEST, preferred_element_type=F32)
    lane = lax.broadcasted_iota(jnp.int32, logits.shape, 1)
    neg = jnp.float32(-jnp.inf)
    lg = jnp.where(lane < N_EXPERTS, logits, neg)
    m1 = jnp.max(lg, axis=-1, keepdims=True)
    i1 = jnp.min(jnp.where(lg == m1, lane, LANES), axis=-1, keepdims=True)
    lg2 = jnp.where(lane == i1, neg, lg)
    m2 = jnp.max(lg2, axis=-1, keepdims=True)
    i2 = jnp.min(jnp.where(lg2 == m2, lane, LANES), axis=-1, keepdims=True)
    e2 = jnp.exp(m2 - m1)
    g1 = 1.0 / (1.0 + e2)
    g2 = e2 * g1
    idx_ref[...] = jnp.where(lane == 0, i1, jnp.where(lane == 1, i2, 0))
    gate_ref[...] = jnp.where(lane == 0, g1, jnp.where(lane == 1, g2, 0.0))


def route(x, gain, mods, router, s_idx, c_idx, *, seq, batch, tm=256):
    m, d = x.shape
    rw = jnp.pad(router, ((0, 0), (0, LANES - N_EXPERTS)))
    return pl.pallas_call(
        functools.partial(_route_kernel, s_idx=s_idx, c_idx=c_idx),
        grid=(m // tm,),
        in_specs=[
            pl.BlockSpec((tm, d), lambda i: (i, 0)),
            pl.BlockSpec((1, d), lambda i: (0, 0)),
            pl.BlockSpec((1, 6, d), lambda i: ((i * tm) // seq, 0, 0)),
            pl.BlockSpec((d, LANES), lambda i: (0, 0)),
        ],
        out_specs=[pl.BlockSpec((tm, d), lambda i: (i, 0)),
                   pl.BlockSpec((tm, LANES), lambda i: (i, 0)),
                   pl.BlockSpec((tm, LANES), lambda i: (i, 0))],
        out_shape=[jax.ShapeDtypeStruct((m, d), F32),
                   jax.ShapeDtypeStruct((m, LANES), jnp.int32),
                   jax.ShapeDtypeStruct((m, LANES), F32)],
        compiler_params=_cparams(("parallel",)),
        name="moe_route",
    )(x, gain.reshape(1, d), mods, rw)


def _row_copy(src_hbm, dst_ref, src_row, dst_row, sem):
    return pltpu.make_async_copy(src_hbm.at[pl.ds(src_row, 1)], dst_ref.at[pl.ds(dst_row, 1)], sem)


def _gather_kernel(used_ref, idx_ref, nxt_ref, src_hbm, o_ref, buf, sem, *, tg, n_tiles):
    i = pl.program_id(0)
    slot = i % 2

    def issue_tile(ids_ref, s):
        def issue(g, c):
            for u in range(DMA_UNROLL):
                r = g * DMA_UNROLL + u
                _row_copy(src_hbm, buf.at[s], ids_ref[0, 0, r], r, sem.at[s]).start(priority=u % 2)
            return c

        lax.fori_loop(0, tg // DMA_UNROLL, issue, 0)

    @pl.when((i == 0) & (used_ref[0] > 0))
    def _():
        issue_tile(idx_ref, slot)

    @pl.when((i + 1 < n_tiles) & ((i + 1) * tg < used_ref[0]))
    def _():
        issue_tile(nxt_ref, 1 - slot)

    @pl.when(i * tg >= used_ref[0])
    def _():
        o_ref[...] = jnp.zeros_like(o_ref)

    @pl.when(i * tg < used_ref[0])
    def _():
        def drain(r, c):
            _row_copy(src_hbm, buf.at[slot], 0, r, sem.at[slot]).wait()
            return c

        lax.fori_loop(0, tg, drain, 0, unroll=DMA_UNROLL)
        o_ref[...] = buf[slot].astype(o_ref.dtype)


def gather_rows(src, idx, rows_used, *, tg=256):
    mp = idx.shape[0]
    d = src.shape[1]
    n_tiles = mp // tg
    idx3 = idx.reshape(n_tiles, 1, tg)
    return pl.pallas_call(
        functools.partial(_gather_kernel, tg=tg, n_tiles=n_tiles),
        grid_spec=pltpu.PrefetchScalarGridSpec(
            num_scalar_prefetch=1,
            grid=(n_tiles,),
            in_specs=[pl.BlockSpec((1, 1, tg), lambda i, u: (i, 0, 0), memory_space=pltpu.SMEM),
                      pl.BlockSpec((1, 1, tg), lambda i, u: (jnp.minimum(i + 1, n_tiles - 1), 0, 0),
                                   memory_space=pltpu.SMEM),
                      pl.BlockSpec(memory_space=pl.ANY)],
            out_specs=pl.BlockSpec((tg, d), lambda i, u: (i, 0)),
            scratch_shapes=[pltpu.VMEM((2, tg, d), src.dtype), pltpu.SemaphoreType.DMA((2,))],
        ),
        out_shape=jax.ShapeDtypeStruct((mp, d), BF16),
        compiler_params=_cparams(("arbitrary",)),
        name="moe_gather",
    )(rows_used, idx3, idx3, src)


def _gmm_kernel(te_ref, nu_ref, *refs, kind, kc):
    if kind == "swiglu":
        x_ref, w1_ref, w3_ref, o_ref = refs
        w_refs = (w1_ref, w3_ref)
    else:
        x_ref, w1_ref, o_ref = refs
        w_refs = (w1_ref,)
    i = pl.program_id(1)

    @pl.when(i >= nu_ref[0])
    def _():
        o_ref[...] = jnp.zeros_like(o_ref)

    @pl.when(i < nu_ref[0])
    def _():
        kdim = x_ref.shape[1]
        accs = [None] * len(w_refs)
        for k0 in range(0, kdim, kc):
            x = x_ref[:, k0:k0 + kc].astype(BF16)
            for n, w_ref in enumerate(w_refs):
                p = jnp.dot(x, w_ref[0, k0:k0 + kc, :].astype(BF16), preferred_element_type=F32)
                accs[n] = p if accs[n] is None else accs[n] + p
        a = accs[0]
        if kind == "swiglu":
            a = jax.nn.silu(a) * accs[1]
        o_ref[...] = a.astype(o_ref.dtype)


def grouped_matmul(x, w, tile_expert, n_used, *, w3=None, out_dtype=F32, tm=512, tn=512, kc=1024):
    mp, kdim = x.shape
    n = w.shape[2]
    kind = "swiglu" if w3 is not None else "plain"
    w_spec = pl.BlockSpec((1, kdim, tn), lambda j, i, te, nu: (te[i], 0, j))
    in_specs = [pl.BlockSpec((tm, kdim), lambda j, i, te, nu: (i, 0)), w_spec]
    args = [x, w]
    if w3 is not None:
        in_specs.append(w_spec)
        args.append(w3)
    return pl.pallas_call(
        functools.partial(_gmm_kernel, kind=kind, kc=min(kc, kdim)),
        grid_spec=pltpu.PrefetchScalarGridSpec(
            num_scalar_prefetch=2,
            grid=(n // tn, mp // tm),
            in_specs=in_specs,
            out_specs=pl.BlockSpec((tm, tn), lambda j, i, te, nu: (i, j)),
        ),
        out_shape=jax.ShapeDtypeStruct((mp, n), out_dtype),
        compiler_params=pltpu.CompilerParams(dimension_semantics=("arbitrary", "arbitrary"),
                                             vmem_limit_bytes=GMM_VMEM_LIMIT),
        name="moe_gmm_" + kind,
    )(tile_expert, n_used, *args)


def _combine_kernel(pos_ref, nxt_ref, ys_hbm, x_ref, gate_ref, mod_ref, o_ref, buf, sem, *, tc, n_tiles, gate_idx):
    i = pl.program_id(0)
    slot = i % 2

    def issue_tile(p_ref, s):
        def issue(g, c):
            for u in range(DMA_UNROLL):
                r = g * DMA_UNROLL + u
                _row_copy(ys_hbm, buf.at[s, 0], p_ref[0, 0, 2 * r], r, sem.at[s]).start(priority=0)
                _row_copy(ys_hbm, buf.at[s, 1], p_ref[0, 0, 2 * r + 1], r, sem.at[s]).start(priority=1)
            return c

        lax.fori_loop(0, tc // DMA_UNROLL, issue, 0)

    @pl.when(i == 0)
    def _():
        issue_tile(pos_ref, slot)

    @pl.when(i + 1 < n_tiles)
    def _():
        issue_tile(nxt_ref, 1 - slot)

    def drain(r, c):
        _row_copy(ys_hbm, buf.at[slot, 0], 0, r, sem.at[slot]).wait()
        _row_copy(ys_hbm, buf.at[slot, 1], 0, r, sem.at[slot]).wait()
        return c

    lax.fori_loop(0, tc, drain, 0, unroll=DMA_UNROLL)
    g = gate_ref[...]
    moe = g[:, 0:1] * buf[slot, 0] + g[:, 1:2] * buf[slot, 1]
    o_ref[...] = x_ref[...] + mod_ref[0, gate_idx:gate_idx + 1, :] * moe


def moe_combine(ys, pos, x, gates, mods, gate_idx, *, seq, tc=256):
    m, d = x.shape
    n_tiles = m // tc
    pos3 = pos.reshape(n_tiles, 1, 2 * tc)
    return pl.pallas_call(
        functools.partial(_combine_kernel, tc=tc, n_tiles=n_tiles, gate_idx=gate_idx),
        grid=(n_tiles,),
        in_specs=[pl.BlockSpec((1, 1, 2 * tc), lambda i: (i, 0, 0), memory_space=pltpu.SMEM),
                  pl.BlockSpec((1, 1, 2 * tc), lambda i: (jnp.minimum(i + 1, n_tiles - 1), 0, 0),
                               memory_space=pltpu.SMEM),
                  pl.BlockSpec(memory_space=pl.ANY),
                  pl.BlockSpec((tc, d), lambda i: (i, 0)),
                  pl.BlockSpec((tc, LANES), lambda i: (i, 0)),
                  pl.BlockSpec((1, 6, d), lambda i: ((i * tc) // seq, 0, 0))],
        out_specs=pl.BlockSpec((tc, d), lambda i: (i, 0)),
        out_shape=jax.ShapeDtypeStruct((m, d), F32),
        scratch_shapes=[pltpu.VMEM((2, 2, tc, d), F32), pltpu.SemaphoreType.DMA((2,))],
        compiler_params=_cparams(("arbitrary",)),
        name="moe_combine",
    )(pos3, pos3, ys, x, gates, mods)


def moe_layer(x, gain, mods, router, w1, w3, w2, *, seq, batch, tm=512):
    n = x.shape[0]
    xn, idx, gates = route(x, gain, mods, router, 3, 4, seq=seq, batch=batch)
    e_flat = idx[:, 0:2].reshape(-1)
    onehot = (e_flat[:, None] == jnp.arange(N_EXPERTS)[None, :]).astype(jnp.int32)
    ranks = jnp.cumsum(onehot, axis=0) - onehot
    rank = jnp.sum(ranks * onehot, axis=1)
    counts = jnp.sum(onehot, axis=0)
    padded = ((counts + tm - 1) // tm) * tm
    starts = jnp.cumsum(padded) - padded
    pos = starts[e_flat] + rank
    mp = 2 * n + N_EXPERTS * tm
    token_of_row = jnp.zeros((mp,), jnp.int32).at[pos].set(jnp.arange(2 * n, dtype=jnp.int32) // 2)
    n_tiles = mp // tm
    ends = jnp.cumsum(padded)
    tile_start = jnp.arange(n_tiles, dtype=jnp.int32) * tm
    tile_expert = jnp.minimum(jnp.sum((tile_start[:, None] >= ends[None, :]).astype(jnp.int32), axis=1),
                              N_EXPERTS - 1).astype(jnp.int32)
    n_used = (ends[-1] // tm).astype(jnp.int32).reshape(1)

    xs = gather_rows(xn, token_of_row, ends[-1].astype(jnp.int32).reshape(1))
    hs = grouped_matmul(xs, w1, tile_expert, n_used, w3=w3, out_dtype=BF16, tm=tm, tn=1024)
    ys = grouped_matmul(hs, w2, tile_expert, n_used, out_dtype=F32, tm=tm, tn=512)
    return moe_combine(ys, pos.astype(jnp.int32), x, gates, mods, 5, seq=seq)


def kernel(x, c, ctx, c_ctx, l0_ada_w, l0_ada_b, l0_norm1_g, l0_norm2_g, l0_w_in, l0_sgu_norm_g, l0_sgu_w, l0_sgu_b, l0_shift_w, l0_w0, l0_w_up, l0_a0, l0_a_up, l0_g_up, l0_k_k, l0_k_a, l0_r_k, l0_lnx_w, l0_lnx_b, l0_w_out, l0_ffn_w1, l0_ffn_w3, l0_ffn_w2, l1_ada_w, l1_ada_b, l1_norm1_g, l1_norm2_g, l1_w_qkv, l1_q_norm_g, l1_k_norm_g, l1_lam_q1, l1_lam_k1, l1_lam_q2, l1_lam_k2, l1_subln_g, l1_w_out, l1_router, l1_exp_w1, l1_exp_w3, l1_exp_w2):
    batch, seq, d = x.shape
    ctx_len = ctx.shape[1]
    n_lat = batch * seq
    tm = min(1024, seq, batch * ctx_len)
    tm_res = min(256, tm)
    geo = dict(n_lat=n_lat, seq=seq, batch=batch)
    bf = lambda w: w.astype(BF16)

    x_lat = x.reshape(n_lat, d)
    x_ctx = ctx.reshape(batch * ctx_len, d)
    cond = jnp.concatenate([c, c_ctx[None, :], jnp.zeros((16 - batch - 1, d), F32)], axis=0)

    def ada(w, b):
        return matmul(cond, w, bias=b, silu_x=True, tn=1024).reshape(16, 6, d)

    mods = ada(l0_ada_w, l0_ada_b)
    hn = norm_mod(x_lat, l0_norm1_g, mods, 0, 1, tm=min(512, tm), x_ctx=x_ctx, **geo)
    w_z = bf(jnp.pad(l0_w_in[:, 2 * MIX_A:], ((0, 0), (0, Z_WIDTH - B_WIDTH))))
    hp_a = matmul(hn, bf(l0_w_in[:, :2 * MIX_A]), out_dtype=BF16, tm=tm, tn=1024)
    hp_z = matmul(hn, w_z, tm=tm, tn=Z_WIDTH // 4)
    a_out = chunk_sgu(hp_a, l0_sgu_norm_g, l0_sgu_w, l0_sgu_b)
    zs = token_shift(hp_z, l0_shift_w, n_lat=n_lat, seq=seq, ctx_len=ctx_len)
    b_out = rwkv7_bidir(zs, l0_w0, l0_w_up, l0_a0, l0_a_up, l0_g_up, l0_k_k, l0_k_a, l0_r_k, l0_lnx_w, l0_lnx_b,
                        batch=batch, seq=seq, ctx_len=ctx_len)
    mixed = jnp.concatenate([a_out, b_out], axis=1)
    x1, hn = matmul(mixed, bf(l0_w_out), res=x_lat, res_ctx=x_ctx, mods=mods, gate_idx=2, geo=geo, tm=tm_res,
                    tn=d, norm=(l0_norm2_g, 3, 4))
    hff = matmul(hn, bf(l0_ffn_w1), w3=bf(l0_ffn_w3), out_dtype=BF16, tm=tm)
    x2 = matmul(hff, bf(l0_ffn_w2), res=x1, mods=mods, gate_idx=5, geo=geo, tm=tm, tn=1024,
                tk=l0_ffn_w2.shape[0] // 2)

    mods = ada(l1_ada_w, l1_ada_b)
    hn = norm_mod(x2, l1_norm1_g, mods, 0, 1, tm=min(512, tm), **geo)
    qkv = matmul(hn, bf(l1_w_qkv), out_dtype=BF16, tm=tm, tn=1024)
    lam_params = jnp.stack([l1_lam_q1, l1_lam_k1, l1_lam_q2, l1_lam_k2])
    lam_init = 0.8 - 0.6 * math.exp(-0.3 * 1)
    o = diff_attention(qkv, l1_q_norm_g, l1_k_norm_g, lam_params, l1_subln_g, lam_init,
                       batch=batch, seq=seq, ctx_len=ctx_len)
    x3 = matmul(o, bf(l1_w_out), res=x2, mods=mods, gate_idx=2, geo=geo, tm=min(512, tm), tn=d)
    x4 = moe_layer(x3, l1_norm2_g, mods, l1_router, l1_exp_w1, l1_exp_w3, l1_exp_w2, seq=seq, batch=batch)
    return x4.reshape(batch, seq, d)
```

```python
import functools
import math

import jax
import jax.numpy as jnp
from jax import lax
from jax.experimental import pallas as pl
from jax.experimental.pallas import tpu as pltpu

F32 = jnp.float32
BF16 = jnp.bfloat16

D_MODEL = 2048
GRID_W = 64
EPS = 1e-6

MIX_A = 1024
SGU_CHUNK = 128
A_GROUPS = 8
MIX_B = 1024
B_HEAD = 64
B_HEADS = 16
DECAY_LORA = 64
AAA_LORA = 64
GATE_LORA = 160
B_WIDTH = 3 * MIX_B + DECAY_LORA + AAA_LORA + GATE_LORA
GN_EPS = B_HEAD * 1e-5

C_HEADS = 16
C_HEAD = 64
ROPE_AXIS = C_HEAD // 2
ROPE_THETA = 10000.0
SUBLN_EPS = 1e-5

N_EXPERTS = 8

LOG2E = 1.4426950408889634
LANES = 128
SUBLANES = 8
COND_ROWS = 16
VMEM_LIMIT = 52 * 1024 * 1024
GMM_VMEM_LIMIT = 58 * 1024 * 1024

RW_CHUNK = 64
INV_BLOCK = 16
DMA_UNROLL = 8
RW_TILE = 256
RW_PAIRS = B_HEADS // 2
Z_WIDTH = 3584


def _cparams(sem):
    return pltpu.CompilerParams(dimension_semantics=sem, vmem_limit_bytes=VMEM_LIMIT)


def _group_of_tile(i, tm, n_lat, seq, batch):
    return jnp.where(i * tm < n_lat, (i * tm) // seq, batch)


def _norm_mod_rows(x, gain, mod, s_idx, c_idx):
    ms = jnp.mean(x * x, axis=-1, keepdims=True)
    y = x * lax.rsqrt(ms + EPS) * gain
    return y * (1.0 + mod[c_idx:c_idx + 1, :]) + mod[s_idx:s_idx + 1, :]


def _tile_rows(ref, ctx_ref, lat_tiles):
    if ctx_ref is None:
        return ref[...]
    return jnp.where(pl.program_id(0) < lat_tiles, ref[...], ctx_ref[...])


def _split_row_specs(block, lat_tiles, col):
    lat = pl.BlockSpec(block, lambda i, *a: (jnp.minimum(i, lat_tiles - 1), col(*a)))
    ctx = pl.BlockSpec(block, lambda i, *a: (jnp.maximum(i - lat_tiles, 0), col(*a)))
    return [lat, ctx]


def _norm_mod_kernel(*refs, s_idx, c_idx, lat_tiles):
    if lat_tiles is None:
        x_ref, g_ref, mod_ref, o_ref = refs
        c_ref = None
    else:
        x_ref, c_ref, g_ref, mod_ref, o_ref = refs
    x = _tile_rows(x_ref, c_ref, lat_tiles)
    o_ref[...] = _norm_mod_rows(x, g_ref[...], mod_ref[0], s_idx, c_idx).astype(o_ref.dtype)


def norm_mod(x, gain, mods, s_idx, c_idx, *, n_lat, seq, batch, tm=512, x_ctx=None):
    d = x.shape[1]
    m = x.shape[0] + (0 if x_ctx is None else x_ctx.shape[0])
    grp = functools.partial(_group_of_tile, tm=tm, n_lat=n_lat, seq=seq, batch=batch)
    lat_tiles = None if x_ctx is None else n_lat // tm
    if x_ctx is None:
        x_specs, x_args = [pl.BlockSpec((tm, d), lambda i: (i, 0))], [x]
    else:
        x_specs, x_args = _split_row_specs((tm, d), lat_tiles, lambda: 0), [x, x_ctx]
    return pl.pallas_call(
        functools.partial(_norm_mod_kernel, s_idx=s_idx, c_idx=c_idx, lat_tiles=lat_tiles),
        grid=(m // tm,),
        in_specs=x_specs + [
            pl.BlockSpec((1, d), lambda i: (0, 0)),
            pl.BlockSpec((1, 6, d), lambda i: (grp(i), 0, 0)),
        ],
        out_specs=pl.BlockSpec((tm, d), lambda i: (i, 0)),
        out_shape=jax.ShapeDtypeStruct((m, d), BF16),
        compiler_params=_cparams(("parallel",)),
        name="norm_mod",
    )(*x_args, gain.reshape(1, d), mods)


def _mm_kernel(*refs, kind, nk, silu_x, has_bias, gate_idx, lat_tiles, norm_idx):
    refs = list(refs)
    x_ref = refs.pop(0)
    w_refs = [refs.pop(0)]
    if kind == "swiglu":
        w_refs.append(refs.pop(0))
    bias_ref = refs.pop(0) if has_bias else None
    res_ref = res_ctx_ref = mod_ref = gain_ref = hn_ref = None
    if kind == "res":
        res_ref = refs.pop(0)
        if lat_tiles is not None:
            res_ctx_ref = refs.pop(0)
        mod_ref = refs.pop(0)
        if norm_idx is not None:
            gain_ref = refs.pop(0)
    o_ref = refs.pop(0)
    if norm_idx is not None:
        hn_ref = refs.pop(0)
    acc_refs = refs

    x = x_ref[...]
    if silu_x:
        x = jax.nn.silu(x.astype(F32))
    x = x.astype(BF16)
    prods = [jnp.dot(x, w_ref[...].astype(BF16), preferred_element_type=F32) for w_ref in w_refs]

    def epilogue(vals):
        a = vals[0]
        if kind == "swiglu":
            a = jax.nn.silu(a) * vals[1]
        if has_bias:
            a = a + bias_ref[...]
        if kind == "res":
            a = _tile_rows(res_ref, res_ctx_ref, lat_tiles) + mod_ref[0, gate_idx:gate_idx + 1, :] * a
        o_ref[...] = a.astype(o_ref.dtype)
        if norm_idx is not None:
            hn_ref[...] = _norm_mod_rows(a, gain_ref[...], mod_ref[0], *norm_idx).astype(hn_ref.dtype)

    if nk == 1:
        epilogue(prods)
        return

    k = pl.program_id(2)

    @pl.when(k == 0)
    def _():
        for acc, p in zip(acc_refs, prods):
            acc[...] = p

    @pl.when(k > 0)
    def _():
        for acc, p in zip(acc_refs, prods):
            acc[...] += p

    @pl.when(k == nk - 1)
    def _():
        epilogue([acc[...] for acc in acc_refs])


def matmul(x, w, *, w3=None, bias=None, res=None, res_ctx=None, mods=None, gate_idx=0, out_dtype=F32,
           silu_x=False, tm=1024, tn=512, tk=None, geo=None, norm=None):
    m, kdim = x.shape
    n = w.shape[1]
    tm = min(tm, m)
    tk = kdim if tk is None else tk
    nk = kdim // tk
    kind = "swiglu" if w3 is not None else ("res" if res is not None else "plain")
    in_specs = [pl.BlockSpec((tm, tk), lambda i, j, k: (i, k)),
                pl.BlockSpec((tk, tn), lambda i, j, k: (k, j))]
    args = [x, w]
    if w3 is not None:
        in_specs.append(pl.BlockSpec((tk, tn), lambda i, j, k: (k, j)))
        args.append(w3)
    if bias is not None:
        in_specs.append(pl.BlockSpec((1, tn), lambda i, j, k: (0, j)))
        args.append(bias.reshape(1, n))
    lat_tiles = None
    if res is not None:
        if res_ctx is None:
            in_specs.append(pl.BlockSpec((tm, tn), lambda i, j, k: (i, j)))
            args.append(res)
        else:
            lat_tiles = geo["n_lat"] // tm
            in_specs += _split_row_specs((tm, tn), lat_tiles, lambda j, k: j)
            args += [res, res_ctx]
        group = functools.partial(_group_of_tile, tm=tm, **geo)
        in_specs.append(pl.BlockSpec((1, 6, tn), lambda i, j, k: (group(i), 0, j)))
        args.append(mods)
    out_specs = pl.BlockSpec((tm, tn), lambda i, j, k: (i, j))
    out_shape = jax.ShapeDtypeStruct((m, n), out_dtype)
    norm_idx = None
    if norm is not None:
        assert kind == "res" and tn == n
        in_specs.append(pl.BlockSpec((1, n), lambda i, j, k: (0, 0)))
        args.append(norm[0].reshape(1, n))
        norm_idx = (norm[1], norm[2])
        out_specs = [out_specs, pl.BlockSpec((tm, tn), lambda i, j, k: (i, j))]
        out_shape = [out_shape, jax.ShapeDtypeStruct((m, n), BF16)]
    n_acc = 0 if nk == 1 else (2 if w3 is not None else 1)
    return pl.pallas_call(
        functools.partial(_mm_kernel, kind=kind, nk=nk, silu_x=silu_x, has_bias=bias is not None,
                          gate_idx=gate_idx, lat_tiles=lat_tiles, norm_idx=norm_idx),
        grid=(m // tm, n // tn, nk),
        in_specs=in_specs,
        out_specs=out_specs,
        out_shape=out_shape,
        scratch_shapes=[pltpu.VMEM((tm, tn), F32)] * n_acc,
        compiler_params=_cparams(("parallel", "parallel", "arbitrary")),
        name="matmul_" + kind,
    )(*args)


def _sgu_kernel(u_ref, v_ref, ng_ref, ws_ref, bs_ref, o_ref, *, chunks):
    for c in range(chunks):
        rows = slice(c * SGU_CHUNK, (c + 1) * SGU_CHUNK)
        for g in range(A_GROUPS):
            cols = slice(g * LANES, (g + 1) * LANES)
            vg = jax.nn.gelu(v_ref[rows, cols].astype(F32))
            ms = jnp.mean(vg * vg, axis=-1, keepdims=True)
            vn = vg * lax.rsqrt(ms + EPS) * ng_ref[:, cols]
            s = jnp.dot(ws_ref[g].astype(BF16), vn.astype(BF16), preferred_element_type=F32)
            u = jax.nn.gelu(u_ref[rows, cols].astype(F32))
            o_ref[rows, cols] = (u * (s + bs_ref[:, cols])).astype(o_ref.dtype)


def chunk_sgu(hp, norm_g, ws, bs, *, tm=256):
    m = hp.shape[0]
    bs_exp = jnp.repeat(bs.T, LANES, axis=1)
    return pl.pallas_call(
        functools.partial(_sgu_kernel, chunks=tm // SGU_CHUNK),
        grid=(m // tm,),
        in_specs=[
            pl.BlockSpec((tm, MIX_A), lambda i: (i, 0)),
            pl.BlockSpec((tm, MIX_A), lambda i: (i, 1)),
            pl.BlockSpec((1, MIX_A), lambda i: (0, 0)),
            pl.BlockSpec((A_GROUPS, SGU_CHUNK, SGU_CHUNK), lambda i: (0, 0, 0)),
            pl.BlockSpec((SGU_CHUNK, MIX_A), lambda i: (0, 0)),
        ],
        out_specs=pl.BlockSpec((tm, MIX_A), lambda i: (i, 0)),
        out_shape=jax.ShapeDtypeStruct((m, MIX_A), BF16),
        compiler_params=_cparams(("parallel",)),
        name="chunk_sgu",
    )(hp, hp, norm_g.reshape(1, MIX_A), ws, bs_exp)


def _shift_kernel(x_ref, p_ref, n_ref, w_ref, o_ref, *, ts, lat_tiles, tiles_per_lat, tiles_per_ctx):
    i = pl.program_id(0)
    is_lat = i < lat_tiles
    pos = jnp.where(is_lat, i % tiles_per_lat, (i - lat_tiles) % tiles_per_ctx)
    last = jnp.where(is_lat, tiles_per_lat - 1, tiles_per_ctx - 1)
    x = x_ref[...]
    prev_row = jnp.where(pos == 0, 0.0, p_ref[SUBLANES - 1:SUBLANES, :])
    next_row = jnp.where(pos == last, 0.0, n_ref[0:1, :])
    rows = lax.broadcasted_iota(jnp.int32, x.shape, 0)
    xm = jnp.where(rows == 0, prev_row, pltpu.roll(x, 1, axis=0))
    xp = jnp.where(rows == ts - 1, next_row, pltpu.roll(x, ts - 1, axis=0))
    o_ref[...] = (w_ref[0:1, :] * xm + w_ref[1:2, :] * x + w_ref[2:3, :] * xp).astype(o_ref.dtype)


def token_shift(z, shift_w, *, n_lat, seq, ctx_len, ts=256, tc=Z_WIDTH):
    m = z.shape[0]
    halo = SUBLANES
    nblk8 = m // halo
    w_pad = jnp.pad(shift_w, ((0, 0), (0, Z_WIDTH - B_WIDTH)))
    kern = functools.partial(_shift_kernel, ts=ts, lat_tiles=n_lat // ts, tiles_per_lat=seq // ts,
                             tiles_per_ctx=ctx_len // ts)
    return pl.pallas_call(
        kern,
        grid=(m // ts, Z_WIDTH // tc),
        in_specs=[
            pl.BlockSpec((ts, tc), lambda i, j: (i, j)),
            pl.BlockSpec((halo, tc), lambda i, j: (jnp.maximum(i * (ts // halo) - 1, 0), j)),
            pl.BlockSpec((halo, tc), lambda i, j: (jnp.minimum((i + 1) * (ts // halo), nblk8 - 1), j)),
            pl.BlockSpec((3, tc), lambda i, j: (0, j)),
        ],
        out_specs=pl.BlockSpec((ts, tc), lambda i, j: (i, j)),
        out_shape=jax.ShapeDtypeStruct((m, Z_WIDTH), BF16),
        compiler_params=_cparams(("parallel", "parallel")),
        name="token_shift",
    )(z, z, z, w_pad)


def _mask_lanes(x, m0):
    z = jnp.zeros_like(x)
    return jnp.concatenate([jnp.where(m0, x, z), jnp.where(m0, z, x)], axis=0)


def _dot(a, b):
    return jnp.dot(a.astype(BF16), b.astype(BF16), preferred_element_type=F32)


def _dot_nt(a, b):
    return lax.dot_general(a.astype(BF16), b.astype(BF16), (((1,), (1,)), ((), ())),
                           preferred_element_type=F32)


def _dot_tn(a, b):
    return lax.dot_general(a.astype(BF16), b.astype(BF16), (((0,), (0,)), ((), ())),
                           preferred_element_type=F32)


def _head_sums(x, m0):
    s0 = jnp.sum(jnp.where(m0, x, 0.0), axis=-1, keepdims=True)
    s1 = jnp.sum(jnp.where(m0, 0.0, x), axis=-1, keepdims=True)
    return jnp.where(m0, s0, s1)


def _rwkv_kernel(*refs, reverse, n_chunks):
    if reverse:
        (zr_ref, zk_ref, zv_ref, zl_ref, w0_ref, wt_ref, a0_ref, wa_ref, kk_ref, ka_ref, rk_ref,
         y0_ref, g_ref, bon0_ref, lnw_ref, lnb_ref, out_ref,
         s_ref, kk_s, bt_s, kd_s, rt_s, v_s, y_s, ge_s) = refs
    else:
        (zr_ref, zk_ref, zv_ref, zl_ref, w0_ref, wt_ref, a0_ref, wa_ref, kk_ref, ka_ref, rk_ref, wg_ref,
         y_out_ref, g_out_ref, bon_out_ref,
         s_ref, kk_s, bt_s, kd_s, rt_s, v_s, y_s, ge_s) = refs
    ch = RW_CHUNK
    j = pl.program_id(1)

    @pl.when(j == 0)
    def _():
        s_ref[...] = jnp.zeros_like(s_ref)

    zl = zl_ref[...].astype(F32)
    lora_in = zl[:, 0:LANES]
    wl = w0_ref[...] + _dot(jnp.tanh(lora_in), wt_ref[...])
    wlog = -(jnp.maximum(-wl, 0.0) + jnp.log(1.0 + jnp.exp(-jnp.abs(wl)))) - 0.5
    lw = -jnp.exp(wlog)
    a = jax.nn.sigmoid(a0_ref[...] + _dot(lora_in, wa_ref[...]))
    r = zr_ref[...].astype(F32)
    k = zk_ref[...].astype(F32)
    v = zv_ref[...].astype(F32)
    kd = k * (1.0 + (a - 1.0) * ka_ref[...])
    kraw = k * kk_ref[...]
    rkd = r * kd * rk_ref[...]
    if not reverse:
        g_out_ref[...] = _dot(jax.nn.sigmoid(zl[:, LANES:3 * LANES]), wg_ref[...])

    lane = lax.broadcasted_iota(jnp.int32, (1, LANES), 1)
    m0 = lane < B_HEAD
    ti = lax.broadcasted_iota(jnp.int32, (ch, ch), 0)
    si = lax.broadcasted_iota(jnp.int32, (ch, ch), 1)
    tri = ((si >= ti) if reverse else (si <= ti)).astype(BF16)

    kn_l, b_l = [], []
    for p in range(RW_PAIRS):
        cols = slice(p * LANES, (p + 1) * LANES)
        kp = kraw[:, cols]
        nrm = jnp.sqrt(_head_sums(kp * kp, m0))
        kn = kp / jnp.maximum(nrm, 1e-12)
        kn_l.append(kn)
        b_l.append(kn * a[:, cols])
        v_s[p] = v[:, cols]
        if not reverse:
            bon_out_ref[:, cols] = _head_sums(rkd[:, cols], m0) * v[:, cols]

    for c in range(n_chunks):
        rws = slice(c * ch, (c + 1) * ch)
        lw_c = lw[rws, :]
        hi = lw_c.astype(BF16)
        rem = lw_c - hi.astype(F32)
        mid = rem.astype(BF16)
        lo = (rem - mid.astype(F32)).astype(BF16)
        cs = (jnp.dot(tri, hi, preferred_element_type=F32) + jnp.dot(tri, mid, preferred_element_type=F32)
              + jnp.dot(tri, lo, preferred_element_type=F32))
        total = cs[0:1, :] if reverse else cs[ch - 1:ch, :]
        g_in = jnp.exp(cs)
        g_ex = jnp.exp(cs - lw_c)
        g_inv = jnp.exp(-cs)
        g_end = jnp.exp(total)
        for p in range(RW_PAIRS):
            cols = slice(p * LANES, (p + 1) * LANES)
            kk_s[p, rws, :] = kn_l[p][rws, :] * g_ex[:, cols]
            bt_s[p, rws, :] = b_l[p][rws, :] * g_inv[:, cols]
            kd_s[p, rws, :] = kd[rws, cols] * g_inv[:, cols]
            rt_s[p, rws, :] = r[rws, cols] * g_in[:, cols]
            ge_s[p, c:c + 1, :] = g_end[:, cols]

    ri = lax.broadcasted_iota(jnp.int32, (2 * ch, 2 * ch), 0)
    ci = lax.broadcasted_iota(jnp.int32, (2 * ch, 2 * ch), 1)
    same_head = (ri // ch) == (ci // ch)
    if reverse:
        strict = same_head & (ci > ri)
        incl = same_head & (ci >= ri)
    else:
        strict = same_head & (ci < ri)
        incl = same_head & (ci <= ri)
    eye = (ri == ci).astype(F32)
    zero = jnp.zeros((2 * ch, 2 * ch), F32)
    n_sq = int(math.log2(INV_BLOCK)) - 1
    blk_diag = []
    bs = INV_BLOCK
    while bs <= ch:
        blk_diag.append((ri // bs) == (ci // bs))
        bs *= 2
    pairs = range(RW_PAIRS)

    def chunk_body(ci_, carry):
        c = (n_chunks - 1 - ci_) if reverse else ci_
        rows = pl.ds(pl.multiple_of(c * ch, ch), ch)
        kk_t = [kk_s[p, rows, :] for p in pairs]
        rt = [rt_s[p, rows, :] for p in pairs]
        bt_m = [_mask_lanes(bt_s[p, rows, :], m0) for p in pairs]
        kd_m = [_mask_lanes(kd_s[p, rows, :], m0) for p in pairs]
        v_st = [_mask_lanes(v_s[p, rows, :], m0) for p in pairs]
        g_end = [ge_s[p, pl.ds(c, 1), :] for p in pairs]

        pq = [_dot_nt(jnp.concatenate([kk_t[p], kk_t[p], rt[p], rt[p]], axis=0),
                      jnp.concatenate([bt_m[p], kd_m[p]], axis=0)) for p in pairs]
        a_m = [jnp.where(strict, pq[p][0:2 * ch, 0:2 * ch], zero) for p in pairs]
        b_m = [jnp.where(strict, pq[p][0:2 * ch, 2 * ch:4 * ch], zero) for p in pairs]
        m2 = [jnp.where(incl, pq[p][2 * ch:4 * ch, 0:2 * ch], zero) for p in pairs]
        m1 = [jnp.where(incl, pq[p][2 * ch:4 * ch, 2 * ch:4 * ch], zero) for p in pairs]

        q = [jnp.where(blk_diag[0], -a_m[p], zero) for p in pairs]
        t_m = [eye + q[p] for p in pairs]
        q = [_dot(q[p], q[p]) for p in pairs]
        for _ in range(n_sq - 1):
            both = [_dot(jnp.concatenate([q[p], t_m[p]], axis=0), q[p]) for p in pairs]
            q = [both[p][0:2 * ch] for p in pairs]
            t_m = [t_m[p] + both[p][2 * ch:4 * ch] for p in pairs]
        t_m = [t_m[p] + _dot(t_m[p], q[p]) for p in pairs]
        for lvl in range(1, len(blk_diag)):
            off = [jnp.where(blk_diag[lvl] & ~blk_diag[lvl - 1], a_m[p], zero) for p in pairs]
            t_off = [_dot(t_m[p], off[p]) for p in pairs]
            t_m = [t_m[p] - _dot(t_off[p], t_m[p]) for p in pairs]

        bm1v = [_dot(jnp.concatenate([b_m[p], m1[p]], axis=0), v_st[p]) for p in pairs]
        ku = [_dot(t_m[p], jnp.concatenate([_mask_lanes(kk_t[p], m0), bm1v[p][0:2 * ch]], axis=1))
              for p in pairs]
        m2ku = [_dot(m2[p], ku[p]) for p in pairs]
        s_bd = [s_ref[p] for p in pairs]
        rp, y0 = [], []
        for p in pairs:
            rp_s = _mask_lanes(rt[p], m0) - m2ku[p][:, 0:LANES]
            y0_s = bm1v[p][2 * ch:4 * ch] - m2ku[p][:, LANES:2 * LANES]
            rp.append(rp_s[0:ch] + rp_s[ch:2 * ch])
            y0.append(y0_s[0:ch] + y0_s[ch:2 * ch])
        rs = [_dot_nt(jnp.concatenate([rp[p], ku[p][:, 0:LANES]], axis=0), s_bd[p]) for p in pairs]
        ys = [rs[p][0:ch] + y0[p] for p in pairs]
        u_all = [rs[p][ch:3 * ch] + ku[p][:, LANES:2 * LANES] for p in pairs]
        for p in pairs:
            upd = _dot_tn(jnp.concatenate([v_st[p], -u_all[p]], axis=0),
                          jnp.concatenate([kd_m[p] * g_end[p], bt_m[p] * g_end[p]], axis=0))
            s_ref[p] = s_bd[p] * g_end[p] + upd
            y_s[p, rows, :] = ys[p]
        return carry

    lax.fori_loop(0, n_chunks, chunk_body, 0)

    for p in range(RW_PAIRS):
        cols = slice(p * LANES, (p + 1) * LANES)
        if not reverse:
            y_out_ref[:, cols] = y_s[p]
        else:
            bon1 = _head_sums(rkd[:, cols], m0) * v[:, cols]
            ysum = y0_ref[:, cols] + y_s[p]
            mu = _head_sums(ysum, m0) * (1.0 / B_HEAD)
            dlt = ysum - mu
            var = _head_sums(dlt * dlt, m0) * (1.0 / B_HEAD)
            yn = dlt * lax.rsqrt(var + GN_EPS)
            o = (yn * lnw_ref[:, cols] + lnb_ref[:, cols] + bon0_ref[:, cols] + bon1) * g_ref[:, cols]
            out_ref[:, cols] = o.astype(out_ref.dtype)


def rwkv7_bidir(zs, w0, w_up, a0, a_up, g_up, k_k, k_a, r_k, lnx_w, lnx_b, *, batch, seq, ctx_len):
    m = zs.shape[0]
    n_lat = batch * seq
    tb = RW_TILE
    assert seq % tb == 0 and ctx_len % tb == 0 and tb // RW_CHUNK <= SUBLANES
    nct, nlt = ctx_len // tb, seq // tb
    steps = nct + nlt

    def row_block(reverse):
        def f(b, j):
            jc = (nct - 1 - j) if reverse else j
            jl = (nlt - 1 - (j - nct)) if reverse else (j - nct)
            return jnp.where(j < nct, (n_lat + b * ctx_len) // tb + jc, b * nlt + jl)
        return f

    def pad_rows(w, top, total):
        return jnp.pad(w, ((top, total - top - w.shape[0]), (0, 0)))

    wg = pad_rows(g_up, 0, 2 * LANES)
    rk = r_k.reshape(1, MIX_B)
    vec = lambda a: a.reshape(1, MIX_B)
    scratch = ([pltpu.VMEM((RW_PAIRS, LANES, LANES), F32)] + [pltpu.VMEM((RW_PAIRS, tb, LANES), F32)] * 6
               + [pltpu.VMEM((RW_PAIRS, SUBLANES, LANES), F32)])
    const = lambda shape: pl.BlockSpec(shape, lambda b, j: (0,) * len(shape))

    def call(reverse, extra_in, extra_specs, out_shapes, out_specs, d):
        rb = row_block(reverse)
        wt = pad_rows(w_up[d], 0, LANES)
        wa = pad_rows(a_up[d], DECAY_LORA, LANES)
        in_specs = [
            pl.BlockSpec((tb, MIX_B), lambda b, j: (rb(b, j), 0)),
            pl.BlockSpec((tb, MIX_B), lambda b, j: (rb(b, j), 1)),
            pl.BlockSpec((tb, MIX_B), lambda b, j: (rb(b, j), 2)),
            pl.BlockSpec((tb, 512), lambda b, j: (rb(b, j), 6)),
            const((1, MIX_B)), const((LANES, MIX_B)), const((1, MIX_B)), const((LANES, MIX_B)),
            const((1, MIX_B)), const((1, MIX_B)), const((1, MIX_B)),
        ] + extra_specs
        args = [zs, zs, zs, zs, vec(w0[d]), wt, vec(a0[d]), wa, vec(k_k), vec(k_a), rk] + extra_in
        return pl.pallas_call(
            functools.partial(_rwkv_kernel, reverse=reverse, n_chunks=tb // RW_CHUNK),
            grid=(batch, steps),
            in_specs=in_specs,
            out_specs=out_specs,
            out_shape=out_shapes,
            scratch_shapes=scratch,
            compiler_params=_cparams(("parallel", "arbitrary")),
            name="rwkv7_rev" if reverse else "rwkv7_fwd",
        )(*args)

    rbf = row_block(False)
    tile_f = pl.BlockSpec((tb, MIX_B), lambda b, j: (rbf(b, j), 0))
    y0, g, bon0 = call(False, [wg], [const((2 * LANES, MIX_B))],
                       [jax.ShapeDtypeStruct((m, MIX_B), F32)] * 3, [tile_f] * 3, 0)
    rbr = row_block(True)
    tile_r = pl.BlockSpec((tb, MIX_B), lambda b, j: (rbr(b, j), 0))
    out = call(True, [y0, g, bon0, vec(lnx_w), vec(lnx_b)],
               [tile_r, tile_r, tile_r, const((1, MIX_B)), const((1, MIX_B))],
               jax.ShapeDtypeStruct((m, MIX_B), BF16), tile_r, 1)
    return out


def _half_rms(x, gain, m0):
    ms = _head_sums(x * x, m0) * (1.0 / C_HEAD)
    return x * lax.rsqrt(ms + EPS) * gain


def _rope(x, cos, sin):
    lane = lax.broadcasted_iota(jnp.int32, x.shape, 1)
    first = (lane % ROPE_AXIS) < (ROPE_AXIS // 2)
    half = ROPE_AXIS // 2
    rot = jnp.where(first, -pltpu.roll(x, LANES - half, axis=1), pltpu.roll(x, half, axis=1))
    return x * cos + rot * sin


def _attn_kernel(q_ref, kl_ref, vl_ref, kc_ref, vc_ref, cq_ref, sq_ref, ck_ref, sk_ref, qg_ref, kg_ref,
                 lam_ref, sg_ref, o_ref, k_s, v_s, *, ctx_len, lam_init, q_blk):
    qi = pl.program_id(2)
    lane = lax.broadcasted_iota(jnp.int32, (1, LANES), 1)
    m0 = lane < C_HEAD

    @pl.when(qi == 0)
    def _():
        kc = _half_rms(kc_ref[...].astype(F32), kg_ref[...], m0)
        kl = _rope(_half_rms(kl_ref[...].astype(F32), kg_ref[...], m0), ck_ref[...], sk_ref[...])
        k_s[0:ctx_len, :] = kc.astype(BF16)
        k_s[ctx_len:, :] = kl.astype(BF16)
        v_s[0:ctx_len, :] = vc_ref[...].astype(BF16)
        v_s[ctx_len:, :] = vl_ref[...].astype(BF16)

    lp = lam_ref[...]
    lam = (jnp.exp(jnp.sum(lp[0:1] * lp[1:2], keepdims=True)) - jnp.exp(jnp.sum(lp[2:3] * lp[3:4], keepdims=True))
           + lam_init)
    q = (_rope(_half_rms(q_ref[...].astype(F32), qg_ref[...], m0), cq_ref[...], sq_ref[...])
         * (C_HEAD ** -0.5 * LOG2E))
    keys = k_s[...]
    vals = v_s[...]
    zq = jnp.zeros_like(q)
    q_sub = [jnp.where(m0, q, zq), jnp.where(m0, zq, q)]
    blocks = [(slice(r0, r0 + q_blk), i) for r0 in range(0, q.shape[0], q_blk) for i in (0, 1)]
    lookahead = 2
    scores = [_dot_nt(q_sub[i][rows], keys) for rows, i in blocks[:lookahead]]
    outs = []
    for n, (rows, i) in enumerate(blocks):
        s = scores[n]
        e = jnp.exp2(s - jnp.max(s, axis=-1, keepdims=True))
        w = (lam if i else 1.0) / jnp.sum(e, axis=-1, keepdims=True)
        if n + lookahead < len(blocks):
            rows2, i2 = blocks[n + lookahead]
            scores.append(_dot_nt(q_sub[i2][rows2], keys))
        outs.append(jnp.dot(e.astype(BF16), vals, preferred_element_type=F32) * w)
    for n, r0 in enumerate(range(0, q.shape[0], q_blk)):
        o = outs[2 * n] - outs[2 * n + 1]
        ms = jnp.mean(o * o, axis=-1, keepdims=True)
        o = o * lax.rsqrt(ms + SUBLN_EPS) * sg_ref[...] * (1.0 - lam_init)
        o_ref[r0:r0 + q_blk, :] = o.astype(o_ref.dtype)


def _rope_tables(n):
    rows = n // GRID_W
    row = jnp.broadcast_to(jnp.arange(rows, dtype=F32)[:, None], (rows, GRID_W)).reshape(-1)
    col = jnp.broadcast_to(jnp.arange(GRID_W, dtype=F32)[None, :], (rows, GRID_W)).reshape(-1)
    inv = ROPE_THETA ** (-jnp.arange(0, ROPE_AXIS, 2, dtype=F32) / ROPE_AXIS)
    ar = row[:, None] * inv
    ac = col[:, None] * inv
    ang = jnp.concatenate([ar, ar, ac, ac, ar, ar, ac, ac], axis=-1)
    return jnp.cos(ang), jnp.sin(ang)


def diff_attention(qkv, q_g, k_g, lam_params, subln_g, lam_init, *, batch, seq, ctx_len, tq=1024):
    n_lat = batch * seq
    tq = min(tq, seq)
    cos, sin = _rope_tables(seq)
    nq = seq // tq
    hq, hk, hv = 0, D_MODEL // LANES, 2 * D_MODEL // LANES
    ctx_blk0 = n_lat // ctx_len
    two = lambda a: jnp.concatenate([a, a]).reshape(1, LANES)
    const = lambda shape: pl.BlockSpec(shape, lambda b, h, i: (0,) * len(shape))
    return pl.pallas_call(
        functools.partial(_attn_kernel, ctx_len=ctx_len, lam_init=lam_init, q_blk=min(128, tq)),
        grid=(batch, C_HEADS, nq),
        in_specs=[
            pl.BlockSpec((tq, LANES), lambda b, h, i: (b * nq + i, hq + h)),
            pl.BlockSpec((seq, LANES), lambda b, h, i: (b, hk + h)),
            pl.BlockSpec((seq, LANES), lambda b, h, i: (b, hv + h)),
            pl.BlockSpec((ctx_len, LANES), lambda b, h, i: (ctx_blk0 + b, hk + h)),
            pl.BlockSpec((ctx_len, LANES), lambda b, h, i: (ctx_blk0 + b, hv + h)),
            pl.BlockSpec((tq, LANES), lambda b, h, i: (i, 0)),
            pl.BlockSpec((tq, LANES), lambda b, h, i: (i, 0)),
            const((seq, LANES)), const((seq, LANES)),
            const((1, LANES)), const((1, LANES)), const((4, C_HEAD)), const((1, LANES)),
        ],
        out_specs=pl.BlockSpec((tq, LANES), lambda b, h, i: (b * nq + i, h)),
        out_shape=jax.ShapeDtypeStruct((n_lat, D_MODEL), BF16),
        scratch_shapes=[pltpu.VMEM((ctx_len + seq, LANES), BF16)] * 2,
        compiler_params=_cparams(("parallel", "parallel", "arbitrary")),
        name="diff_attention",
    )(qkv, qkv, qkv, qkv, qkv, cos, sin, cos, sin, two(q_g), two(k_g), lam_params, subln_g.reshape(1, LANES))


def _route_kernel(x_ref, g_ref, mod_ref, rw_ref, xn_ref, idx_ref, gate_ref, *, s_idx, c_idx):
    h = _norm_mod_rows(x_ref[...], g_ref[...], mod_ref[0], s_idx, c_idx)
    xn_ref[...] = h
    logits = jnp.dot(h, rw_ref[...], precision=lax.Precision.HIGHEST, preferred_element_type=F32)
    lane = lax.broadcasted_iota(jnp.int32, logits.shape, 1)
    neg = jnp.float32(-jnp.inf)
    lg = jnp.where(lane < N_EXPERTS, logits, neg)
    m1 = jnp.max(lg, axis=-1, keepdims=True)
    i1 = jnp.min(jnp.where(lg == m1, lane, LANES), axis=-1, keepdims=True)
    lg2 = jnp.where(lane == i1, neg, lg)
    m2 = jnp.max(lg2, axis=-1, keepdims=True)
    i2 = jnp.min(jnp.where(lg2 == m2, lane, LANES), axis=-1, keepdims=True)
    e2 = jnp.exp(m2 - m1)
    g1 = 1.0 / (1.0 + e2)
    g2 = e2 * g1
    idx_ref[...] = jnp.where(lane == 0, i1, jnp.where(lane == 1, i2, 0))
    gate_ref[...] = jnp.where(lane == 0, g1, jnp.where(lane == 1, g2, 0.0))


def route(x, gain, mods, router, s_idx, c_idx, *, seq, batch, tm=256):
    m, d = x.shape
    rw = jnp.pad(router, ((0, 0), (0, LANES - N_EXPERTS)))
    return pl.pallas_call(
        functools.partial(_route_kernel, s_idx=s_idx, c_idx=c_idx),
        grid=(m // tm,),
        in_specs=[
            pl.BlockSpec((tm, d), lambda i: (i, 0)),
            pl.BlockSpec((1, d), lambda i: (0, 0)),
            pl.BlockSpec((1, 6, d), lambda i: ((i * tm) // seq, 0, 0)),
            pl.BlockSpec((d, LANES), lambda i: (0, 0)),
        ],
        out_specs=[pl.BlockSpec((tm, d), lambda i: (i, 0)),
                   pl.BlockSpec((tm, LANES), lambda i: (i, 0)),
                   pl.BlockSpec((tm, LANES), lambda i: (i, 0))],
        out_shape=[jax.ShapeDtypeStruct((m, d), F32),
                   jax.ShapeDtypeStruct((m, LANES), jnp.int32),
                   jax.ShapeDtypeStruct((m, LANES), F32)],
        compiler_params=_cparams(("parallel",)),
        name="moe_route",
    )(x, gain.reshape(1, d), mods, rw)


def _row_copy(src_hbm, dst_ref, src_row, dst_row, sem):
    return pltpu.make_async_copy(src_hbm.at[pl.ds(src_row, 1)], dst_ref.at[pl.ds(dst_row, 1)], sem)


def _gather_kernel(used_ref, idx_ref, nxt_ref, src_hbm, o_ref, buf, sem, *, tg, n_tiles):
    i = pl.program_id(0)
    slot = i % 2

    def issue_tile(ids_ref, s):
        def issue(g, c):
            for u in range(DMA_UNROLL):
                r = g * DMA_UNROLL + u
                _row_copy(src_hbm, buf.at[s], ids_ref[0, 0, r], r, sem.at[s]).start(priority=u % 2)
            return c

        lax.fori_loop(0, tg // DMA_UNROLL, issue, 0)

    @pl.when((i == 0) & (used_ref[0] > 0))
    def _():
        issue_tile(idx_ref, slot)

    @pl.when((i + 1 < n_tiles) & ((i + 1) * tg < used_ref[0]))
    def _():
        issue_tile(nxt_ref, 1 - slot)

    @pl.when(i * tg >= used_ref[0])
    def _():
        o_ref[...] = jnp.zeros_like(o_ref)

    @pl.when(i * tg < used_ref[0])
    def _():
        def drain(r, c):
            _row_copy(src_hbm, buf.at[slot], 0, r, sem.at[slot]).wait()
            return c

        lax.fori_loop(0, tg, drain, 0, unroll=DMA_UNROLL)
        o_ref[...] = buf[slot].astype(o_ref.dtype)


def gather_rows(src, idx, rows_used, *, tg=256):
    mp = idx.shape[0]
    d = src.shape[1]
    n_tiles = mp // tg
    idx3 = idx.reshape(n_tiles, 1, tg)
    return pl.pallas_call(
        functools.partial(_gather_kernel, tg=tg, n_tiles=n_tiles),
        grid_spec=pltpu.PrefetchScalarGridSpec(
            num_scalar_prefetch=1,
            grid=(n_tiles,),
            in_specs=[pl.BlockSpec((1, 1, tg), lambda i, u: (i, 0, 0), memory_space=pltpu.SMEM),
                      pl.BlockSpec((1, 1, tg), lambda i, u: (jnp.minimum(i + 1, n_tiles - 1), 0, 0),
                                   memory_space=pltpu.SMEM),
                      pl.BlockSpec(memory_space=pl.ANY)],
            out_specs=pl.BlockSpec((tg, d), lambda i, u: (i, 0)),
            scratch_shapes=[pltpu.VMEM((2, tg, d), src.dtype), pltpu.SemaphoreType.DMA((2,))],
        ),
        out_shape=jax.ShapeDtypeStruct((mp, d), BF16),
        compiler_params=_cparams(("arbitrary",)),
        name="moe_gather",
    )(rows_used, idx3, idx3, src)


def _gmm_kernel(te_ref, nu_ref, *refs, kind, kc):
    if kind == "swiglu":
        x_ref, w1_ref, w3_ref, o_ref = refs
        w_refs = (w1_ref, w3_ref)
    else:
        x_ref, w1_ref, o_ref = refs
        w_refs = (w1_ref,)
    i = pl.program_id(1)

    @pl.when(i >= nu_ref[0])
    def _():
        o_ref[...] = jnp.zeros_like(o_ref)

    @pl.when(i < nu_ref[0])
    def _():
        kdim = x_ref.shape[1]
        accs = [None] * len(w_refs)
        for k0 in range(0, kdim, kc):
            x = x_ref[:, k0:k0 + kc].astype(BF16)
            for n, w_ref in enumerate(w_refs):
                p = jnp.dot(x, w_ref[0, k0:k0 + kc, :].astype(BF16), preferred_element_type=F32)
                accs[n] = p if accs[n] is None else accs[n] + p
        a = accs[0]
        if kind == "swiglu":
            a = jax.nn.silu(a) * accs[1]
        o_ref[...] = a.astype(o_ref.dtype)


def grouped_matmul(x, w, tile_expert, n_used, *, w3=None, out_dtype=F32, tm=512, tn=512, kc=1024):
    mp, kdim = x.shape
    n = w.shape[2]
    kind = "swiglu" if w3 is not None else "plain"
    w_spec = pl.BlockSpec((1, kdim, tn), lambda j, i, te, nu: (te[i], 0, j))
    in_specs = [pl.BlockSpec((tm, kdim), lambda j, i, te, nu: (i, 0)), w_spec]
    args = [x, w]
    if w3 is not None:
        in_specs.append(w_spec)
        args.append(w3)
    return pl.pallas_call(
        functools.partial(_gmm_kernel, kind=kind, kc=min(kc, kdim)),
        grid_spec=pltpu.PrefetchScalarGridSpec(
            num_scalar_prefetch=2,
            grid=(n // tn, mp // tm),
            in_specs=in_specs,
            out_specs=pl.BlockSpec((tm, tn), lambda j, i, te, nu: (i, j)),
        ),
        out_shape=jax.ShapeDtypeStruct((mp, n), out_dtype),
        compiler_params=pltpu.CompilerParams(dimension_semantics=("arbitrary", "arbitrary"),
                                             vmem_limit_bytes=GMM_VMEM_LIMIT),
        name="moe_gmm_" + kind,
    )(tile_expert, n_used, *args)


def _combine_kernel(pos_ref, nxt_ref, ys_hbm, x_ref, gate_ref, mod_ref, o_ref, buf, sem, *, tc, n_tiles, gate_idx):
    i = pl.program_id(0)
    slot = i % 2

    def issue_tile(p_ref, s):
        def issue(g, c):
            for u in range(DMA_UNROLL):
                r = g * DMA_UNROLL + u
                _row_copy(ys_hbm, buf.at[s, 0], p_ref[0, 0, 2 * r], r, sem.at[s]).start(priority=0)
                _row_copy(ys_hbm, buf.at[s, 1], p_ref[0, 0, 2 * r + 1], r, sem.at[s]).start(priority=1)
            return c

        lax.fori_loop(0, tc // DMA_UNROLL, issue, 0)

    @pl.when(i == 0)
    def _():
        issue_tile(pos_ref, slot)

    @pl.when(i + 1 < n_tiles)
    def _():
        issue_tile(nxt_ref, 1 - slot)

    def drain(r, c):
        _row_copy(ys_hbm, buf.at[slot, 0], 0, r, sem.at[slot]).wait()
        _row_copy(ys_hbm, buf.at[slot, 1], 0, r, sem.at[slot]).wait()
        return c

    lax.fori_loop(0, tc, drain, 0, unroll=DMA_UNROLL)
    g = gate_ref[...]
    moe = g[:, 0:1] * buf[slot, 0] + g[:, 1:2] * buf[slot, 1]
    o_ref[...] = x_ref[...] + mod_ref[0, gate_idx:gate_idx + 1, :] * moe


def moe_combine(ys, pos, x, gates, mods, gate_idx, *, seq, tc=256):
    m, d = x.shape
    n_tiles = m // tc
    pos3 = pos.reshape(n_tiles, 1, 2 * tc)
    return pl.pallas_call(
        functools.partial(_combine_kernel, tc=tc, n_tiles=n_tiles, gate_idx=gate_idx),
        grid=(n_tiles,),
        in_specs=[pl.BlockSpec((1, 1, 2 * tc), lambda i: (i, 0, 0), memory_space=pltpu.SMEM),
                  pl.BlockSpec((1, 1, 2 * tc), lambda i: (jnp.minimum(i + 1, n_tiles - 1), 0, 0),
                               memory_space=pltpu.SMEM),
                  pl.BlockSpec(memory_space=pl.ANY),
                  pl.BlockSpec((tc, d), lambda i: (i, 0)),
                  pl.BlockSpec((tc, LANES), lambda i: (i, 0)),
                  pl.BlockSpec((1, 6, d), lambda i: ((i * tc) // seq, 0, 0))],
        out_specs=pl.BlockSpec((tc, d), lambda i: (i, 0)),
        out_shape=jax.ShapeDtypeStruct((m, d), F32),
        scratch_shapes=[pltpu.VMEM((2, 2, tc, d), F32), pltpu.SemaphoreType.DMA((2,))],
        compiler_params=_cparams(("arbitrary",)),
        name="moe_combine",
    )(pos3, pos3, ys, x, gates, mods)


def moe_layer(x, gain, mods, router, w1, w3, w2, *, seq, batch, tm=512):
    n = x.shape[0]
    xn, idx, gates = route(x, gain, mods, router, 3, 4, seq=seq, batch=batch)
    e_flat = idx[:, 0:2].reshape(-1)
    onehot = (e_flat[:, None] == jnp.arange(N_EXPERTS)[None, :]).astype(jnp.int32)
    ranks = jnp.cumsum(onehot, axis=0) - onehot
    rank = jnp.sum(ranks * onehot, axis=1)
    counts = jnp.sum(onehot, axis=0)
    padded = ((counts + tm - 1) // tm) * tm
    starts = jnp.cumsum(padded) - padded
    pos = starts[e_flat] + rank
    mp = 2 * n + N_EXPERTS * tm
    token_of_row = jnp.zeros((mp,), jnp.int32).at[pos].set(jnp.arange(2 * n, dtype=jnp.int32) // 2)
    n_tiles = mp // tm
    ends = jnp.cumsum(padded)
    tile_start = jnp.arange(n_tiles, dtype=jnp.int32) * tm
    tile_expert = jnp.minimum(jnp.sum((tile_start[:, None] >= ends[None, :]).astype(jnp.int32), axis=1),
                              N_EXPERTS - 1).astype(jnp.int32)
    n_used = (ends[-1] // tm).astype(jnp.int32).reshape(1)

    xs = gather_rows(xn, token_of_row, ends[-1].astype(jnp.int32).reshape(1))
    hs = grouped_matmul(xs, w1, tile_expert, n_used, w3=w3, out_dtype=BF16, tm=tm, tn=1024)
    ys = grouped_matmul(hs, w2, tile_expert, n_used, out_dtype=F32, tm=tm, tn=512)
    return moe_combine(ys, pos.astype(jnp.int32), x, gates, mods, 5, seq=seq)


def kernel(x, c, ctx, c_ctx, l0_ada_w, l0_ada_b, l0_norm1_g, l0_norm2_g, l0_w_in, l0_sgu_norm_g, l0_sgu_w, l0_sgu_b, l0_shift_w, l0_w0, l0_w_up, l0_a0, l0_a_up, l0_g_up, l0_k_k, l0_k_a, l0_r_k, l0_lnx_w, l0_lnx_b, l0_w_out, l0_ffn_w1, l0_ffn_w3, l0_ffn_w2, l1_ada_w, l1_ada_b, l1_norm1_g, l1_norm2_g, l1_w_qkv, l1_q_norm_g, l1_k_norm_g, l1_lam_q1, l1_lam_k1, l1_lam_q2, l1_lam_k2, l1_subln_g, l1_w_out, l1_router, l1_exp_w1, l1_exp_w3, l1_exp_w2):
    batch, seq, d = x.shape
    ctx_len = ctx.shape[1]
    n_lat = batch * seq
    tm = min(1024, seq, batch * ctx_len)
    tm_res = min(256, tm)
    geo = dict(n_lat=n_lat, seq=seq, batch=batch)
    bf = lambda w: w.astype(BF16)

    x_lat = x.reshape(n_lat, d)
    x_ctx = ctx.reshape(batch * ctx_len, d)
    cond = jnp.concatenate([c, c_ctx[None, :], jnp.zeros((COND_ROWS - batch - 1, d), F32)], axis=0)

    def ada(w, b):
        return matmul(cond, w, bias=b, silu_x=True, tn=1024).reshape(COND_ROWS, 6, d)

    mods = ada(l0_ada_w, l0_ada_b)
    hn = norm_mod(x_lat, l0_norm1_g, mods, 0, 1, tm=min(512, tm), x_ctx=x_ctx, **geo)
    w_z = bf(jnp.pad(l0_w_in[:, 2 * MIX_A:], ((0, 0), (0, Z_WIDTH - B_WIDTH))))
    hp_a = matmul(hn, bf(l0_w_in[:, :2 * MIX_A]), out_dtype=BF16, tm=tm, tn=1024)
    hp_z = matmul(hn, w_z, tm=tm, tn=Z_WIDTH // 4)
    a_out = chunk_sgu(hp_a, l0_sgu_norm_g, l0_sgu_w, l0_sgu_b)
    zs = token_shift(hp_z, l0_shift_w, n_lat=n_lat, seq=seq, ctx_len=ctx_len)
    b_out = rwkv7_bidir(zs, l0_w0, l0_w_up, l0_a0, l0_a_up, l0_g_up, l0_k_k, l0_k_a, l0_r_k, l0_lnx_w, l0_lnx_b,
                        batch=batch, seq=seq, ctx_len=ctx_len)
    mixed = jnp.concatenate([a_out, b_out], axis=1)
    x1, hn = matmul(mixed, bf(l0_w_out), res=x_lat, res_ctx=x_ctx, mods=mods, gate_idx=2, geo=geo, tm=tm_res,
                    tn=d, norm=(l0_norm2_g, 3, 4))
    hff = matmul(hn, bf(l0_ffn_w1), w3=bf(l0_ffn_w3), out_dtype=BF16, tm=tm)
    x2 = matmul(hff, bf(l0_ffn_w2), res=x1, mods=mods, gate_idx=5, geo=geo, tm=tm, tn=1024,
                tk=l0_ffn_w2.shape[0] // 2)

    mods = ada(l1_ada_w, l1_ada_b)
    hn = norm_mod(x2, l1_norm1_g, mods, 0, 1, tm=min(512, tm), **geo)
    qkv = matmul(hn, bf(l1_w_qkv), out_dtype=BF16, tm=tm, tn=1024)
    lam_params = jnp.stack([l1_lam_q1, l1_lam_k1, l1_lam_q2, l1_lam_k2])
    lam_init = 0.8 - 0.6 * math.exp(-0.3 * 1)
    o = diff_attention(qkv, l1_q_norm_g, l1_k_norm_g, lam_params, l1_subln_g, lam_init,
                       batch=batch, seq=seq, ctx_len=ctx_len)
    x3 = matmul(o, bf(l1_w_out), res=x2, mods=mods, gate_idx=2, geo=geo, tm=min(512, tm), tn=d)
    x4 = moe_layer(x3, l1_norm2_g, mods, l1_router, l1_exp_w1, l1_exp_w3, l1_exp_w2, seq=seq, batch=batch)
    return x4.reshape(batch, seq, d)
```

```python
import functools
import math

import jax
import jax.numpy as jnp
from jax import lax
from jax.experimental import pallas as pl
from jax.experimental.pallas import tpu as pltpu

F32 = jnp.float32
BF16 = jnp.bfloat16

D_MODEL = 2048
GRID_W = 64
EPS = 1e-6

MIX_A = 1024
SGU_CHUNK = 128
A_GROUPS = 8
MIX_B = 1024
B_HEAD = 64
B_HEADS = 16
DECAY_LORA = 64
AAA_LORA = 64
GATE_LORA = 160
B_WIDTH = 3 * MIX_B + DECAY_LORA + AAA_LORA + GATE_LORA
GN_EPS = B_HEAD * 1e-5

C_HEADS = 16
C_HEAD = 64
ROPE_AXIS = C_HEAD // 2
ROPE_THETA = 10000.0
SUBLN_EPS = 1e-5

N_EXPERTS = 8

LOG2E = 1.4426950408889634
LANES = 128
SUBLANES = 8
COND_ROWS = 16
VMEM_LIMIT = 52 * 1024 * 1024
GMM_VMEM_LIMIT = 58 * 1024 * 1024

RW_CHUNK = 64
INV_BLOCK = 16
DMA_UNROLL = 8
RW_TILE = 256
RW_CHUNKS_PER_ITER = 4
RW_PAIRS = B_HEADS // 2
Z_WIDTH = 3584


def _cparams(sem):
    return pltpu.CompilerParams(dimension_semantics=sem, vmem_limit_bytes=VMEM_LIMIT)


def _group_of_tile(i, tm, n_lat, seq, batch):
    return jnp.where(i * tm < n_lat, (i * tm) // seq, batch)


def _norm_mod_rows(x, gain, mod, s_idx, c_idx):
    ms = jnp.mean(x * x, axis=-1, keepdims=True)
    y = x * lax.rsqrt(ms + EPS) * gain
    return y * (1.0 + mod[c_idx:c_idx + 1, :]) + mod[s_idx:s_idx + 1, :]


def _tile_rows(ref, ctx_ref, lat_tiles):
    if ctx_ref is None:
        return ref[...]
    return jnp.where(pl.program_id(0) < lat_tiles, ref[...], ctx_ref[...])


def _split_row_specs(block, lat_tiles, col):
    lat = pl.BlockSpec(block, lambda i, *a: (jnp.minimum(i, lat_tiles - 1), col(*a)))
    ctx = pl.BlockSpec(block, lambda i, *a: (jnp.maximum(i - lat_tiles, 0), col(*a)))
    return [lat, ctx]


def _norm_mod_kernel(*refs, s_idx, c_idx, lat_tiles):
    if lat_tiles is None:
        x_ref, g_ref, mod_ref, o_ref = refs
        c_ref = None
    else:
        x_ref, c_ref, g_ref, mod_ref, o_ref = refs
    x = _tile_rows(x_ref, c_ref, lat_tiles)
    o_ref[...] = _norm_mod_rows(x, g_ref[...], mod_ref[0], s_idx, c_idx).astype(o_ref.dtype)


def norm_mod(x, gain, mods, s_idx, c_idx, *, n_lat, seq, batch, tm=512, x_ctx=None):
    d = x.shape[1]
    m = x.shape[0] + (0 if x_ctx is None else x_ctx.shape[0])
    grp = functools.partial(_group_of_tile, tm=tm, n_lat=n_lat, seq=seq, batch=batch)
    lat_tiles = None if x_ctx is None else n_lat // tm
    if x_ctx is None:
        x_specs, x_args = [pl.BlockSpec((tm, d), lambda i: (i, 0))], [x]
    else:
        x_specs, x_args = _split_row_specs((tm, d), lat_tiles, lambda: 0), [x, x_ctx]
    return pl.pallas_call(
        functools.partial(_norm_mod_kernel, s_idx=s_idx, c_idx=c_idx, lat_tiles=lat_tiles),
        grid=(m // tm,),
        in_specs=x_specs + [
            pl.BlockSpec((1, d), lambda i: (0, 0)),
            pl.BlockSpec((1, 6, d), lambda i: (grp(i), 0, 0)),
        ],
        out_specs=pl.BlockSpec((tm, d), lambda i: (i, 0)),
        out_shape=jax.ShapeDtypeStruct((m, d), BF16),
        compiler_params=_cparams(("parallel",)),
        name="norm_mod",
    )(*x_args, gain.reshape(1, d), mods)


def _mm_kernel(*refs, kind, nk, silu_x, has_bias, gate_idx, lat_tiles, norm_idx):
    refs = list(refs)
    x_ref = refs.pop(0)
    w_refs = [refs.pop(0)]
    if kind == "swiglu":
        w_refs.append(refs.pop(0))
    bias_ref = refs.pop(0) if has_bias else None
    res_ref = res_ctx_ref = mod_ref = gain_ref = hn_ref = None
    if kind == "res":
        res_ref = refs.pop(0)
        if lat_tiles is not None:
            res_ctx_ref = refs.pop(0)
        mod_ref = refs.pop(0)
        if norm_idx is not None:
            gain_ref = refs.pop(0)
    o_ref = refs.pop(0)
    if norm_idx is not None:
        hn_ref = refs.pop(0)
    acc_refs = refs

    x = x_ref[...]
    if silu_x:
        x = jax.nn.silu(x.astype(F32))
    x = x.astype(BF16)
    prods = [jnp.dot(x, w_ref[...].astype(BF16), preferred_element_type=F32) for w_ref in w_refs]

    def epilogue(vals):
        a = vals[0]
        if kind == "swiglu":
            a = jax.nn.silu(a) * vals[1]
        if has_bias:
            a = a + bias_ref[...]
        if kind == "res":
            a = _tile_rows(res_ref, res_ctx_ref, lat_tiles) + mod_ref[0, gate_idx:gate_idx + 1, :] * a
        o_ref[...] = a.astype(o_ref.dtype)
        if norm_idx is not None:
            hn_ref[...] = _norm_mod_rows(a, gain_ref[...], mod_ref[0], *norm_idx).astype(hn_ref.dtype)

    if nk == 1:
        epilogue(prods)
        return

    k = pl.program_id(2)

    @pl.when(k == 0)
    def _():
        for acc, p in zip(acc_refs, prods):
            acc[...] = p

    @pl.when(k > 0)
    def _():
        for acc, p in zip(acc_refs, prods):
            acc[...] += p

    @pl.when(k == nk - 1)
    def _():
        epilogue([acc[...] for acc in acc_refs])


def matmul(x, w, *, w3=None, bias=None, res=None, res_ctx=None, mods=None, gate_idx=0, out_dtype=F32,
           silu_x=False, tm=1024, tn=512, tk=None, geo=None, norm=None):
    m, kdim = x.shape
    n = w.shape[1]
    tm = min(tm, m)
    tk = kdim if tk is None else tk
    nk = kdim // tk
    kind = "swiglu" if w3 is not None else ("res" if res is not None else "plain")
    in_specs = [pl.BlockSpec((tm, tk), lambda i, j, k: (i, k)),
                pl.BlockSpec((tk, tn), lambda i, j, k: (k, j))]
    args = [x, w]
    if w3 is not None:
        in_specs.append(pl.BlockSpec((tk, tn), lambda i, j, k: (k, j)))
        args.append(w3)
    if bias is not None:
        in_specs.append(pl.BlockSpec((1, tn), lambda i, j, k: (0, j)))
        args.append(bias.reshape(1, n))
    lat_tiles = None
    if res is not None:
        if res_ctx is None:
            in_specs.append(pl.BlockSpec((tm, tn), lambda i, j, k: (i, j)))
            args.append(res)
        else:
            lat_tiles = geo["n_lat"] // tm
            in_specs += _split_row_specs((tm, tn), lat_tiles, lambda j, k: j)
            args += [res, res_ctx]
        group = functools.partial(_group_of_tile, tm=tm, **geo)
        in_specs.append(pl.BlockSpec((1, 6, tn), lambda i, j, k: (group(i), 0, j)))
        args.append(mods)
    out_specs = pl.BlockSpec((tm, tn), lambda i, j, k: (i, j))
    out_shape = jax.ShapeDtypeStruct((m, n), out_dtype)
    norm_idx = None
    if norm is not None:
        assert kind == "res" and tn == n
        in_specs.append(pl.BlockSpec((1, n), lambda i, j, k: (0, 0)))
        args.append(norm[0].reshape(1, n))
        norm_idx = (norm[1], norm[2])
        out_specs = [out_specs, pl.BlockSpec((tm, tn), lambda i, j, k: (i, j))]
        out_shape = [out_shape, jax.ShapeDtypeStruct((m, n), BF16)]
    n_acc = 0 if nk == 1 else (2 if w3 is not None else 1)
    return pl.pallas_call(
        functools.partial(_mm_kernel, kind=kind, nk=nk, silu_x=silu_x, has_bias=bias is not None,
                          gate_idx=gate_idx, lat_tiles=lat_tiles, norm_idx=norm_idx),
        grid=(m // tm, n // tn, nk),
        in_specs=in_specs,
        out_specs=out_specs,
        out_shape=out_shape,
        scratch_shapes=[pltpu.VMEM((tm, tn), F32)] * n_acc,
        compiler_params=_cparams(("parallel", "parallel", "arbitrary")),
        name="matmul_" + kind,
    )(*args)


def _sgu_kernel(u_ref, v_ref, ng_ref, ws_ref, bs_ref, o_ref, *, chunks):
    for c in range(chunks):
        rows = slice(c * SGU_CHUNK, (c + 1) * SGU_CHUNK)
        for g in range(A_GROUPS):
            cols = slice(g * LANES, (g + 1) * LANES)
            vg = jax.nn.gelu(v_ref[rows, cols].astype(F32))
            ms = jnp.mean(vg * vg, axis=-1, keepdims=True)
            vn = vg * lax.rsqrt(ms + EPS) * ng_ref[:, cols]
            s = jnp.dot(ws_ref[g].astype(BF16), vn.astype(BF16), preferred_element_type=F32)
            u = jax.nn.gelu(u_ref[rows, cols].astype(F32))
            o_ref[rows, cols] = (u * (s + bs_ref[:, cols])).astype(o_ref.dtype)


def chunk_sgu(hp, norm_g, ws, bs, *, tm=256):
    m = hp.shape[0]
    bs_exp = jnp.repeat(bs.T, LANES, axis=1)
    return pl.pallas_call(
        functools.partial(_sgu_kernel, chunks=tm // SGU_CHUNK),
        grid=(m // tm,),
        in_specs=[
            pl.BlockSpec((tm, MIX_A), lambda i: (i, 0)),
            pl.BlockSpec((tm, MIX_A), lambda i: (i, 1)),
            pl.BlockSpec((1, MIX_A), lambda i: (0, 0)),
            pl.BlockSpec((A_GROUPS, SGU_CHUNK, SGU_CHUNK), lambda i: (0, 0, 0)),
            pl.BlockSpec((SGU_CHUNK, MIX_A), lambda i: (0, 0)),
        ],
        out_specs=pl.BlockSpec((tm, MIX_A), lambda i: (i, 0)),
        out_shape=jax.ShapeDtypeStruct((m, MIX_A), BF16),
        compiler_params=_cparams(("parallel",)),
        name="chunk_sgu",
    )(hp, hp, norm_g.reshape(1, MIX_A), ws, bs_exp)


def _shift_kernel(x_ref, p_ref, n_ref, w_ref, o_ref, *, ts, lat_tiles, tiles_per_lat, tiles_per_ctx):
    i = pl.program_id(0)
    is_lat = i < lat_tiles
    pos = jnp.where(is_lat, i % tiles_per_lat, (i - lat_tiles) % tiles_per_ctx)
    last = jnp.where(is_lat, tiles_per_lat - 1, tiles_per_ctx - 1)
    x = x_ref[...]
    prev_row = jnp.where(pos == 0, 0.0, p_ref[SUBLANES - 1:SUBLANES, :])
    next_row = jnp.where(pos == last, 0.0, n_ref[0:1, :])
    rows = lax.broadcasted_iota(jnp.int32, x.shape, 0)
    xm = jnp.where(rows == 0, prev_row, pltpu.roll(x, 1, axis=0))
    xp = jnp.where(rows == ts - 1, next_row, pltpu.roll(x, ts - 1, axis=0))
    o_ref[...] = (w_ref[0:1, :] * xm + w_ref[1:2, :] * x + w_ref[2:3, :] * xp).astype(o_ref.dtype)


def token_shift(z, shift_w, *, n_lat, seq, ctx_len, ts=256, tc=Z_WIDTH):
    m = z.shape[0]
    halo = SUBLANES
    nblk8 = m // halo
    w_pad = jnp.pad(shift_w, ((0, 0), (0, Z_WIDTH - B_WIDTH)))
    kern = functools.partial(_shift_kernel, ts=ts, lat_tiles=n_lat // ts, tiles_per_lat=seq // ts,
                             tiles_per_ctx=ctx_len // ts)
    return pl.pallas_call(
        kern,
        grid=(m // ts, Z_WIDTH // tc),
        in_specs=[
            pl.BlockSpec((ts, tc), lambda i, j: (i, j)),
            pl.BlockSpec((halo, tc), lambda i, j: (jnp.maximum(i * (ts // halo) - 1, 0), j)),
            pl.BlockSpec((halo, tc), lambda i, j: (jnp.minimum((i + 1) * (ts // halo), nblk8 - 1), j)),
            pl.BlockSpec((3, tc), lambda i, j: (0, j)),
        ],
        out_specs=pl.BlockSpec((ts, tc), lambda i, j: (i, j)),
        out_shape=jax.ShapeDtypeStruct((m, Z_WIDTH), BF16),
        compiler_params=_cparams(("parallel", "parallel")),
        name="token_shift",
    )(z, z, z, w_pad)


def _mask_lanes(x, m0):
    z = jnp.zeros_like(x)
    return jnp.concatenate([jnp.where(m0, x, z), jnp.where(m0, z, x)], axis=0)


def _dot(a, b):
    return jnp.dot(a.astype(BF16), b.astype(BF16), preferred_element_type=F32)


def _dot_nt(a, b):
    return lax.dot_general(a.astype(BF16), b.astype(BF16), (((1,), (1,)), ((), ())),
                           preferred_element_type=F32)


def _dot_tn(a, b):
    return lax.dot_general(a.astype(BF16), b.astype(BF16), (((0,), (0,)), ((), ())),
                           preferred_element_type=F32)


def _head_sums(x, m0):
    s0 = jnp.sum(jnp.where(m0, x, 0.0), axis=-1, keepdims=True)
    s1 = jnp.sum(jnp.where(m0, 0.0, x), axis=-1, keepdims=True)
    return jnp.where(m0, s0, s1)


def _rwkv_kernel(*refs, reverse, n_chunks):
    if reverse:
        (zr_ref, zk_ref, zv_ref, zl_ref, w0_ref, wt_ref, a0_ref, wa_ref, kk_ref, ka_ref, rk_ref,
         y0_ref, g_ref, bon0_ref, lnw_ref, lnb_ref, out_ref,
         s_ref, kk_s, bt_s, kd_s, rt_s, v_s, y_s, ge_s) = refs
    else:
        (zr_ref, zk_ref, zv_ref, zl_ref, w0_ref, wt_ref, a0_ref, wa_ref, kk_ref, ka_ref, rk_ref, wg_ref,
         y_out_ref, g_out_ref, bon_out_ref,
         s_ref, kk_s, bt_s, kd_s, rt_s, v_s, y_s, ge_s) = refs
    ch = RW_CHUNK
    j = pl.program_id(1)

    @pl.when(j == 0)
    def _():
        s_ref[...] = jnp.zeros_like(s_ref)

    zl = zl_ref[...].astype(F32)
    lora_in = zl[:, 0:LANES]
    wl = w0_ref[...] + _dot(jnp.tanh(lora_in), wt_ref[...])
    wlog = -(jnp.maximum(-wl, 0.0) + jnp.log(1.0 + jnp.exp(-jnp.abs(wl)))) - 0.5
    lw = -jnp.exp(wlog)
    a = jax.nn.sigmoid(a0_ref[...] + _dot(lora_in, wa_ref[...]))
    r = zr_ref[...].astype(F32)
    k = zk_ref[...].astype(F32)
    v = zv_ref[...].astype(F32)
    kd = k * (1.0 + (a - 1.0) * ka_ref[...])
    kraw = k * kk_ref[...]
    rkd = r * kd * rk_ref[...]
    if not reverse:
        g_out_ref[...] = _dot(jax.nn.sigmoid(zl[:, LANES:3 * LANES]), wg_ref[...])

    lane = lax.broadcasted_iota(jnp.int32, (1, LANES), 1)
    m0 = lane < B_HEAD
    ti = lax.broadcasted_iota(jnp.int32, (ch, ch), 0)
    si = lax.broadcasted_iota(jnp.int32, (ch, ch), 1)
    tri = ((si >= ti) if reverse else (si <= ti)).astype(BF16)

    kn_l, b_l = [], []
    for p in range(RW_PAIRS):
        cols = slice(p * LANES, (p + 1) * LANES)
        kp = kraw[:, cols]
        nrm = jnp.sqrt(_head_sums(kp * kp, m0))
        kn = kp / jnp.maximum(nrm, 1e-12)
        kn_l.append(kn)
        b_l.append(kn * a[:, cols])
        v_s[p] = v[:, cols]
        if not reverse:
            bon_out_ref[:, cols] = _head_sums(rkd[:, cols], m0) * v[:, cols]

    for c in range(n_chunks):
        rws = slice(c * ch, (c + 1) * ch)
        lw_c = lw[rws, :]
        hi = lw_c.astype(BF16)
        rem = lw_c - hi.astype(F32)
        mid = rem.astype(BF16)
        lo = (rem - mid.astype(F32)).astype(BF16)
        cs = (jnp.dot(tri, hi, preferred_element_type=F32) + jnp.dot(tri, mid, preferred_element_type=F32)
              + jnp.dot(tri, lo, preferred_element_type=F32))
        total = cs[0:1, :] if reverse else cs[ch - 1:ch, :]
        g_in = jnp.exp(cs)
        g_ex = jnp.exp(cs - lw_c)
        g_inv = jnp.exp(-cs)
        g_end = jnp.exp(total)
        for p in range(RW_PAIRS):
            cols = slice(p * LANES, (p + 1) * LANES)
            kk_s[p, rws, :] = kn_l[p][rws, :] * g_ex[:, cols]
            bt_s[p, rws, :] = b_l[p][rws, :] * g_inv[:, cols]
            kd_s[p, rws, :] = kd[rws, cols] * g_inv[:, cols]
            rt_s[p, rws, :] = r[rws, cols] * g_in[:, cols]
            ge_s[p, c:c + 1, :] = g_end[:, cols]

    ri = lax.broadcasted_iota(jnp.int32, (2 * ch, 2 * ch), 0)
    ci = lax.broadcasted_iota(jnp.int32, (2 * ch, 2 * ch), 1)
    same_head = (ri // ch) == (ci // ch)
    if reverse:
        strict = same_head & (ci > ri)
        incl = same_head & (ci >= ri)
    else:
        strict = same_head & (ci < ri)
        incl = same_head & (ci <= ri)
    eye = (ri == ci).astype(F32)
    zero = jnp.zeros((2 * ch, 2 * ch), F32)
    n_sq = int(math.log2(INV_BLOCK)) - 1
    blk_diag = []
    bs = INV_BLOCK
    while bs <= ch:
        blk_diag.append((ri // bs) == (ci // bs))
        bs *= 2

    def chunk_body(it, carry):
        cis = [it * RW_CHUNKS_PER_ITER + s for s in range(RW_CHUNKS_PER_ITER)]
        cs_ = [(n_chunks - 1 - ci) if reverse else ci for ci in cis]
        rows_l = [pl.ds(pl.multiple_of(c * ch, ch), ch) for c in cs_]
        units = [(s, p) for s in range(RW_CHUNKS_PER_ITER) for p in range(RW_PAIRS)]
        pairs = range(len(units))
        kk_t = [kk_s[p, rows_l[s], :] for s, p in units]
        rt = [rt_s[p, rows_l[s], :] for s, p in units]
        bt_m = [_mask_lanes(bt_s[p, rows_l[s], :], m0) for s, p in units]
        kd_m = [_mask_lanes(kd_s[p, rows_l[s], :], m0) for s, p in units]
        v_st = [_mask_lanes(v_s[p, rows_l[s], :], m0) for s, p in units]
        g_end = [ge_s[p, pl.ds(cs_[s], 1), :] for s, p in units]

        pq = [_dot_nt(jnp.concatenate([kk_t[p], kk_t[p], rt[p], rt[p]], axis=0),
                      jnp.concatenate([bt_m[p], kd_m[p]], axis=0)) for p in pairs]
        a_m = [jnp.where(strict, pq[p][0:2 * ch, 0:2 * ch], zero) for p in pairs]
        b_m = [jnp.where(strict, pq[p][0:2 * ch, 2 * ch:4 * ch], zero) for p in pairs]
        m2 = [jnp.where(incl, pq[p][2 * ch:4 * ch, 0:2 * ch], zero) for p in pairs]
        m1 = [jnp.where(incl, pq[p][2 * ch:4 * ch, 2 * ch:4 * ch], zero) for p in pairs]

        q = [jnp.where(blk_diag[0], -a_m[p], zero) for p in pairs]
        t_m = [eye + q[p] for p in pairs]
        q = [_dot(q[p], q[p]) for p in pairs]
        for _ in range(n_sq - 1):
            both = [_dot(jnp.concatenate([q[p], t_m[p]], axis=0), q[p]) for p in pairs]
            q = [both[p][0:2 * ch] for p in pairs]
            t_m = [t_m[p] + both[p][2 * ch:4 * ch] for p in pairs]
        t_m = [t_m[p] + _dot(t_m[p], q[p]) for p in pairs]
        for lvl in range(1, len(blk_diag)):
            off = [jnp.where(blk_diag[lvl] & ~blk_diag[lvl - 1], a_m[p], zero) for p in pairs]
            t_off = [_dot(t_m[p], off[p]) for p in pairs]
            t_m = [t_m[p] - _dot(t_off[p], t_m[p]) for p in pairs]

        bm1v = [_dot(jnp.concatenate([b_m[p], m1[p]], axis=0), v_st[p]) for p in pairs]
        ku = [_dot(t_m[p], jnp.concatenate([_mask_lanes(kk_t[p], m0), bm1v[p][0:2 * ch]], axis=1))
              for p in pairs]
        m2ku = [_dot(m2[p], ku[p]) for p in pairs]
        rp, y0 = [], []
        for p in pairs:
            rp_s = _mask_lanes(rt[p], m0) - m2ku[p][:, 0:LANES]
            y0_s = bm1v[p][2 * ch:4 * ch] - m2ku[p][:, LANES:2 * LANES]
            rp.append(rp_s[0:ch] + rp_s[ch:2 * ch])
            y0.append(y0_s[0:ch] + y0_s[ch:2 * ch])
        for s in range(RW_CHUNKS_PER_ITER):
            us = [s * RW_PAIRS + p for p in range(RW_PAIRS)]
            s_bd = [s_ref[p] for p in range(RW_PAIRS)]
            rs = [_dot_nt(jnp.concatenate([rp[u], ku[u][:, 0:LANES]], axis=0), s_bd[p])
                  for p, u in enumerate(us)]
            for p, u in enumerate(us):
                u_all = rs[p][ch:3 * ch] + ku[u][:, LANES:2 * LANES]
                upd = _dot_tn(jnp.concatenate([v_st[u], -u_all], axis=0),
                              jnp.concatenate([kd_m[u] * g_end[u], bt_m[u] * g_end[u]], axis=0))
                s_ref[p] = s_bd[p] * g_end[u] + upd
                y_s[p, rows_l[s], :] = rs[p][0:ch] + y0[u]
        return carry

    lax.fori_loop(0, n_chunks // RW_CHUNKS_PER_ITER, chunk_body, 0)

    for p in range(RW_PAIRS):
        cols = slice(p * LANES, (p + 1) * LANES)
        if not reverse:
            y_out_ref[:, cols] = y_s[p]
        else:
            bon1 = _head_sums(rkd[:, cols], m0) * v[:, cols]
            ysum = y0_ref[:, cols] + y_s[p]
            mu = _head_sums(ysum, m0) * (1.0 / B_HEAD)
            dlt = ysum - mu
            var = _head_sums(dlt * dlt, m0) * (1.0 / B_HEAD)
            yn = dlt * lax.rsqrt(var + GN_EPS)
            o = (yn * lnw_ref[:, cols] + lnb_ref[:, cols] + bon0_ref[:, cols] + bon1) * g_ref[:, cols]
            out_ref[:, cols] = o.astype(out_ref.dtype)


def rwkv7_bidir(zs, w0, w_up, a0, a_up, g_up, k_k, k_a, r_k, lnx_w, lnx_b, *, batch, seq, ctx_len):
    m = zs.shape[0]
    n_lat = batch * seq
    tb = RW_TILE
    assert seq % tb == 0 and ctx_len % tb == 0 and tb // RW_CHUNK <= SUBLANES
    nct, nlt = ctx_len // tb, seq // tb
    steps = nct + nlt

    def row_block(reverse):
        def f(b, j):
            jc = (nct - 1 - j) if reverse else j
            jl = (nlt - 1 - (j - nct)) if reverse else (j - nct)
            return jnp.where(j < nct, (n_lat + b * ctx_len) // tb + jc, b * nlt + jl)
        return f

    def pad_rows(w, top, total):
        return jnp.pad(w, ((top, total - top - w.shape[0]), (0, 0)))

    wg = pad_rows(g_up, 0, 2 * LANES)
    rk = r_k.reshape(1, MIX_B)
    vec = lambda a: a.reshape(1, MIX_B)
    scratch = ([pltpu.VMEM((RW_PAIRS, LANES, LANES), F32)] + [pltpu.VMEM((RW_PAIRS, tb, LANES), F32)] * 6
               + [pltpu.VMEM((RW_PAIRS, SUBLANES, LANES), F32)])
    const = lambda shape: pl.BlockSpec(shape, lambda b, j: (0,) * len(shape))

    def call(reverse, extra_in, extra_specs, out_shapes, out_specs, d):
        rb = row_block(reverse)
        wt = pad_rows(w_up[d], 0, LANES)
        wa = pad_rows(a_up[d], DECAY_LORA, LANES)
        in_specs = [
            pl.BlockSpec((tb, MIX_B), lambda b, j: (rb(b, j), 0)),
            pl.BlockSpec((tb, MIX_B), lambda b, j: (rb(b, j), 1)),
            pl.BlockSpec((tb, MIX_B), lambda b, j: (rb(b, j), 2)),
            pl.BlockSpec((tb, 512), lambda b, j: (rb(b, j), 6)),
            const((1, MIX_B)), const((LANES, MIX_B)), const((1, MIX_B)), const((LANES, MIX_B)),
            const((1, MIX_B)), const((1, MIX_B)), const((1, MIX_B)),
        ] + extra_specs
        args = [zs, zs, zs, zs, vec(w0[d]), wt, vec(a0[d]), wa, vec(k_k), vec(k_a), rk] + extra_in
        return pl.pallas_call(
            functools.partial(_rwkv_kernel, reverse=reverse, n_chunks=tb // RW_CHUNK),
            grid=(batch, steps),
            in_specs=in_specs,
            out_specs=out_specs,
            out_shape=out_shapes,
            scratch_shapes=scratch,
            compiler_params=_cparams(("parallel", "arbitrary")),
            name="rwkv7_rev" if reverse else "rwkv7_fwd",
        )(*args)

    rbf = row_block(False)
    tile_f = pl.BlockSpec((tb, MIX_B), lambda b, j: (rbf(b, j), 0))
    y0, g, bon0 = call(False, [wg], [const((2 * LANES, MIX_B))],
                       [jax.ShapeDtypeStruct((m, MIX_B), F32)] * 3, [tile_f] * 3, 0)
    rbr = row_block(True)
    tile_r = pl.BlockSpec((tb, MIX_B), lambda b, j: (rbr(b, j), 0))
    out = call(True, [y0, g, bon0, vec(lnx_w), vec(lnx_b)],
               [tile_r, tile_r, tile_r, const((1, MIX_B)), const((1, MIX_B))],
               jax.ShapeDtypeStruct((m, MIX_B), BF16), tile_r, 1)
    return out


def _half_rms(x, gain, m0):
    ms = _head_sums(x * x, m0) * (1.0 / C_HEAD)
    return x * lax.rsqrt(ms + EPS) * gain


def _rope(x, cos, sin):
    lane = lax.broadcasted_iota(jnp.int32, x.shape, 1)
    first = (lane % ROPE_AXIS) < (ROPE_AXIS // 2)
    half = ROPE_AXIS // 2
    rot = jnp.where(first, -pltpu.roll(x, LANES - half, axis=1), pltpu.roll(x, half, axis=1))
    return x * cos + rot * sin


def _attn_kernel(q_ref, kl_ref, vl_ref, kc_ref, vc_ref, cq_ref, sq_ref, ck_ref, sk_ref, qg_ref, kg_ref,
                 lam_ref, sg_ref, o_ref, k_s, v_s, *, ctx_len, lam_init, q_blk):
    qi = pl.program_id(2)
    lane = lax.broadcasted_iota(jnp.int32, (1, LANES), 1)
    m0 = lane < C_HEAD

    @pl.when(qi == 0)
    def _():
        kc = _half_rms(kc_ref[...].astype(F32), kg_ref[...], m0)
        kl = _rope(_half_rms(kl_ref[...].astype(F32), kg_ref[...], m0), ck_ref[...], sk_ref[...])
        k_s[0:ctx_len, :] = kc.astype(BF16)
        k_s[ctx_len:, :] = kl.astype(BF16)
        v_s[0:ctx_len, :] = vc_ref[...].astype(BF16)
        v_s[ctx_len:, :] = vl_ref[...].astype(BF16)

    lp = lam_ref[...]
    lam = (jnp.exp(jnp.sum(lp[0:1] * lp[1:2], keepdims=True)) - jnp.exp(jnp.sum(lp[2:3] * lp[3:4], keepdims=True))
           + lam_init)
    q = (_rope(_half_rms(q_ref[...].astype(F32), qg_ref[...], m0), cq_ref[...], sq_ref[...])
         * (C_HEAD ** -0.5 * LOG2E))
    keys = k_s[...]
    vals = v_s[...]
    zq = jnp.zeros_like(q)
    q_sub = [jnp.where(m0, q, zq), jnp.where(m0, zq, q)]
    blocks = [(slice(r0, r0 + q_blk), i) for r0 in range(0, q.shape[0], q_blk) for i in (0, 1)]
    lookahead = 2
    scores = [_dot_nt(q_sub[i][rows], keys) for rows, i in blocks[:lookahead]]
    outs = []
    for n, (rows, i) in enumerate(blocks):
        s = scores[n]
        e = jnp.exp2(s - jnp.max(s, axis=-1, keepdims=True))
        w = (lam if i else 1.0) / jnp.sum(e, axis=-1, keepdims=True)
        if n + lookahead < len(blocks):
            rows2, i2 = blocks[n + lookahead]
            scores.append(_dot_nt(q_sub[i2][rows2], keys))
        outs.append(jnp.dot(e.astype(BF16), vals, preferred_element_type=F32) * w)
    for n, r0 in enumerate(range(0, q.shape[0], q_blk)):
        o = outs[2 * n] - outs[2 * n + 1]
        ms = jnp.mean(o * o, axis=-1, keepdims=True)
        o = o * lax.rsqrt(ms + SUBLN_EPS) * sg_ref[...] * (1.0 - lam_init)
        o_ref[r0:r0 + q_blk, :] = o.astype(o_ref.dtype)


def _rope_tables(n):
    rows = n // GRID_W
    row = jnp.broadcast_to(jnp.arange(rows, dtype=F32)[:, None], (rows, GRID_W)).reshape(-1)
    col = jnp.broadcast_to(jnp.arange(GRID_W, dtype=F32)[None, :], (rows, GRID_W)).reshape(-1)
    inv = ROPE_THETA ** (-jnp.arange(0, ROPE_AXIS, 2, dtype=F32) / ROPE_AXIS)
    ar = row[:, None] * inv
    ac = col[:, None] * inv
    ang = jnp.concatenate([ar, ar, ac, ac, ar, ar, ac, ac], axis=-1)
    return jnp.cos(ang), jnp.sin(ang)


def diff_attention(qkv, q_g, k_g, lam_params, subln_g, lam_init, *, batch, seq, ctx_len, tq=1024):
    n_lat = batch * seq
    tq = min(tq, seq)
    cos, sin = _rope_tables(seq)
    nq = seq // tq
    hq, hk, hv = 0, D_MODEL // LANES, 2 * D_MODEL // LANES
    ctx_blk0 = n_lat // ctx_len
    two = lambda a: jnp.concatenate([a, a]).reshape(1, LANES)
    const = lambda shape: pl.BlockSpec(shape, lambda b, h, i: (0,) * len(shape))
    return pl.pallas_call(
        functools.partial(_attn_kernel, ctx_len=ctx_len, lam_init=lam_init, q_blk=min(128, tq)),
        grid=(batch, C_HEADS, nq),
        in_specs=[
            pl.BlockSpec((tq, LANES), lambda b, h, i: (b * nq + i, hq + h)),
            pl.BlockSpec((seq, LANES), lambda b, h, i: (b, hk + h)),
            pl.BlockSpec((seq, LANES), lambda b, h, i: (b, hv + h)),
            pl.BlockSpec((ctx_len, LANES), lambda b, h, i: (ctx_blk0 + b, hk + h)),
            pl.BlockSpec((ctx_len, LANES), lambda b, h, i: (ctx_blk0 + b, hv + h)),
            pl.BlockSpec((tq, LANES), lambda b, h, i: (i, 0)),
            pl.BlockSpec((tq, LANES), lambda b, h, i: (i, 0)),
            const((seq, LANES)), const((seq, LANES)),
            const((1, LANES)), const((1, LANES)), const((4, C_HEAD)), const((1, LANES)),
        ],
        out_specs=pl.BlockSpec((tq, LANES), lambda b, h, i: (b * nq + i, h)),
        out_shape=jax.ShapeDtypeStruct((n_lat, D_MODEL), BF16),
        scratch_shapes=[pltpu.VMEM((ctx_len + seq, LANES), BF16)] * 2,
        compiler_params=_cparams(("parallel", "parallel", "arbitrary")),
        name="diff_attention",
    )(qkv, qkv, qkv, qkv, qkv, cos, sin, cos, sin, two(q_g), two(k_g), lam_params, subln_g.reshape(1, LANES))


def _route_kernel(x_ref, g_ref, mod_ref, rw_ref, xn_ref, idx_ref, gate_ref, *, s_idx, c_idx):
    h = _norm_mod_rows(x_ref[...], g_ref[...], mod_ref[0], s_idx, c_idx)
    xn_ref[...] = h
    logits = jnp.dot(h, rw_ref[...], precision=lax.Precision.HIGHEST, preferred_element_type=F32)
    lane = lax.broadcasted_iota(jnp.int32, logits.shape, 1)
    neg = jnp.float32(-jnp.inf)
    lg = jnp.where(lane < N_EXPERTS, logits, neg)
    m1 = jnp.max(lg, axis=-1, keepdims=True)
    i1 = jnp.min(jnp.where(lg == m1, lane, LANES), axis=-1, keepdims=True)
    lg2 = jnp.where(lane == i1, neg, lg)
    m2 = jnp.max(lg2, axis=-1, keepdims=True)
    i2 = jnp.min(jnp.where(lg2 == m2, lane, LANES), axis=-1, keepdims=True)
    e2 = jnp.exp(m2 - m1)
    g1 = 1.0 / (1.0 + e2)
    g2 = e2 * g1
    idx_ref[...] = jnp.where(lane == 0, i1, jnp.where(lane == 1, i2, 0))
    gate_ref[...] = jnp.where(lane == 0, g1, jnp.where(lane == 1, g2, 0.0))


def route(x, gain, mods, router, s_idx, c_idx, *, seq, batch, tm=256):
    m, d = x.shape
    rw = jnp.pad(router, ((0, 0), (0, LANES - N_EXPERTS)))
    return pl.pallas_call(
        functools.partial(_route_kernel, s_idx=s_idx, c_idx=c_idx),
        grid=(m // tm,),
        in_specs=[
            pl.BlockSpec((tm, d), lambda i: (i, 0)),
            pl.BlockSpec((1, d), lambda i: (0, 0)),
            pl.BlockSpec((1, 6, d), lambda i: ((i * tm) // seq, 0, 0)),
            pl.BlockSpec((d, LANES), lambda i: (0, 0)),
        ],
        out_specs=[pl.BlockSpec((tm, d), lambda i: (i, 0)),
                   pl.BlockSpec((tm, LANES), lambda i: (i, 0)),
                   pl.BlockSpec((tm, LANES), lambda i: (i, 0))],
        out_shape=[jax.ShapeDtypeStruct((m, d), F32),
                   jax.ShapeDtypeStruct((m, LANES), jnp.int32),
                   jax.ShapeDtypeStruct((m, LANES), F32)],
        compiler_params=_cparams(("parallel",)),
        name="moe_route",
    )(x, gain.reshape(1, d), mods, rw)


def _row_copy(src_hbm, dst_ref, src_row, dst_row, sem):
    return pltpu.make_async_copy(src_hbm.at[pl.ds(src_row, 1)], dst_ref.at[pl.ds(dst_row, 1)], sem)


def _gather_kernel(used_ref, idx_ref, nxt_ref, src_hbm, o_ref, buf, sem, *, tg, n_tiles):
    i = pl.program_id(0)
    slot = i % 2

    def issue_tile(ids_ref, s):
        def issue(g, c):
            for u in range(DMA_UNROLL):
                r = g * DMA_UNROLL + u
                _row_copy(src_hbm, buf.at[s], ids_ref[0, 0, r], r, sem.at[s]).start(priority=u % 2)
            return c

        lax.fori_loop(0, tg // DMA_UNROLL, issue, 0)

    @pl.when((i == 0) & (used_ref[0] > 0))
    def _():
        issue_tile(idx_ref, slot)

    @pl.when((i + 1 < n_tiles) & ((i + 1) * tg < used_ref[0]))
    def _():
        issue_tile(nxt_ref, 1 - slot)

    @pl.when(i * tg >= used_ref[0])
    def _():
        o_ref[...] = jnp.zeros_like(o_ref)

    @pl.when(i * tg < used_ref[0])
    def _():
        def drain(r, c):
            _row_copy(src_hbm, buf.at[slot], 0, r, sem.at[slot]).wait()
            return c

        lax.fori_loop(0, tg, drain, 0, unroll=DMA_UNROLL)
        o_ref[...] = buf[slot].astype(o_ref.dtype)


def gather_rows(src, idx, rows_used, *, tg=256):
    mp = idx.shape[0]
    d = src.shape[1]
    n_tiles = mp // tg
    idx3 = idx.reshape(n_tiles, 1, tg)
    return pl.pallas_call(
        functools.partial(_gather_kernel, tg=tg, n_tiles=n_tiles),
        grid_spec=pltpu.PrefetchScalarGridSpec(
            num_scalar_prefetch=1,
            grid=(n_tiles,),
            in_specs=[pl.BlockSpec((1, 1, tg), lambda i, u: (i, 0, 0), memory_space=pltpu.SMEM),
                      pl.BlockSpec((1, 1, tg), lambda i, u: (jnp.minimum(i + 1, n_tiles - 1), 0, 0),
                                   memory_space=pltpu.SMEM),
                      pl.BlockSpec(memory_space=pl.ANY)],
            out_specs=pl.BlockSpec((tg, d), lambda i, u: (i, 0)),
            scratch_shapes=[pltpu.VMEM((2, tg, d), src.dtype), pltpu.SemaphoreType.DMA((2,))],
        ),
        out_shape=jax.ShapeDtypeStruct((mp, d), BF16),
        compiler_params=_cparams(("arbitrary",)),
        name="moe_gather",
    )(rows_used, idx3, idx3, src)


def _gmm_kernel(te_ref, nu_ref, *refs, kind, kc):
    if kind == "swiglu":
        x_ref, w1_ref, w3_ref, o_ref = refs
        w_refs = (w1_ref, w3_ref)
    else:
        x_ref, w1_ref, o_ref = refs
        w_refs = (w1_ref,)
    i = pl.program_id(1)

    @pl.when(i >= nu_ref[0])
    def _():
        o_ref[...] = jnp.zeros_like(o_ref)

    @pl.when(i < nu_ref[0])
    def _():
        kdim = x_ref.shape[1]
        accs = [None] * len(w_refs)
        for k0 in range(0, kdim, kc):
            x = x_ref[:, k0:k0 + kc].astype(BF16)
            for n, w_ref in enumerate(w_refs):
                p = jnp.dot(x, w_ref[0, k0:k0 + kc, :].astype(BF16), preferred_element_type=F32)
                accs[n] = p if accs[n] is None else accs[n] + p
        a = accs[0]
        if kind == "swiglu":
            a = jax.nn.silu(a) * accs[1]
        o_ref[...] = a.astype(o_ref.dtype)


def grouped_matmul(x, w, tile_expert, n_used, *, w3=None, out_dtype=F32, tm=512, tn=512, kc=1024):
    mp, kdim = x.shape
    n = w.shape[2]
    kind = "swiglu" if w3 is not None else "plain"
    w_spec = pl.BlockSpec((1, kdim, tn), lambda j, i, te, nu: (te[i], 0, j))
    in_specs = [pl.BlockSpec((tm, kdim), lambda j, i, te, nu: (i, 0)), w_spec]
    args = [x, w]
    if w3 is not None:
        in_specs.append(w_spec)
        args.append(w3)
    return pl.pallas_call(
        functools.partial(_gmm_kernel, kind=kind, kc=min(kc, kdim)),
        grid_spec=pltpu.PrefetchScalarGridSpec(
            num_scalar_prefetch=2,
            grid=(n // tn, mp // tm),
            in_specs=in_specs,
            out_specs=pl.BlockSpec((tm, tn), lambda j, i, te, nu: (i, j)),
        ),
        out_shape=jax.ShapeDtypeStruct((mp, n), out_dtype),
        compiler_params=pltpu.CompilerParams(dimension_semantics=("arbitrary", "arbitrary"),
                                             vmem_limit_bytes=GMM_VMEM_LIMIT),
        name="moe_gmm_" + kind,
    )(tile_expert, n_used, *args)


def _combine_kernel(pos_ref, nxt_ref, ys_hbm, x_ref, gate_ref, mod_ref, o_ref, buf, sem, *, tc, n_tiles, gate_idx):
    i = pl.program_id(0)
    slot = i % 2

    def issue_tile(p_ref, s):
        def issue(g, c):
            for u in range(DMA_UNROLL):
                r = g * DMA_UNROLL + u
                _row_copy(ys_hbm, buf.at[s, 0], p_ref[0, 0, 2 * r], r, sem.at[s]).start(priority=0)
                _row_copy(ys_hbm, buf.at[s, 1], p_ref[0, 0, 2 * r + 1], r, sem.at[s]).start(priority=1)
            return c

        lax.fori_loop(0, tc // DMA_UNROLL, issue, 0)

    @pl.when(i == 0)
    def _():
        issue_tile(pos_ref, slot)

    @pl.when(i + 1 < n_tiles)
    def _():
        issue_tile(nxt_ref, 1 - slot)

    def drain(r, c):
        _row_copy(ys_hbm, buf.at[slot, 0], 0, r, sem.at[slot]).wait()
        _row_copy(ys_hbm, buf.at[slot, 1], 0, r, sem.at[slot]).wait()
        return c

    lax.fori_loop(0, tc, drain, 0, unroll=DMA_UNROLL)
    g = gate_ref[...]
    moe = g[:, 0:1] * buf[slot, 0] + g[:, 1:2] * buf[slot, 1]
    o_ref[...] = x_ref[...] + mod_ref[0, gate_idx:gate_idx + 1, :] * moe


def moe_combine(ys, pos, x, gates, mods, gate_idx, *, seq, tc=256):
    m, d = x.shape
    n_tiles = m // tc
    pos3 = pos.reshape(n_tiles, 1, 2 * tc)
    return pl.pallas_call(
        functools.partial(_combine_kernel, tc=tc, n_tiles=n_tiles, gate_idx=gate_idx),
        grid=(n_tiles,),
        in_specs=[pl.BlockSpec((1, 1, 2 * tc), lambda i: (i, 0, 0), memory_space=pltpu.SMEM),
                  pl.BlockSpec((1, 1, 2 * tc), lambda i: (jnp.minimum(i + 1, n_tiles - 1), 0, 0),
                               memory_space=pltpu.SMEM),
                  pl.BlockSpec(memory_space=pl.ANY),
                  pl.BlockSpec((tc, d), lambda i: (i, 0)),
                  pl.BlockSpec((tc, LANES), lambda i: (i, 0)),
                  pl.BlockSpec((1, 6, d), lambda i: ((i * tc) // seq, 0, 0))],
        out_specs=pl.BlockSpec((tc, d), lambda i: (i, 0)),
        out_shape=jax.ShapeDtypeStruct((m, d), F32),
        scratch_shapes=[pltpu.VMEM((2, 2, tc, d), F32), pltpu.SemaphoreType.DMA((2,))],
        compiler_params=_cparams(("arbitrary",)),
        name="moe_combine",
    )(pos3, pos3, ys, x, gates, mods)


def moe_layer(x, gain, mods, router, w1, w3, w2, *, seq, batch, tm=512):
    n = x.shape[0]
    xn, idx, gates = route(x, gain, mods, router, 3, 4, seq=seq, batch=batch)
    e_flat = idx[:, 0:2].reshape(-1)
    onehot = (e_flat[:, None] == jnp.arange(N_EXPERTS)[None, :]).astype(jnp.int32)
    ranks = jnp.cumsum(onehot, axis=0) - onehot
    rank = jnp.sum(ranks * onehot, axis=1)
    counts = jnp.sum(onehot, axis=0)
    padded = ((counts + tm - 1) // tm) * tm
    starts = jnp.cumsum(padded) - padded
    pos = starts[e_flat] + rank
    mp = 2 * n + N_EXPERTS * tm
    token_of_row = jnp.zeros((mp,), jnp.int32).at[pos].set(jnp.arange(2 * n, dtype=jnp.int32) // 2)
    n_tiles = mp // tm
    ends = jnp.cumsum(padded)
    tile_start = jnp.arange(n_tiles, dtype=jnp.int32) * tm
    tile_expert = jnp.minimum(jnp.sum((tile_start[:, None] >= ends[None, :]).astype(jnp.int32), axis=1),
                              N_EXPERTS - 1).astype(jnp.int32)
    n_used = (ends[-1] // tm).astype(jnp.int32).reshape(1)

    xs = gather_rows(xn, token_of_row, ends[-1].astype(jnp.int32).reshape(1))
    hs = grouped_matmul(xs, w1, tile_expert, n_used, w3=w3, out_dtype=BF16, tm=tm, tn=1024)
    ys = grouped_matmul(hs, w2, tile_expert, n_used, out_dtype=F32, tm=tm, tn=512)
    return moe_combine(ys, pos.astype(jnp.int32), x, gates, mods, 5, seq=seq)


def kernel(x, c, ctx, c_ctx, l0_ada_w, l0_ada_b, l0_norm1_g, l0_norm2_g, l0_w_in, l0_sgu_norm_g, l0_sgu_w, l0_sgu_b, l0_shift_w, l0_w0, l0_w_up, l0_a0, l0_a_up, l0_g_up, l0_k_k, l0_k_a, l0_r_k, l0_lnx_w, l0_lnx_b, l0_w_out, l0_ffn_w1, l0_ffn_w3, l0_ffn_w2, l1_ada_w, l1_ada_b, l1_norm1_g, l1_norm2_g, l1_w_qkv, l1_q_norm_g, l1_k_norm_g, l1_lam_q1, l1_lam_k1, l1_lam_q2, l1_lam_k2, l1_subln_g, l1_w_out, l1_router, l1_exp_w1, l1_exp_w3, l1_exp_w2):
    batch, seq, d = x.shape
    ctx_len = ctx.shape[1]
    n_lat = batch * seq
    tm = min(1024, seq, batch * ctx_len)
    tm_res = min(256, tm)
    geo = dict(n_lat=n_lat, seq=seq, batch=batch)
    bf = lambda w: w.astype(BF16)

    x_lat = x.reshape(n_lat, d)
    x_ctx = ctx.reshape(batch * ctx_len, d)
    cond = jnp.concatenate([c, c_ctx[None, :], jnp.zeros((COND_ROWS - batch - 1, d), F32)], axis=0)

    def ada(w, b):
        return matmul(cond, w, bias=b, silu_x=True, tn=1024).reshape(COND_ROWS, 6, d)

    mods = ada(l0_ada_w, l0_ada_b)
    hn = norm_mod(x_lat, l0_norm1_g, mods, 0, 1, tm=min(512, tm), x_ctx=x_ctx, **geo)
    w_z = bf(jnp.pad(l0_w_in[:, 2 * MIX_A:], ((0, 0), (0, Z_WIDTH - B_WIDTH))))
    hp_a = matmul(hn, bf(l0_w_in[:, :2 * MIX_A]), out_dtype=BF16, tm=tm, tn=1024)
    hp_z = matmul(hn, w_z, tm=tm, tn=Z_WIDTH // 4)
    a_out = chunk_sgu(hp_a, l0_sgu_norm_g, l0_sgu_w, l0_sgu_b)
    zs = token_shift(hp_z, l0_shift_w, n_lat=n_lat, seq=seq, ctx_len=ctx_len)
    b_out = rwkv7_bidir(zs, l0_w0, l0_w_up, l0_a0, l0_a_up, l0_g_up, l0_k_k, l0_k_a, l0_r_k, l0_lnx_w, l0_lnx_b,
                        batch=batch, seq=seq, ctx_len=ctx_len)
    mixed = jnp.concatenate([a_out, b_out], axis=1)
    x1, hn = matmul(mixed, bf(l0_w_out), res=x_lat, res_ctx=x_ctx, mods=mods, gate_idx=2, geo=geo, tm=tm_res,
                    tn=d, norm=(l0_norm2_g, 3, 4))
    hff = matmul(hn, bf(l0_ffn_w1), w3=bf(l0_ffn_w3), out_dtype=BF16, tm=tm)
    x2 = matmul(hff, bf(l0_ffn_w2), res=x1, mods=mods, gate_idx=5, geo=geo, tm=tm, tn=1024,
                tk=l0_ffn_w2.shape[0] // 2)

    mods = ada(l1_ada_w, l1_ada_b)
    hn = norm_mod(x2, l1_norm1_g, mods, 0, 1, tm=min(512, tm), **geo)
    qkv = matmul(hn, bf(l1_w_qkv), out_dtype=BF16, tm=tm, tn=1024)
    lam_params = jnp.stack([l1_lam_q1, l1_lam_k1, l1_lam_q2, l1_lam_k2])
    lam_init = 0.8 - 0.6 * math.exp(-0.3 * 1)
    o = diff_attention(qkv, l1_q_norm_g, l1_k_norm_g, lam_params, l1_subln_g, lam_init,
                       batch=batch, seq=seq, ctx_len=ctx_len)
    x3 = matmul(o, bf(l1_w_out), res=x2, mods=mods, gate_idx=2, geo=geo, tm=min(512, tm), tn=d)
    x4 = moe_layer(x3, l1_norm2_g, mods, l1_router, l1_exp_w1, l1_exp_w3, l1_exp_w2, seq=seq, batch=batch)
    return x4.reshape(batch, seq, d)
```

```python
import functools
import math

import jax
import jax.numpy as jnp
from jax import lax
from jax.experimental import pallas as pl
from jax.experimental.pallas import tpu as pltpu

F32 = jnp.float32
BF16 = jnp.bfloat16

D_MODEL = 2048
GRID_W = 64
EPS = 1e-6

MIX_A = 1024
SGU_CHUNK = 128
A_GROUPS = 8
MIX_B = 1024
B_HEAD = 64
B_HEADS = 16
DECAY_LORA = 64
AAA_LORA = 64
GATE_LORA = 160
B_WIDTH = 3 * MIX_B + DECAY_LORA + AAA_LORA + GATE_LORA
GN_EPS = B_HEAD * 1e-5

C_HEADS = 16
C_HEAD = 64
ROPE_AXIS = C_HEAD // 2
ROPE_THETA = 10000.0
SUBLN_EPS = 1e-5

N_EXPERTS = 8

LOG2E = 1.4426950408889634
LANES = 128
SUBLANES = 8
COND_ROWS = 16
VMEM_LIMIT = 52 * 1024 * 1024
GMM_VMEM_LIMIT = 58 * 1024 * 1024

RW_CHUNK = 64
INV_BLOCK = 16
DMA_UNROLL = 8
RW_TILE = 256
RW_CHUNKS_PER_ITER = 4
RW_PAIRS = B_HEADS // 2
Z_WIDTH = 3584


def _cparams(sem):
    return pltpu.CompilerParams(dimension_semantics=sem, vmem_limit_bytes=VMEM_LIMIT)


def _group_of_tile(i, tm, n_lat, seq, batch):
    return jnp.where(i * tm < n_lat, (i * tm) // seq, batch)


def _norm_mod_rows(x, gain, mod, s_idx, c_idx):
    ms = jnp.mean(x * x, axis=-1, keepdims=True)
    y = x * lax.rsqrt(ms + EPS) * gain
    return y * (1.0 + mod[c_idx:c_idx + 1, :]) + mod[s_idx:s_idx + 1, :]


def _tile_rows(ref, ctx_ref, lat_tiles):
    if ctx_ref is None:
        return ref[...]
    return jnp.where(pl.program_id(0) < lat_tiles, ref[...], ctx_ref[...])


def _split_row_specs(block, lat_tiles, col):
    lat = pl.BlockSpec(block, lambda i, *a: (jnp.minimum(i, lat_tiles - 1), col(*a)))
    ctx = pl.BlockSpec(block, lambda i, *a: (jnp.maximum(i - lat_tiles, 0), col(*a)))
    return [lat, ctx]


def _norm_mod_kernel(*refs, s_idx, c_idx, lat_tiles):
    if lat_tiles is None:
        x_ref, g_ref, mod_ref, o_ref = refs
        c_ref = None
    else:
        x_ref, c_ref, g_ref, mod_ref, o_ref = refs
    x = _tile_rows(x_ref, c_ref, lat_tiles)
    o_ref[...] = _norm_mod_rows(x, g_ref[...], mod_ref[0], s_idx, c_idx).astype(o_ref.dtype)


def norm_mod(x, gain, mods, s_idx, c_idx, *, n_lat, seq, batch, tm=512, x_ctx=None):
    d = x.shape[1]
    m = x.shape[0] + (0 if x_ctx is None else x_ctx.shape[0])
    grp = functools.partial(_group_of_tile, tm=tm, n_lat=n_lat, seq=seq, batch=batch)
    lat_tiles = None if x_ctx is None else n_lat // tm
    if x_ctx is None:
        x_specs, x_args = [pl.BlockSpec((tm, d), lambda i: (i, 0))], [x]
    else:
        x_specs, x_args = _split_row_specs((tm, d), lat_tiles, lambda: 0), [x, x_ctx]
    return pl.pallas_call(
        functools.partial(_norm_mod_kernel, s_idx=s_idx, c_idx=c_idx, lat_tiles=lat_tiles),
        grid=(m // tm,),
        in_specs=x_specs + [
            pl.BlockSpec((1, d), lambda i: (0, 0)),
            pl.BlockSpec((1, 6, d), lambda i: (grp(i), 0, 0)),
        ],
        out_specs=pl.BlockSpec((tm, d), lambda i: (i, 0)),
        out_shape=jax.ShapeDtypeStruct((m, d), BF16),
        compiler_params=_cparams(("parallel",)),
        name="norm_mod",
    )(*x_args, gain.reshape(1, d), mods)


def _mm_kernel(*refs, kind, nk, silu_x, has_bias, gate_idx, lat_tiles, norm_idx, has_x2):
    refs = list(refs)
    x_ref = refs.pop(0)
    w_refs = [refs.pop(0)]
    if kind == "swiglu":
        w_refs.append(refs.pop(0))
    x2_ref = w2_ref = None
    if has_x2:
        x2_ref = refs.pop(0)
        w2_ref = refs.pop(0)
    bias_ref = refs.pop(0) if has_bias else None
    res_ref = res_ctx_ref = mod_ref = gain_ref = hn_ref = None
    if kind == "res":
        res_ref = refs.pop(0)
        if lat_tiles is not None:
            res_ctx_ref = refs.pop(0)
        mod_ref = refs.pop(0)
        if norm_idx is not None:
            gain_ref = refs.pop(0)
    o_ref = refs.pop(0)
    if norm_idx is not None:
        hn_ref = refs.pop(0)
    acc_refs = refs

    x = x_ref[...]
    if silu_x:
        x = jax.nn.silu(x.astype(F32))
    x = x.astype(BF16)
    prods = [jnp.dot(x, w_ref[...].astype(BF16), preferred_element_type=F32) for w_ref in w_refs]
    if has_x2:
        prods[0] = prods[0] + jnp.dot(x2_ref[...].astype(BF16), w2_ref[...].astype(BF16),
                                      preferred_element_type=F32)

    def epilogue(vals):
        a = vals[0]
        if kind == "swiglu":
            a = jax.nn.silu(a) * vals[1]
        if has_bias:
            a = a + bias_ref[...]
        if kind == "res":
            a = _tile_rows(res_ref, res_ctx_ref, lat_tiles) + mod_ref[0, gate_idx:gate_idx + 1, :] * a
        o_ref[...] = a.astype(o_ref.dtype)
        if norm_idx is not None:
            hn_ref[...] = _norm_mod_rows(a, gain_ref[...], mod_ref[0], *norm_idx).astype(hn_ref.dtype)

    if nk == 1:
        epilogue(prods)
        return

    k = pl.program_id(2)

    @pl.when(k == 0)
    def _():
        for acc, p in zip(acc_refs, prods):
            acc[...] = p

    @pl.when(k > 0)
    def _():
        for acc, p in zip(acc_refs, prods):
            acc[...] += p

    @pl.when(k == nk - 1)
    def _():
        epilogue([acc[...] for acc in acc_refs])


def matmul(x, w, *, w3=None, bias=None, res=None, res_ctx=None, mods=None, gate_idx=0, out_dtype=F32,
           silu_x=False, tm=1024, tn=512, tk=None, geo=None, norm=None, x2=None):
    m, kdim = x.shape
    n = w.shape[1]
    tm = min(tm, m)
    tk = kdim if tk is None else tk
    nk = kdim // tk
    kind = "swiglu" if w3 is not None else ("res" if res is not None else "plain")
    in_specs = [pl.BlockSpec((tm, tk), lambda i, j, k: (i, k)),
                pl.BlockSpec((tk, tn), lambda i, j, k: (k, j))]
    args = [x, w]
    if x2 is not None:
        k2 = x2.shape[1]
        assert nk == 1 and w3 is None and kdim % k2 == 0 and w.shape[0] == kdim + k2
        in_specs += [pl.BlockSpec((tm, k2), lambda i, j, k: (i, 0)),
                     pl.BlockSpec((k2, tn), lambda i, j, k: (kdim // k2, j))]
        args += [x2, w]
    if w3 is not None:
        in_specs.append(pl.BlockSpec((tk, tn), lambda i, j, k: (k, j)))
        args.append(w3)
    if bias is not None:
        in_specs.append(pl.BlockSpec((1, tn), lambda i, j, k: (0, j)))
        args.append(bias.reshape(1, n))
    lat_tiles = None
    if res is not None:
        if res_ctx is None:
            in_specs.append(pl.BlockSpec((tm, tn), lambda i, j, k: (i, j)))
            args.append(res)
        else:
            lat_tiles = geo["n_lat"] // tm
            in_specs += _split_row_specs((tm, tn), lat_tiles, lambda j, k: j)
            args += [res, res_ctx]
        group = functools.partial(_group_of_tile, tm=tm, **geo)
        in_specs.append(pl.BlockSpec((1, 6, tn), lambda i, j, k: (group(i), 0, j)))
        args.append(mods)
    out_specs = pl.BlockSpec((tm, tn), lambda i, j, k: (i, j))
    out_shape = jax.ShapeDtypeStruct((m, n), out_dtype)
    norm_idx = None
    if norm is not None:
        assert kind == "res" and tn == n
        in_specs.append(pl.BlockSpec((1, n), lambda i, j, k: (0, 0)))
        args.append(norm[0].reshape(1, n))
        norm_idx = (norm[1], norm[2])
        out_specs = [out_specs, pl.BlockSpec((tm, tn), lambda i, j, k: (i, j))]
        out_shape = [out_shape, jax.ShapeDtypeStruct((m, n), BF16)]
    n_acc = 0 if nk == 1 else (2 if w3 is not None else 1)
    return pl.pallas_call(
        functools.partial(_mm_kernel, kind=kind, nk=nk, silu_x=silu_x, has_bias=bias is not None,
                          gate_idx=gate_idx, lat_tiles=lat_tiles, norm_idx=norm_idx, has_x2=x2 is not None),
        grid=(m // tm, n // tn, nk),
        in_specs=in_specs,
        out_specs=out_specs,
        out_shape=out_shape,
        scratch_shapes=[pltpu.VMEM((tm, tn), F32)] * n_acc,
        compiler_params=_cparams(("parallel", "parallel", "arbitrary")),
        name="matmul_" + kind,
    )(*args)


def _sgu_kernel(u_ref, v_ref, ng_ref, ws_ref, bs_ref, o_ref, *, chunks):
    for c in range(chunks):
        rows = slice(c * SGU_CHUNK, (c + 1) * SGU_CHUNK)
        for g in range(A_GROUPS):
            cols = slice(g * LANES, (g + 1) * LANES)
            vg = jax.nn.gelu(v_ref[rows, cols].astype(F32))
            ms = jnp.mean(vg * vg, axis=-1, keepdims=True)
            vn = vg * lax.rsqrt(ms + EPS) * ng_ref[:, cols]
            s = jnp.dot(ws_ref[g].astype(BF16), vn.astype(BF16), preferred_element_type=F32)
            u = jax.nn.gelu(u_ref[rows, cols].astype(F32))
            o_ref[rows, cols] = (u * (s + bs_ref[:, cols])).astype(o_ref.dtype)


def chunk_sgu(hp, norm_g, ws, bs, *, tm=256):
    m = hp.shape[0]
    bs_exp = jnp.repeat(bs.T, LANES, axis=1)
    return pl.pallas_call(
        functools.partial(_sgu_kernel, chunks=tm // SGU_CHUNK),
        grid=(m // tm,),
        in_specs=[
            pl.BlockSpec((tm, MIX_A), lambda i: (i, 0)),
            pl.BlockSpec((tm, MIX_A), lambda i: (i, 1)),
            pl.BlockSpec((1, MIX_A), lambda i: (0, 0)),
            pl.BlockSpec((A_GROUPS, SGU_CHUNK, SGU_CHUNK), lambda i: (0, 0, 0)),
            pl.BlockSpec((SGU_CHUNK, MIX_A), lambda i: (0, 0)),
        ],
        out_specs=pl.BlockSpec((tm, MIX_A), lambda i: (i, 0)),
        out_shape=jax.ShapeDtypeStruct((m, MIX_A), BF16),
        compiler_params=_cparams(("parallel",)),
        name="chunk_sgu",
    )(hp, hp, norm_g.reshape(1, MIX_A), ws, bs_exp)


def _shift_kernel(x_ref, p_ref, n_ref, w_ref, o_ref, *, ts, lat_tiles, tiles_per_lat, tiles_per_ctx):
    i = pl.program_id(0)
    is_lat = i < lat_tiles
    pos = jnp.where(is_lat, i % tiles_per_lat, (i - lat_tiles) % tiles_per_ctx)
    last = jnp.where(is_lat, tiles_per_lat - 1, tiles_per_ctx - 1)
    x = x_ref[...]
    prev_row = jnp.where(pos == 0, 0.0, p_ref[SUBLANES - 1:SUBLANES, :])
    next_row = jnp.where(pos == last, 0.0, n_ref[0:1, :])
    rows = lax.broadcasted_iota(jnp.int32, x.shape, 0)
    xm = jnp.where(rows == 0, prev_row, pltpu.roll(x, 1, axis=0))
    xp = jnp.where(rows == ts - 1, next_row, pltpu.roll(x, ts - 1, axis=0))
    o_ref[...] = (w_ref[0:1, :] * xm + w_ref[1:2, :] * x + w_ref[2:3, :] * xp).astype(o_ref.dtype)


def token_shift(z, shift_w, *, n_lat, seq, ctx_len, ts=256, tc=Z_WIDTH):
    m = z.shape[0]
    halo = SUBLANES
    nblk8 = m // halo
    w_pad = jnp.pad(shift_w, ((0, 0), (0, Z_WIDTH - B_WIDTH)))
    kern = functools.partial(_shift_kernel, ts=ts, lat_tiles=n_lat // ts, tiles_per_lat=seq // ts,
                             tiles_per_ctx=ctx_len // ts)
    return pl.pallas_call(
        kern,
        grid=(m // ts, Z_WIDTH // tc),
        in_specs=[
            pl.BlockSpec((ts, tc), lambda i, j: (i, j)),
            pl.BlockSpec((halo, tc), lambda i, j: (jnp.maximum(i * (ts // halo) - 1, 0), j)),
            pl.BlockSpec((halo, tc), lambda i, j: (jnp.minimum((i + 1) * (ts // halo), nblk8 - 1), j)),
            pl.BlockSpec((3, tc), lambda i, j: (0, j)),
        ],
        out_specs=pl.BlockSpec((ts, tc), lambda i, j: (i, j)),
        out_shape=jax.ShapeDtypeStruct((m, Z_WIDTH), BF16),
        compiler_params=_cparams(("parallel", "parallel")),
        name="token_shift",
    )(z, z, z, w_pad)


def _mask_lanes(x, m0):
    z = jnp.zeros_like(x)
    return jnp.concatenate([jnp.where(m0, x, z), jnp.where(m0, z, x)], axis=0)


def _dot(a, b):
    return jnp.dot(a.astype(BF16), b.astype(BF16), preferred_element_type=F32)


def _dot_nt(a, b):
    return lax.dot_general(a.astype(BF16), b.astype(BF16), (((1,), (1,)), ((), ())),
                           preferred_element_type=F32)


def _dot_tn(a, b):
    return lax.dot_general(a.astype(BF16), b.astype(BF16), (((0,), (0,)), ((), ())),
                           preferred_element_type=F32)


def _head_sums(x, m0):
    s0 = jnp.sum(jnp.where(m0, x, 0.0), axis=-1, keepdims=True)
    s1 = jnp.sum(jnp.where(m0, 0.0, x), axis=-1, keepdims=True)
    return jnp.where(m0, s0, s1)


def _rwkv_kernel(*refs, reverse, n_chunks):
    if reverse:
        (zr_ref, zk_ref, zv_ref, zl_ref, w0_ref, wt_ref, a0_ref, wa_ref, kk_ref, ka_ref, rk_ref,
         y0_ref, g_ref, bon0_ref, lnw_ref, lnb_ref, out_ref,
         s_ref, kk_s, bt_s, kd_s, rt_s, v_s, y_s, ge_s) = refs
    else:
        (zr_ref, zk_ref, zv_ref, zl_ref, w0_ref, wt_ref, a0_ref, wa_ref, kk_ref, ka_ref, rk_ref, wg_ref,
         y_out_ref, g_out_ref, bon_out_ref,
         s_ref, kk_s, bt_s, kd_s, rt_s, v_s, y_s, ge_s) = refs
    ch = RW_CHUNK
    j = pl.program_id(1)

    @pl.when(j == 0)
    def _():
        s_ref[...] = jnp.zeros_like(s_ref)

    zl = zl_ref[...].astype(F32)
    lora_in = zl[:, 0:LANES]
    wl = w0_ref[...] + _dot(jnp.tanh(lora_in), wt_ref[...])
    wlog = -(jnp.maximum(-wl, 0.0) + jnp.log(1.0 + jnp.exp(-jnp.abs(wl)))) - 0.5
    lw = -jnp.exp(wlog)
    a = jax.nn.sigmoid(a0_ref[...] + _dot(lora_in, wa_ref[...]))
    r = zr_ref[...].astype(F32)
    k = zk_ref[...].astype(F32)
    v = zv_ref[...].astype(F32)
    kd = k * (1.0 + (a - 1.0) * ka_ref[...])
    kraw = k * kk_ref[...]
    rkd = r * kd * rk_ref[...]
    if not reverse:
        g_out_ref[...] = _dot(jax.nn.sigmoid(zl[:, LANES:3 * LANES]), wg_ref[...])

    lane = lax.broadcasted_iota(jnp.int32, (1, LANES), 1)
    m0 = lane < B_HEAD
    ti = lax.broadcasted_iota(jnp.int32, (ch, ch), 0)
    si = lax.broadcasted_iota(jnp.int32, (ch, ch), 1)
    tri = ((si >= ti) if reverse else (si <= ti)).astype(BF16)

    kn_l, b_l = [], []
    for p in range(RW_PAIRS):
        cols = slice(p * LANES, (p + 1) * LANES)
        kp = kraw[:, cols]
        nrm = jnp.sqrt(_head_sums(kp * kp, m0))
        kn = kp / jnp.maximum(nrm, 1e-12)
        kn_l.append(kn)
        b_l.append(kn * a[:, cols])
        v_s[p] = v[:, cols]
        if not reverse:
            bon_out_ref[:, cols] = _head_sums(rkd[:, cols], m0) * v[:, cols]

    for c in range(n_chunks):
        rws = slice(c * ch, (c + 1) * ch)
        lw_c = lw[rws, :]
        hi = lw_c.astype(BF16)
        rem = lw_c - hi.astype(F32)
        mid = rem.astype(BF16)
        lo = (rem - mid.astype(F32)).astype(BF16)
        cs = (jnp.dot(tri, hi, preferred_element_type=F32) + jnp.dot(tri, mid, preferred_element_type=F32)
              + jnp.dot(tri, lo, preferred_element_type=F32))
        total = cs[0:1, :] if reverse else cs[ch - 1:ch, :]
        g_in = jnp.exp(cs)
        g_ex = jnp.exp(cs - lw_c)
        g_inv = jnp.exp(-cs)
        g_end = jnp.exp(total)
        for p in range(RW_PAIRS):
            cols = slice(p * LANES, (p + 1) * LANES)
            kk_s[p, rws, :] = kn_l[p][rws, :] * g_ex[:, cols]
            bt_s[p, rws, :] = b_l[p][rws, :] * g_inv[:, cols]
            kd_s[p, rws, :] = kd[rws, cols] * g_inv[:, cols]
            rt_s[p, rws, :] = r[rws, cols] * g_in[:, cols]
            ge_s[p, c:c + 1, :] = g_end[:, cols]

    ri = lax.broadcasted_iota(jnp.int32, (2 * ch, 2 * ch), 0)
    ci = lax.broadcasted_iota(jnp.int32, (2 * ch, 2 * ch), 1)
    same_head = (ri // ch) == (ci // ch)
    if reverse:
        strict = same_head & (ci > ri)
        incl = same_head & (ci >= ri)
    else:
        strict = same_head & (ci < ri)
        incl = same_head & (ci <= ri)
    eye = (ri == ci).astype(F32)
    zero = jnp.zeros((2 * ch, 2 * ch), F32)
    n_sq = int(math.log2(INV_BLOCK)) - 1
    blk_diag = []
    bs = INV_BLOCK
    while bs <= ch:
        blk_diag.append((ri // bs) == (ci // bs))
        bs *= 2

    def chunk_body(it, carry):
        cis = [it * RW_CHUNKS_PER_ITER + s for s in range(RW_CHUNKS_PER_ITER)]
        cs_ = [(n_chunks - 1 - ci) if reverse else ci for ci in cis]
        rows_l = [pl.ds(pl.multiple_of(c * ch, ch), ch) for c in cs_]
        units = [(s, p) for s in range(RW_CHUNKS_PER_ITER) for p in range(RW_PAIRS)]
        pairs = range(len(units))
        kk_t = [kk_s[p, rows_l[s], :] for s, p in units]
        rt = [rt_s[p, rows_l[s], :] for s, p in units]
        bt_m = [_mask_lanes(bt_s[p, rows_l[s], :], m0) for s, p in units]
        kd_m = [_mask_lanes(kd_s[p, rows_l[s], :], m0) for s, p in units]
        v_st = [_mask_lanes(v_s[p, rows_l[s], :], m0) for s, p in units]
        g_end = [ge_s[p, pl.ds(cs_[s], 1), :] for s, p in units]

        pq = [_dot_nt(jnp.concatenate([kk_t[p], kk_t[p], rt[p], rt[p]], axis=0),
                      jnp.concatenate([bt_m[p], kd_m[p]], axis=0)) for p in pairs]
        a_m = [jnp.where(strict, pq[p][0:2 * ch, 0:2 * ch], zero) for p in pairs]
        b_m = [jnp.where(strict, pq[p][0:2 * ch, 2 * ch:4 * ch], zero) for p in pairs]
        m2 = [jnp.where(incl, pq[p][2 * ch:4 * ch, 0:2 * ch], zero) for p in pairs]
        m1 = [jnp.where(incl, pq[p][2 * ch:4 * ch, 2 * ch:4 * ch], zero) for p in pairs]

        q = [jnp.where(blk_diag[0], -a_m[p], zero) for p in pairs]
        t_m = [eye + q[p] for p in pairs]
        q = [_dot(q[p], q[p]) for p in pairs]
        for _ in range(n_sq - 1):
            both = [_dot(jnp.concatenate([q[p], t_m[p]], axis=0), q[p]) for p in pairs]
            q = [both[p][0:2 * ch] for p in pairs]
            t_m = [t_m[p] + both[p][2 * ch:4 * ch] for p in pairs]
        t_m = [t_m[p] + _dot(t_m[p], q[p]) for p in pairs]
        for lvl in range(1, len(blk_diag)):
            off = [jnp.where(blk_diag[lvl] & ~blk_diag[lvl - 1], a_m[p], zero) for p in pairs]
            t_off = [_dot(t_m[p], off[p]) for p in pairs]
            t_m = [t_m[p] - _dot(t_off[p], t_m[p]) for p in pairs]

        bm1v = [_dot(jnp.concatenate([b_m[p], m1[p]], axis=0), v_st[p]) for p in pairs]
        ku = [_dot(t_m[p], jnp.concatenate([_mask_lanes(kk_t[p], m0), bm1v[p][0:2 * ch]], axis=1))
              for p in pairs]
        m2ku = [_dot(m2[p], ku[p]) for p in pairs]
        rp, y0 = [], []
        for p in pairs:
            rp_s = _mask_lanes(rt[p], m0) - m2ku[p][:, 0:LANES]
            y0_s = bm1v[p][2 * ch:4 * ch] - m2ku[p][:, LANES:2 * LANES]
            rp.append(rp_s[0:ch] + rp_s[ch:2 * ch])
            y0.append(y0_s[0:ch] + y0_s[ch:2 * ch])
        for s in range(RW_CHUNKS_PER_ITER):
            us = [s * RW_PAIRS + p for p in range(RW_PAIRS)]
            s_bd = [s_ref[p] for p in range(RW_PAIRS)]
            rs = [_dot_nt(jnp.concatenate([rp[u], ku[u][:, 0:LANES]], axis=0), s_bd[p])
                  for p, u in enumerate(us)]
            for p, u in enumerate(us):
                u_all = rs[p][ch:3 * ch] + ku[u][:, LANES:2 * LANES]
                upd = _dot_tn(jnp.concatenate([v_st[u], -u_all], axis=0),
                              jnp.concatenate([kd_m[u] * g_end[u], bt_m[u] * g_end[u]], axis=0))
                s_ref[p] = s_bd[p] * g_end[u] + upd
                y_s[p, rows_l[s], :] = rs[p][0:ch] + y0[u]
        return carry

    lax.fori_loop(0, n_chunks // RW_CHUNKS_PER_ITER, chunk_body, 0)

    for p in range(RW_PAIRS):
        cols = slice(p * LANES, (p + 1) * LANES)
        if not reverse:
            y_out_ref[:, cols] = y_s[p]
        else:
            bon1 = _head_sums(rkd[:, cols], m0) * v[:, cols]
            ysum = y0_ref[:, cols] + y_s[p]
            mu = _head_sums(ysum, m0) * (1.0 / B_HEAD)
            dlt = ysum - mu
            var = _head_sums(dlt * dlt, m0) * (1.0 / B_HEAD)
            yn = dlt * lax.rsqrt(var + GN_EPS)
            o = (yn * lnw_ref[:, cols] + lnb_ref[:, cols] + bon0_ref[:, cols] + bon1) * g_ref[:, cols]
            out_ref[:, cols] = o.astype(out_ref.dtype)


def rwkv7_bidir(zs, w0, w_up, a0, a_up, g_up, k_k, k_a, r_k, lnx_w, lnx_b, *, batch, seq, ctx_len):
    m = zs.shape[0]
    n_lat = batch * seq
    tb = RW_TILE
    assert seq % tb == 0 and ctx_len % tb == 0 and tb // RW_CHUNK <= SUBLANES
    nct, nlt = ctx_len // tb, seq // tb
    steps = nct + nlt

    def row_block(reverse):
        def f(b, j):
            jc = (nct - 1 - j) if reverse else j
            jl = (nlt - 1 - (j - nct)) if reverse else (j - nct)
            return jnp.where(j < nct, (n_lat + b * ctx_len) // tb + jc, b * nlt + jl)
        return f

    def pad_rows(w, top, total):
        return jnp.pad(w, ((top, total - top - w.shape[0]), (0, 0)))

    wg = pad_rows(g_up, 0, 2 * LANES)
    rk = r_k.reshape(1, MIX_B)
    vec = lambda a: a.reshape(1, MIX_B)
    scratch = ([pltpu.VMEM((RW_PAIRS, LANES, LANES), F32)] + [pltpu.VMEM((RW_PAIRS, tb, LANES), F32)] * 6
               + [pltpu.VMEM((RW_PAIRS, SUBLANES, LANES), F32)])
    const = lambda shape: pl.BlockSpec(shape, lambda b, j: (0,) * len(shape))

    def call(reverse, extra_in, extra_specs, out_shapes, out_specs, d):
        rb = row_block(reverse)
        wt = pad_rows(w_up[d], 0, LANES)
        wa = pad_rows(a_up[d], DECAY_LORA, LANES)
        in_specs = [
            pl.BlockSpec((tb, MIX_B), lambda b, j: (rb(b, j), 0)),
            pl.BlockSpec((tb, MIX_B), lambda b, j: (rb(b, j), 1)),
            pl.BlockSpec((tb, MIX_B), lambda b, j: (rb(b, j), 2)),
            pl.BlockSpec((tb, 512), lambda b, j: (rb(b, j), 6)),
            const((1, MIX_B)), const((LANES, MIX_B)), const((1, MIX_B)), const((LANES, MIX_B)),
            const((1, MIX_B)), const((1, MIX_B)), const((1, MIX_B)),
        ] + extra_specs
        args = [zs, zs, zs, zs, vec(w0[d]), wt, vec(a0[d]), wa, vec(k_k), vec(k_a), rk] + extra_in
        return pl.pallas_call(
            functools.partial(_rwkv_kernel, reverse=reverse, n_chunks=tb // RW_CHUNK),
            grid=(batch, steps),
            in_specs=in_specs,
            out_specs=out_specs,
            out_shape=out_shapes,
            scratch_shapes=scratch,
            compiler_params=_cparams(("parallel", "arbitrary")),
            name="rwkv7_rev" if reverse else "rwkv7_fwd",
        )(*args)

    rbf = row_block(False)
    tile_f = pl.BlockSpec((tb, MIX_B), lambda b, j: (rbf(b, j), 0))
    y0, g, bon0 = call(False, [wg], [const((2 * LANES, MIX_B))],
                       [jax.ShapeDtypeStruct((m, MIX_B), F32)] * 3, [tile_f] * 3, 0)
    rbr = row_block(True)
    tile_r = pl.BlockSpec((tb, MIX_B), lambda b, j: (rbr(b, j), 0))
    out = call(True, [y0, g, bon0, vec(lnx_w), vec(lnx_b)],
               [tile_r, tile_r, tile_r, const((1, MIX_B)), const((1, MIX_B))],
               jax.ShapeDtypeStruct((m, MIX_B), BF16), tile_r, 1)
    return out


def _half_rms(x, gain, m0):
    ms = _head_sums(x * x, m0) * (1.0 / C_HEAD)
    return x * lax.rsqrt(ms + EPS) * gain


def _rope(x, cos, sin):
    lane = lax.broadcasted_iota(jnp.int32, x.shape, 1)
    first = (lane % ROPE_AXIS) < (ROPE_AXIS // 2)
    half = ROPE_AXIS // 2
    rot = jnp.where(first, -pltpu.roll(x, LANES - half, axis=1), pltpu.roll(x, half, axis=1))
    return x * cos + rot * sin


def _attn_kernel(q_ref, kl_ref, vl_ref, kc_ref, vc_ref, cq_ref, sq_ref, ck_ref, sk_ref, qg_ref, kg_ref,
                 lam_ref, sg_ref, o_ref, k_s, v_s, *, ctx_len, lam_init, q_blk):
    qi = pl.program_id(2)
    lane = lax.broadcasted_iota(jnp.int32, (1, LANES), 1)
    m0 = lane < C_HEAD

    @pl.when(qi == 0)
    def _():
        kc = _half_rms(kc_ref[...].astype(F32), kg_ref[...], m0)
        kl = _rope(_half_rms(kl_ref[...].astype(F32), kg_ref[...], m0), ck_ref[...], sk_ref[...])
        k_s[0:ctx_len, :] = kc.astype(BF16)
        k_s[ctx_len:, :] = kl.astype(BF16)
        v_s[0:ctx_len, :] = vc_ref[...].astype(BF16)
        v_s[ctx_len:, :] = vl_ref[...].astype(BF16)

    lp = lam_ref[...]
    lam = (jnp.exp(jnp.sum(lp[0:1] * lp[1:2], keepdims=True)) - jnp.exp(jnp.sum(lp[2:3] * lp[3:4], keepdims=True))
           + lam_init)
    q = (_rope(_half_rms(q_ref[...].astype(F32), qg_ref[...], m0), cq_ref[...], sq_ref[...])
         * (C_HEAD ** -0.5 * LOG2E))
    keys = k_s[...]
    vals = v_s[...]
    zq = jnp.zeros_like(q)
    q_sub = [jnp.where(m0, q, zq), jnp.where(m0, zq, q)]
    blocks = [(slice(r0, r0 + q_blk), i) for r0 in range(0, q.shape[0], q_blk) for i in (0, 1)]
    lookahead = 2
    scores = [_dot_nt(q_sub[i][rows], keys) for rows, i in blocks[:lookahead]]
    outs = []
    for n, (rows, i) in enumerate(blocks):
        s = scores[n]
        e = jnp.exp2(s - jnp.max(s, axis=-1, keepdims=True))
        w = (lam if i else 1.0) / jnp.sum(e, axis=-1, keepdims=True)
        if n + lookahead < len(blocks):
            rows2, i2 = blocks[n + lookahead]
            scores.append(_dot_nt(q_sub[i2][rows2], keys))
        outs.append(jnp.dot(e.astype(BF16), vals, preferred_element_type=F32) * w)
    for n, r0 in enumerate(range(0, q.shape[0], q_blk)):
        o = outs[2 * n] - outs[2 * n + 1]
        ms = jnp.mean(o * o, axis=-1, keepdims=True)
        o = o * lax.rsqrt(ms + SUBLN_EPS) * sg_ref[...] * (1.0 - lam_init)
        o_ref[r0:r0 + q_blk, :] = o.astype(o_ref.dtype)


def _rope_tables(n):
    rows = n // GRID_W
    row = jnp.broadcast_to(jnp.arange(rows, dtype=F32)[:, None], (rows, GRID_W)).reshape(-1)
    col = jnp.broadcast_to(jnp.arange(GRID_W, dtype=F32)[None, :], (rows, GRID_W)).reshape(-1)
    inv = ROPE_THETA ** (-jnp.arange(0, ROPE_AXIS, 2, dtype=F32) / ROPE_AXIS)
    ar = row[:, None] * inv
    ac = col[:, None] * inv
    ang = jnp.concatenate([ar, ar, ac, ac, ar, ar, ac, ac], axis=-1)
    return jnp.cos(ang), jnp.sin(ang)


def diff_attention(qkv, q_g, k_g, lam_params, subln_g, lam_init, *, batch, seq, ctx_len, tq=1024):
    n_lat = batch * seq
    tq = min(tq, seq)
    cos, sin = _rope_tables(seq)
    nq = seq // tq
    hq, hk, hv = 0, D_MODEL // LANES, 2 * D_MODEL // LANES
    ctx_blk0 = n_lat // ctx_len
    two = lambda a: jnp.concatenate([a, a]).reshape(1, LANES)
    const = lambda shape: pl.BlockSpec(shape, lambda b, h, i: (0,) * len(shape))
    return pl.pallas_call(
        functools.partial(_attn_kernel, ctx_len=ctx_len, lam_init=lam_init, q_blk=min(128, tq)),
        grid=(batch, C_HEADS, nq),
        in_specs=[
            pl.BlockSpec((tq, LANES), lambda b, h, i: (b * nq + i, hq + h)),
            pl.BlockSpec((seq, LANES), lambda b, h, i: (b, hk + h)),
            pl.BlockSpec((seq, LANES), lambda b, h, i: (b, hv + h)),
            pl.BlockSpec((ctx_len, LANES), lambda b, h, i: (ctx_blk0 + b, hk + h)),
            pl.BlockSpec((ctx_len, LANES), lambda b, h, i: (ctx_blk0 + b, hv + h)),
            pl.BlockSpec((tq, LANES), lambda b, h, i: (i, 0)),
            pl.BlockSpec((tq, LANES), lambda b, h, i: (i, 0)),
            const((seq, LANES)), const((seq, LANES)),
            const((1, LANES)), const((1, LANES)), const((4, C_HEAD)), const((1, LANES)),
        ],
        out_specs=pl.BlockSpec((tq, LANES), lambda b, h, i: (b * nq + i, h)),
        out_shape=jax.ShapeDtypeStruct((n_lat, D_MODEL), BF16),
        scratch_shapes=[pltpu.VMEM((ctx_len + seq, LANES), BF16)] * 2,
        compiler_params=_cparams(("parallel", "parallel", "arbitrary")),
        name="diff_attention",
    )(qkv, qkv, qkv, qkv, qkv, cos, sin, cos, sin, two(q_g), two(k_g), lam_params, subln_g.reshape(1, LANES))


def _route_kernel(x_ref, g_ref, mod_ref, rw_ref, xn_ref, idx_ref, gate_ref, *, s_idx, c_idx):
    h = _norm_mod_rows(x_ref[...], g_ref[...], mod_ref[0], s_idx, c_idx)
    xn_ref[...] = h
    logits = jnp.dot(h, rw_ref[...], precision=lax.Precision.HIGHEST, preferred_element_type=F32)
    lane = lax.broadcasted_iota(jnp.int32, logits.shape, 1)
    neg = jnp.float32(-jnp.inf)
    lg = jnp.where(lane < N_EXPERTS, logits, neg)
    m1 = jnp.max(lg, axis=-1, keepdims=True)
    i1 = jnp.min(jnp.where(lg == m1, lane, LANES), axis=-1, keepdims=True)
    lg2 = jnp.where(lane == i1, neg, lg)
    m2 = jnp.max(lg2, axis=-1, keepdims=True)
    i2 = jnp.min(jnp.where(lg2 == m2, lane, LANES), axis=-1, keepdims=True)
    e2 = jnp.exp(m2 - m1)
    g1 = 1.0 / (1.0 + e2)
    g2 = e2 * g1
    idx_ref[...] = jnp.where(lane == 0, i1, jnp.where(lane == 1, i2, 0))
    gate_ref[...] = jnp.where(lane == 0, g1, jnp.where(lane == 1, g2, 0.0))


def route(x, gain, mods, router, s_idx, c_idx, *, seq, batch, tm=256):
    m, d = x.shape
    rw = jnp.pad(router, ((0, 0), (0, LANES - N_EXPERTS)))
    return pl.pallas_call(
        functools.partial(_route_kernel, s_idx=s_idx, c_idx=c_idx),
        grid=(m // tm,),
        in_specs=[
            pl.BlockSpec((tm, d), lambda i: (i, 0)),
            pl.BlockSpec((1, d), lambda i: (0, 0)),
            pl.BlockSpec((1, 6, d), lambda i: ((i * tm) // seq, 0, 0)),
            pl.BlockSpec((d, LANES), lambda i: (0, 0)),
        ],
        out_specs=[pl.BlockSpec((tm, d), lambda i: (i, 0)),
                   pl.BlockSpec((tm, LANES), lambda i: (i, 0)),
                   pl.BlockSpec((tm, LANES), lambda i: (i, 0))],
        out_shape=[jax.ShapeDtypeStruct((m, d), F32),
                   jax.ShapeDtypeStruct((m, LANES), jnp.int32),
                   jax.ShapeDtypeStruct((m, LANES), F32)],
        compiler_params=_cparams(("parallel",)),
        name="moe_route",
    )(x, gain.reshape(1, d), mods, rw)


def _row_copy(src_hbm, dst_ref, src_row, dst_row, sem):
    return pltpu.make_async_copy(src_hbm.at[pl.ds(src_row, 1)], dst_ref.at[pl.ds(dst_row, 1)], sem)


def _gather_kernel(used_ref, idx_ref, nxt_ref, src_hbm, o_ref, buf, sem, *, tg, n_tiles):
    i = pl.program_id(0)
    slot = i % 2

    def issue_tile(ids_ref, s):
        def issue(g, c):
            for u in range(DMA_UNROLL):
                r = g * DMA_UNROLL + u
                _row_copy(src_hbm, buf.at[s], ids_ref[0, 0, r], r, sem.at[s]).start(priority=u % 2)
            return c

        lax.fori_loop(0, tg // DMA_UNROLL, issue, 0)

    @pl.when((i == 0) & (used_ref[0] > 0))
    def _():
        issue_tile(idx_ref, slot)

    @pl.when((i + 1 < n_tiles) & ((i + 1) * tg < used_ref[0]))
    def _():
        issue_tile(nxt_ref, 1 - slot)

    @pl.when(i * tg >= used_ref[0])
    def _():
        o_ref[...] = jnp.zeros_like(o_ref)

    @pl.when(i * tg < used_ref[0])
    def _():
        def drain(r, c):
            _row_copy(src_hbm, buf.at[slot], 0, r, sem.at[slot]).wait()
            return c

        lax.fori_loop(0, tg, drain, 0, unroll=DMA_UNROLL)
        o_ref[...] = buf[slot].astype(o_ref.dtype)


def gather_rows(src, idx, rows_used, *, tg=256):
    mp = idx.shape[0]
    d = src.shape[1]
    n_tiles = mp // tg
    idx3 = idx.reshape(n_tiles, 1, tg)
    return pl.pallas_call(
        functools.partial(_gather_kernel, tg=tg, n_tiles=n_tiles),
        grid_spec=pltpu.PrefetchScalarGridSpec(
            num_scalar_prefetch=1,
            grid=(n_tiles,),
            in_specs=[pl.BlockSpec((1, 1, tg), lambda i, u: (i, 0, 0), memory_space=pltpu.SMEM),
                      pl.BlockSpec((1, 1, tg), lambda i, u: (jnp.minimum(i + 1, n_tiles - 1), 0, 0),
                                   memory_space=pltpu.SMEM),
                      pl.BlockSpec(memory_space=pl.ANY)],
            out_specs=pl.BlockSpec((tg, d), lambda i, u: (i, 0)),
            scratch_shapes=[pltpu.VMEM((2, tg, d), src.dtype), pltpu.SemaphoreType.DMA((2,))],
        ),
        out_shape=jax.ShapeDtypeStruct((mp, d), BF16),
        compiler_params=_cparams(("arbitrary",)),
        name="moe_gather",
    )(rows_used, idx3, idx3, src)


def _gmm_kernel(te_ref, nu_ref, *refs, kind, kc):
    if kind == "swiglu":
        x_ref, w1_ref, w3_ref, o_ref = refs
        w_refs = (w1_ref, w3_ref)
    else:
        x_ref, w1_ref, o_ref = refs
        w_refs = (w1_ref,)
    i = pl.program_id(1)

    @pl.when(i >= nu_ref[0])
    def _():
        o_ref[...] = jnp.zeros_like(o_ref)

    @pl.when(i < nu_ref[0])
    def _():
        kdim = x_ref.shape[1]
        accs = [None] * len(w_refs)
        for k0 in range(0, kdim, kc):
            x = x_ref[:, k0:k0 + kc].astype(BF16)
            for n, w_ref in enumerate(w_refs):
                p = jnp.dot(x, w_ref[0, k0:k0 + kc, :].astype(BF16), preferred_element_type=F32)
                accs[n] = p if accs[n] is None else accs[n] + p
        a = accs[0]
        if kind == "swiglu":
            a = jax.nn.silu(a) * accs[1]
        o_ref[...] = a.astype(o_ref.dtype)


def grouped_matmul(x, w, tile_expert, n_used, *, w3=None, out_dtype=F32, tm=512, tn=512, kc=1024):
    mp, kdim = x.shape
    n = w.shape[2]
    kind = "swiglu" if w3 is not None else "plain"
    w_spec = pl.BlockSpec((1, kdim, tn), lambda j, i, te, nu: (te[i], 0, j))
    in_specs = [pl.BlockSpec((tm, kdim), lambda j, i, te, nu: (i, 0)), w_spec]
    args = [x, w]
    if w3 is not None:
        in_specs.append(w_spec)
        args.append(w3)
    return pl.pallas_call(
        functools.partial(_gmm_kernel, kind=kind, kc=min(kc, kdim)),
        grid_spec=pltpu.PrefetchScalarGridSpec(
            num_scalar_prefetch=2,
            grid=(n // tn, mp // tm),
            in_specs=in_specs,
            out_specs=pl.BlockSpec((tm, tn), lambda j, i, te, nu: (i, j)),
        ),
        out_shape=jax.ShapeDtypeStruct((mp, n), out_dtype),
        compiler_params=pltpu.CompilerParams(dimension_semantics=("arbitrary", "arbitrary"),
                                             vmem_limit_bytes=GMM_VMEM_LIMIT),
        name="moe_gmm_" + kind,
    )(tile_expert, n_used, *args)


def _combine_kernel(pos_ref, nxt_ref, ys_hbm, x_ref, gate_ref, mod_ref, o_ref, buf, sem, *, tc, n_tiles, gate_idx):
    i = pl.program_id(0)
    slot = i % 2

    def issue_tile(p_ref, s):
        def issue(g, c):
            for u in range(DMA_UNROLL):
                r = g * DMA_UNROLL + u
                _row_copy(ys_hbm, buf.at[s, 0], p_ref[0, 0, 2 * r], r, sem.at[s]).start(priority=0)
                _row_copy(ys_hbm, buf.at[s, 1], p_ref[0, 0, 2 * r + 1], r, sem.at[s]).start(priority=1)
            return c

        lax.fori_loop(0, tc // DMA_UNROLL, issue, 0)

    @pl.when(i == 0)
    def _():
        issue_tile(pos_ref, slot)

    @pl.when(i + 1 < n_tiles)
    def _():
        issue_tile(nxt_ref, 1 - slot)

    def drain(r, c):
        _row_copy(ys_hbm, buf.at[slot, 0], 0, r, sem.at[slot]).wait()
        _row_copy(ys_hbm, buf.at[slot, 1], 0, r, sem.at[slot]).wait()
        return c

    lax.fori_loop(0, tc, drain, 0, unroll=DMA_UNROLL)
    g = gate_ref[...]
    moe = g[:, 0:1] * buf[slot, 0] + g[:, 1:2] * buf[slot, 1]
    o_ref[...] = x_ref[...] + mod_ref[0, gate_idx:gate_idx + 1, :] * moe


def moe_combine(ys, pos, x, gates, mods, gate_idx, *, seq, tc=256):
    m, d = x.shape
    n_tiles = m // tc
    pos3 = pos.reshape(n_tiles, 1, 2 * tc)
    return pl.pallas_call(
        functools.partial(_combine_kernel, tc=tc, n_tiles=n_tiles, gate_idx=gate_idx),
        grid=(n_tiles,),
        in_specs=[pl.BlockSpec((1, 1, 2 * tc), lambda i: (i, 0, 0), memory_space=pltpu.SMEM),
                  pl.BlockSpec((1, 1, 2 * tc), lambda i: (jnp.minimum(i + 1, n_tiles - 1), 0, 0),
                               memory_space=pltpu.SMEM),
                  pl.BlockSpec(memory_space=pl.ANY),
                  pl.BlockSpec((tc, d), lambda i: (i, 0)),
                  pl.BlockSpec((tc, LANES), lambda i: (i, 0)),
                  pl.BlockSpec((1, 6, d), lambda i: ((i * tc) // seq, 0, 0))],
        out_specs=pl.BlockSpec((tc, d), lambda i: (i, 0)),
        out_shape=jax.ShapeDtypeStruct((m, d), F32),
        scratch_shapes=[pltpu.VMEM((2, 2, tc, d), F32), pltpu.SemaphoreType.DMA((2,))],
        compiler_params=_cparams(("arbitrary",)),
        name="moe_combine",
    )(pos3, pos3, ys, x, gates, mods)


def moe_layer(x, gain, mods, router, w1, w3, w2, *, seq, batch, tm=512):
    n = x.shape[0]
    xn, idx, gates = route(x, gain, mods, router, 3, 4, seq=seq, batch=batch)
    e_flat = idx[:, 0:2].reshape(-1)
    onehot = (e_flat[:, None] == jnp.arange(N_EXPERTS)[None, :]).astype(jnp.int32)
    ranks = jnp.cumsum(onehot, axis=0) - onehot
    rank = jnp.sum(ranks * onehot, axis=1)
    counts = jnp.sum(onehot, axis=0)
    padded = ((counts + tm - 1) // tm) * tm
    starts = jnp.cumsum(padded) - padded
    pos = starts[e_flat] + rank
    mp = 2 * n + N_EXPERTS * tm
    token_of_row = jnp.zeros((mp,), jnp.int32).at[pos].set(jnp.arange(2 * n, dtype=jnp.int32) // 2)
    n_tiles = mp // tm
    ends = jnp.cumsum(padded)
    tile_start = jnp.arange(n_tiles, dtype=jnp.int32) * tm
    tile_expert = jnp.minimum(jnp.sum((tile_start[:, None] >= ends[None, :]).astype(jnp.int32), axis=1),
                              N_EXPERTS - 1).astype(jnp.int32)
    n_used = (ends[-1] // tm).astype(jnp.int32).reshape(1)

    xs = gather_rows(xn, token_of_row, ends[-1].astype(jnp.int32).reshape(1))
    hs = grouped_matmul(xs, w1, tile_expert, n_used, w3=w3, out_dtype=BF16, tm=tm, tn=1024)
    ys = grouped_matmul(hs, w2, tile_expert, n_used, out_dtype=F32, tm=tm, tn=512)
    return moe_combine(ys, pos.astype(jnp.int32), x, gates, mods, 5, seq=seq)


def kernel(x, c, ctx, c_ctx, l0_ada_w, l0_ada_b, l0_norm1_g, l0_norm2_g, l0_w_in, l0_sgu_norm_g, l0_sgu_w, l0_sgu_b, l0_shift_w, l0_w0, l0_w_up, l0_a0, l0_a_up, l0_g_up, l0_k_k, l0_k_a, l0_r_k, l0_lnx_w, l0_lnx_b, l0_w_out, l0_ffn_w1, l0_ffn_w3, l0_ffn_w2, l1_ada_w, l1_ada_b, l1_norm1_g, l1_norm2_g, l1_w_qkv, l1_q_norm_g, l1_k_norm_g, l1_lam_q1, l1_lam_k1, l1_lam_q2, l1_lam_k2, l1_subln_g, l1_w_out, l1_router, l1_exp_w1, l1_exp_w3, l1_exp_w2):
    batch, seq, d = x.shape
    ctx_len = ctx.shape[1]
    n_lat = batch * seq
    tm = min(1024, seq, batch * ctx_len)
    tm_res = min(256, tm)
    geo = dict(n_lat=n_lat, seq=seq, batch=batch)
    bf = lambda w: w.astype(BF16)

    x_lat = x.reshape(n_lat, d)
    x_ctx = ctx.reshape(batch * ctx_len, d)
    cond = jnp.concatenate([c, c_ctx[None, :], jnp.zeros((COND_ROWS - batch - 1, d), F32)], axis=0)

    def ada(w, b):
        return matmul(cond, w, bias=b, silu_x=True, tn=1024).reshape(COND_ROWS, 6, d)

    mods = ada(l0_ada_w, l0_ada_b)
    hn = norm_mod(x_lat, l0_norm1_g, mods, 0, 1, tm=min(512, tm), x_ctx=x_ctx, **geo)
    w_z = bf(jnp.pad(l0_w_in[:, 2 * MIX_A:], ((0, 0), (0, Z_WIDTH - B_WIDTH))))
    hp_a = matmul(hn, bf(l0_w_in[:, :2 * MIX_A]), out_dtype=BF16, tm=tm, tn=1024)
    hp_z = matmul(hn, w_z, tm=tm, tn=Z_WIDTH // 4)
    a_out = chunk_sgu(hp_a, l0_sgu_norm_g, l0_sgu_w, l0_sgu_b)
    zs = token_shift(hp_z, l0_shift_w, n_lat=n_lat, seq=seq, ctx_len=ctx_len)
    b_out = rwkv7_bidir(zs, l0_w0, l0_w_up, l0_a0, l0_a_up, l0_g_up, l0_k_k, l0_k_a, l0_r_k, l0_lnx_w, l0_lnx_b,
                        batch=batch, seq=seq, ctx_len=ctx_len)
    x1, hn = matmul(a_out, bf(l0_w_out), x2=b_out, res=x_lat, res_ctx=x_ctx, mods=mods, gate_idx=2, geo=geo, tm=tm_res,
                    tn=d, norm=(l0_norm2_g, 3, 4))
    hff = matmul(hn, bf(l0_ffn_w1), w3=bf(l0_ffn_w3), out_dtype=BF16, tm=tm)
    x2 = matmul(hff, bf(l0_ffn_w2), res=x1, mods=mods, gate_idx=5, geo=geo, tm=tm, tn=1024,
                tk=l0_ffn_w2.shape[0] // 2)

    mods = ada(l1_ada_w, l1_ada_b)
    hn = norm_mod(x2, l1_norm1_g, mods, 0, 1, tm=min(512, tm), **geo)
    qkv = matmul(hn, bf(l1_w_qkv), out_dtype=BF16, tm=tm, tn=1024)
    lam_params = jnp.stack([l1_lam_q1, l1_lam_k1, l1_lam_q2, l1_lam_k2])
    lam_init = 0.8 - 0.6 * math.exp(-0.3 * 1)
    o = diff_attention(qkv, l1_q_norm_g, l1_k_norm_g, lam_params, l1_subln_g, lam_init,
                       batch=batch, seq=seq, ctx_len=ctx_len)
    x3 = matmul(o, bf(l1_w_out), res=x2, mods=mods, gate_idx=2, geo=geo, tm=min(512, tm), tn=d)
    x4 = moe_layer(x3, l1_norm2_g, mods, l1_router, l1_exp_w1, l1_exp_w3, l1_exp_w2, seq=seq, batch=batch)
    return x4.reshape(batch, seq, d)
```
